```python
import jax, jax.numpy as jnp
from jax import lax
import numpy as np

D_MODEL = 2048
BATCH = 1
SEQ = 8192
DEPTH = 1

D_PLE = 256
N_Q_HEADS = 16
N_KV_HEADS = 4
HEAD_DIM = 64
WINDOW = 128
ATTN_BLOCK = 128
N_SG_HEADS = 8
SG_HEAD_DIM = 128
SG_CHUNK = 128
D_ATTN = N_Q_HEADS * HEAD_DIM
D_KV = N_KV_HEADS * HEAD_DIM
D_SG = N_SG_HEADS * SG_HEAD_DIM
D_MIX = D_ATTN + D_SG
D_IN = D_ATTN + 2 * D_KV + 2 * D_SG
N_EXPERTS = 32
TOP_K = 4
D_FF = D_MODEL
SWIGLU_LIMIT = 7.0
SWIGLU_ALPHA = 1.702
MOE_BLOCK = 128
EPS = 1e-6

kernel_name = "hymba_swa_sink_gmlp_moe_ple"


def rms_norm(x, g):
    xf = x.astype(jnp.float32)
    y = xf * lax.rsqrt(jnp.mean(xf * xf, axis=-1, keepdims=True) + EPS)
    return (y * g.astype(jnp.float32)).astype(x.dtype)


def layer_norm(x, g, b):
    xf = x.astype(jnp.float32)
    mu = jnp.mean(xf, axis=-1, keepdims=True)
    var = jnp.mean(jnp.square(xf - mu), axis=-1, keepdims=True)
    y = (xf - mu) * lax.rsqrt(var + EPS)
    return (y * g.astype(jnp.float32) + b.astype(jnp.float32)).astype(x.dtype)


def sliding_window_sink_attention(q, k, v, sinks):
    bsz, s = q.shape[0], q.shape[1]
    nb = s // ATTN_BLOCK
    grp = N_Q_HEADS // N_KV_HEADS
    qb = q.reshape(bsz, nb, ATTN_BLOCK, N_KV_HEADS, grp, HEAD_DIM)
    kb = k.reshape(bsz, nb, ATTN_BLOCK, N_KV_HEADS, HEAD_DIM)
    vb = v.reshape(bsz, nb, ATTN_BLOCK, N_KV_HEADS, HEAD_DIM)

    def with_prev(t):
        prev = jnp.pad(t[:, :-1], ((0, 0), (1, 0), (0, 0), (0, 0), (0, 0)))
        return jnp.concatenate([prev, t], axis=2)

    kk, vv = with_prev(kb), with_prev(vb)
    scores = jnp.einsum('bnqkgd,bnskd->bnkgqs', qb, kk).astype(jnp.float32) * (HEAD_DIM ** -0.5)
    qi = jnp.arange(ATTN_BLOCK)[:, None]
    sj = jnp.arange(2 * ATTN_BLOCK)[None, :]
    rel = sj - ATTN_BLOCK - qi
    band = (rel <= 0) & (rel > -WINDOW)
    has_prev = (jnp.arange(nb)[:, None, None] > 0) | (sj[None] >= ATTN_BLOCK)
    mask = band[None] & has_prev
    scores = jnp.where(mask[None, :, None, None], scores, jnp.float32(-1e30))
    sink = sinks.astype(jnp.float32).reshape(N_KV_HEADS, grp)[None, None, :, :, None, None]
    sink = jnp.broadcast_to(sink, scores.shape[:-1] + (1,))
    probs = jax.nn.softmax(jnp.concatenate([scores, sink], axis=-1), axis=-1)[..., :-1]
    out = jnp.einsum('bnkgqs,bnskd->bnqkgd', probs.astype(v.dtype), vv)
    return out.reshape(bsz, s, D_ATTN)


def chunked_spatial_gating(u, v, ln_g, ln_b, w_s, b_s):
    bsz, s = u.shape[0], u.shape[1]
    nc = s // SG_CHUNK
    u = jax.nn.gelu(u, approximate=False)
    v = layer_norm(jax.nn.gelu(v, approximate=False), ln_g, ln_b)
    vc = v.reshape(bsz, nc, SG_CHUNK, N_SG_HEADS, SG_HEAD_DIM)
    causal = jnp.tril(jnp.ones((SG_CHUNK, SG_CHUNK), dtype=bool))
    w = jnp.where(causal[None], w_s, jnp.zeros_like(w_s))
    mixed = jnp.einsum('gts,bnsgc->bntgc', w, vc) + b_s.T[None, None, :, :, None]
    return u * mixed.reshape(bsz, s, D_SG)


def moe_ffn(xn, w_router, b_router, w_gate_up, b_gate_up, w_down, b_down):
    bsz, s, d = xn.shape
    t = bsz * s
    xt = xn.reshape(t, d)
    logits = (xt @ w_router + b_router).astype(jnp.float32)
    top_val, top_idx = lax.top_k(logits, TOP_K)
    gates = jax.nn.softmax(top_val, axis=-1)
    tk = t * TOP_K
    e_flat = top_idx.reshape(tk).astype(jnp.int32)
    tok_flat = (jnp.arange(tk, dtype=jnp.int32) // TOP_K)
    g_flat = gates.reshape(tk)
    order = jnp.argsort(e_flat, stable=True)
    sorted_e = e_flat[order]
    counts = jnp.bincount(e_flat, length=N_EXPERTS).astype(jnp.int32)
    padded = (counts + MOE_BLOCK - 1) // MOE_BLOCK * MOE_BLOCK
    pad_end = jnp.cumsum(padded)
    pad_start = pad_end - padded
    start = jnp.cumsum(counts) - counts
    dest = pad_start[sorted_e] + jnp.arange(tk, dtype=jnp.int32) - start[sorted_e]
    nblk = -(-tk // MOE_BLOCK) + N_EXPERTS
    rows = nblk * MOE_BLOCK
    row_tok = jnp.full((rows,), t, dtype=jnp.int32).at[dest].set(tok_flat[order])
    row_gate = jnp.zeros((rows,), jnp.float32).at[dest].set(g_flat[order])
    blk_start = jnp.arange(nblk, dtype=jnp.int32) * MOE_BLOCK
    blk_expert = jnp.minimum(jnp.searchsorted(pad_end, blk_start, side='right'), N_EXPERTS - 1).astype(jnp.int32)
    x_pad = jnp.concatenate([xt, jnp.zeros((1, d), xt.dtype)], axis=0)
    xb = x_pad[row_tok].reshape(nblk, MOE_BLOCK, d)

    def expert_block(args):
        xblk, e = args
        h = xblk @ w_gate_up[e] + b_gate_up[e]
        gate, up = h[:, ::2], h[:, 1::2]
        gate = jnp.minimum(gate, SWIGLU_LIMIT)
        up = jnp.clip(up, -SWIGLU_LIMIT, SWIGLU_LIMIT)
        glu = gate * jax.nn.sigmoid(gate * SWIGLU_ALPHA)
        return ((up + 1) * glu) @ w_down[e] + b_down[e]

    yb = lax.map(expert_block, (xb, blk_expert))
    yw = yb.reshape(rows, d) * row_gate[:, None].astype(yb.dtype)
    y = jax.ops.segment_sum(yw, row_tok, num_segments=t + 1)[:t]
    return y.reshape(bsz, s, d)


def hybrid_layer(x, p_i, ln1_g, w_in, q_norm_g, k_norm_g, sinks, sg_ln_g, sg_ln_b,
                 w_spatial, b_spatial, attn_out_g, sg_out_g, w_out, ln2_g,
                 w_router, b_router, w_gate_up, b_gate_up, w_down, b_down,
                 w_ple, ple_norm_g, w_ple_gate):
    bsz, s, _ = x.shape
    hn = rms_norm(x, ln1_g)
    proj = hn @ w_in
    splits = [D_ATTN, D_ATTN + D_KV, D_ATTN + 2 * D_KV, D_ATTN + 2 * D_KV + D_SG]
    q, k, v, u, vg = jnp.split(proj, splits, axis=-1)
    q = rms_norm(q.reshape(bsz, s, N_Q_HEADS, HEAD_DIM), q_norm_g)
    k = rms_norm(k.reshape(bsz, s, N_KV_HEADS, HEAD_DIM), k_norm_g)
    v = v.reshape(bsz, s, N_KV_HEADS, HEAD_DIM)
    a = sliding_window_sink_attention(q, k, v, sinks)
    sg = chunked_spatial_gating(u, vg, sg_ln_g, sg_ln_b, w_spatial, b_spatial)
    merged = jnp.concatenate([rms_norm(a, attn_out_g), rms_norm(sg, sg_out_g)], axis=-1)
    h = x + merged @ w_out
    h = h + moe_ffn(rms_norm(h, ln2_g), w_router, b_router, w_gate_up, b_gate_up, w_down, b_down)
    gate = jax.nn.sigmoid(rms_norm(h, ple_norm_g) @ w_ple_gate)
    return h + (p_i @ w_ple) * gate


def setup_inputs(seed: int = 0) -> dict:
    key = jax.random.key(seed)
    ks = jax.random.split(key, 32)
    f32 = jnp.float32

    def nrm(k, shape, scale):
        return jax.random.normal(k, shape, f32) * scale

    def gain(k, shape):
        return 1.0 + 0.1 * jax.random.normal(k, shape, f32)

    L = DEPTH
    return {
        "x": jax.random.normal(ks[0], (BATCH, SEQ, D_MODEL), f32),
        "p": jax.random.normal(ks[1], (DEPTH, BATCH, SEQ, D_PLE), f32),
        "ln1_g": gain(ks[2], (L, D_MODEL)),
        "w_in": nrm(ks[3], (L, D_MODEL, D_IN), D_MODEL ** -0.5),
        "q_norm_g": gain(ks[4], (L, HEAD_DIM)),
        "k_norm_g": gain(ks[5], (L, HEAD_DIM)),
        "sinks": nrm(ks[6], (L, N_Q_HEADS), 1.0),
        "sg_ln_g": gain(ks[7], (L, D_SG)),
        "sg_ln_b": nrm(ks[8], (L, D_SG), 0.02),
        "w_spatial": nrm(ks[9], (L, N_SG_HEADS, SG_CHUNK, SG_CHUNK), 0.5 * SG_CHUNK ** -0.5),
        "b_spatial": gain(ks[10], (L, N_SG_HEADS, SG_CHUNK)),
        "attn_out_g": gain(ks[11], (L, D_ATTN)),
        "sg_out_g": gain(ks[12], (L, D_SG)),
        "w_out": nrm(ks[13], (L, D_MIX, D_MODEL), D_MIX ** -0.5),
        "ln2_g": gain(ks[14], (L, D_MODEL)),
        "w_router": nrm(ks[15], (L, D_MODEL, N_EXPERTS), D_MODEL ** -0.5),
        "b_router": nrm(ks[16], (L, N_EXPERTS), 0.01),
        "w_gate_up": nrm(ks[17], (L, N_EXPERTS, D_MODEL, 2 * D_FF), D_MODEL ** -0.5),
        "b_gate_up": nrm(ks[18], (L, N_EXPERTS, 2 * D_FF), 0.02),
        "w_down": nrm(ks[19], (L, N_EXPERTS, D_FF, D_MODEL), D_FF ** -0.5),
        "b_down": nrm(ks[20], (L, N_EXPERTS, D_MODEL), 0.02),
        "w_ple": nrm(ks[21], (L, D_PLE, D_MODEL), D_PLE ** -0.5),
        "ple_norm_g": gain(ks[22], (L, D_MODEL)),
        "w_ple_gate": nrm(ks[23], (L, D_MODEL, D_MODEL), D_MODEL ** -0.5),
    }


def reference(x, p, ln1_g, w_in, q_norm_g, k_norm_g, sinks, sg_ln_g, sg_ln_b,
              w_spatial, b_spatial, attn_out_g, sg_out_g, w_out, ln2_g,
              w_router, b_router, w_gate_up, b_gate_up, w_down, b_down,
              w_ple, ple_norm_g, w_ple_gate):
    h = x
    for i in range(DEPTH):
        h = hybrid_layer(h, p[i], ln1_g[i], w_in[i], q_norm_g[i], k_norm_g[i], sinks[i],
                         sg_ln_g[i], sg_ln_b[i], w_spatial[i], b_spatial[i],
                         attn_out_g[i], sg_out_g[i], w_out[i], ln2_g[i],
                         w_router[i], b_router[i], w_gate_up[i], b_gate_up[i],
                         w_down[i], b_down[i], w_ple[i], ple_norm_g[i], w_ple_gate[i])
    return h
```

```python
import functools

import jax
import jax.numpy as jnp
import numpy as np
from jax import lax
from jax.experimental import pallas as pl
from jax.experimental.pallas import tpu as pltpu

F32 = jnp.float32
BF16 = jnp.bfloat16
I32 = jnp.int32

D_MODEL = 2048
D_PLE = 256
N_Q_HEADS = 16
N_KV_HEADS = 4
HEAD_DIM = 64
WINDOW = 128
N_SG_HEADS = 8
SG_HEAD_DIM = 128
D_ATTN = N_Q_HEADS * HEAD_DIM
D_KV = N_KV_HEADS * HEAD_DIM
D_SG = N_SG_HEADS * SG_HEAD_DIM
D_MIX = D_ATTN + D_SG
D_IN = D_ATTN + 2 * D_KV + 2 * D_SG
N_EXPERTS = 32
TOP_K = 4
D_FF = D_MODEL
SWIGLU_LIMIT = 7.0
SWIGLU_ALPHA = 1.702
EPS = 1e-6
NEG_INF = -1e30

SUBLANES = 8
LANES = 128
VMEM_LIMIT_BYTES = 56 * 1024 * 1024

IN_BM = 512
IN_BN = 1792
MIX_R = 256
ROUTE_TB = 512
ROW_TB = 256
MOE_SUB = 256
MOE_BM = 2048
MOE_F = 256
ROW_ALIGN = SUBLANES


def _rms(x, g):
    ms = jnp.mean(x * x, axis=-1, keepdims=True)
    return x * lax.rsqrt(ms + EPS) * g


def _gelu(x):
    return 0.5 * x * (1.0 + lax.erf(x * np.float32(np.sqrt(0.5))))


def _params(*sem):
    return pltpu.CompilerParams(dimension_semantics=sem, vmem_limit_bytes=VMEM_LIMIT_BYTES)


def _in_proj_kernel(x_ref, g_ref, w_ref, o_ref, hn_ref):
    @pl.when(pl.program_id(1) == 0)
    def _():
        hn_ref[...] = _rms(x_ref[...], g_ref[...]).astype(BF16)

    o_ref[...] = jnp.dot(hn_ref[...], w_ref[...], preferred_element_type=F32)


def _in_proj(x2, ln1_g, w_in_bf):
    t = x2.shape[0]
    return pl.pallas_call(
        _in_proj_kernel,
        grid=(t // IN_BM, D_IN // IN_BN),
        in_specs=[
            pl.BlockSpec((IN_BM, D_MODEL), lambda i, j: (i, 0)),
            pl.BlockSpec((1, D_MODEL), lambda i, j: (0, 0)),
            pl.BlockSpec((D_MODEL, IN_BN), lambda i, j: (0, j)),
        ],
        out_specs=pl.BlockSpec((IN_BM, IN_BN), lambda i, j: (i, j)),
        out_shape=jax.ShapeDtypeStruct((t, D_IN), F32),
        scratch_shapes=[pltpu.VMEM((IN_BM, D_MODEL), BF16)],
        compiler_params=_params("arbitrary", "arbitrary"),
        name="in_proj",
    )(x2, ln1_g.reshape(1, D_MODEL), w_in_bf)


def _head_mean_sq(t, bd):
    tt = t * t
    hi = tt.astype(BF16)
    lo = (tt - hi.astype(F32)).astype(BF16)
    s = jnp.dot(hi, bd, preferred_element_type=F32) + jnp.dot(lo, bd, preferred_element_type=F32)
    return s * (1.0 / HEAD_DIM)


def _mixer_kernel(sinks_ref, proj_ref, pkv_ref, x_ref, qg_ref, kg_ref, bd_ref, lng_ref, lnb_ref,
                  wsp_ref, bsp_ref, ag_ref, sgg_ref, wout_ref, ln2_ref,
                  h_ref, xn2_ref, a_ref, merged_ref):
    i = pl.program_id(0)
    nsb = MIX_R // WINDOW
    grp = N_Q_HEADS // N_KV_HEADS

    bd = bd_ref[...]
    kv_all = jnp.concatenate([pkv_ref[...], proj_ref[:, D_ATTN:D_ATTN + 2 * D_KV]], axis=0)
    k_all = kv_all[:, :D_KV]
    kn = (k_all * lax.rsqrt(_head_mean_sq(k_all, bd[:D_KV, :D_KV]) + EPS) * kg_ref[...]).astype(BF16)
    vb = kv_all[:, D_KV:].astype(BF16)
    q_all = proj_ref[:, :D_ATTN]
    qn = (q_all * lax.rsqrt(_head_mean_sq(q_all, bd) + EPS) * qg_ref[...] * (HEAD_DIM ** -0.5)).astype(BF16)

    rows = lax.broadcasted_iota(I32, (grp * WINDOW, 2 * WINDOW), 0)
    cols = lax.broadcasted_iota(I32, (grp * WINDOW, 2 * WINDOW), 1)
    rel = cols - (rows & (WINDOW - 1))
    band = (rel >= 1) & (rel <= WINDOW)
    cur = cols >= WINDOW
    rblk = lax.broadcasted_iota(I32, (grp * WINDOW, 1), 0) // WINDOW

    tr = lax.broadcasted_iota(I32, (WINDOW, WINDOW), 0)
    tc = lax.broadcasted_iota(I32, (WINDOW, WINDOW), 1)
    wsp = [jnp.where(tc <= tr, wsp_ref[g], 0.0).astype(BF16) for g in range(N_SG_HEADS)]

    for sb in range(nsb):
        r0 = sb * WINDOW
        has_prev = (i * nsb + sb) > 0
        ok = band & (cur | has_prev)
        for g in range(N_KV_HEADS):
            kc = kn[r0:r0 + 2 * WINDOW, g * HEAD_DIM:(g + 1) * HEAD_DIM]
            vc = vb[r0:r0 + 2 * WINDOW, g * HEAD_DIM:(g + 1) * HEAD_DIM]
            qg = jnp.concatenate(
                [qn[r0:r0 + WINDOW, (g * grp + j) * HEAD_DIM:(g * grp + j + 1) * HEAD_DIM] for j in range(grp)],
                axis=0)
            s = lax.dot_general(qg, kc, (((1,), (1,)), ((), ())), preferred_element_type=F32)
            s = jnp.where(ok, s, NEG_INF)
            sink = jnp.full((grp * WINDOW, 1), sinks_ref[g * grp], F32)
            for j in range(1, grp):
                sink = jnp.where(rblk == j, sinks_ref[g * grp + j], sink)
            m = jnp.maximum(jnp.max(s, axis=-1, keepdims=True), sink)
            p = jnp.exp(s - m)
            den = jnp.sum(p, axis=-1, keepdims=True) + jnp.exp(sink - m)
            o = jnp.dot((p / den).astype(BF16), vc, preferred_element_type=F32)
            for j in range(grp):
                hh = g * grp + j
                a_ref[r0:r0 + WINDOW, hh * HEAD_DIM:(hh + 1) * HEAD_DIM] = o[j * WINDOW:(j + 1) * WINDOW]
        merged_ref[r0:r0 + WINDOW, :D_ATTN] = _rms(a_ref[r0:r0 + WINDOW, :], ag_ref[...]).astype(BF16)

        u = _gelu(proj_ref[r0:r0 + WINDOW, D_ATTN + 2 * D_KV:D_ATTN + 2 * D_KV + D_SG])
        vg = _gelu(proj_ref[r0:r0 + WINDOW, D_ATTN + 2 * D_KV + D_SG:])
        mu = jnp.mean(vg, axis=-1, keepdims=True)
        vc0 = vg - mu
        var = jnp.mean(vc0 * vc0, axis=-1, keepdims=True)
        vn = (vc0 * lax.rsqrt(var + EPS) * lng_ref[...] + lnb_ref[...]).astype(BF16)
        mixed = jnp.concatenate(
            [jnp.dot(wsp[g], vn[:, g * SG_HEAD_DIM:(g + 1) * SG_HEAD_DIM], preferred_element_type=F32)
             for g in range(N_SG_HEADS)], axis=1) + bsp_ref[...]
        merged_ref[r0:r0 + WINDOW, D_ATTN:] = _rms(u * mixed, sgg_ref[...]).astype(BF16)

    hblk = x_ref[...] + jnp.dot(merged_ref[...], wout_ref[...], preferred_element_type=F32)
    h_ref[...] = hblk
    xn2_ref[...] = _rms(hblk, ln2_ref[...])


def _mixer(proj, x2, sinks, q_norm_g, k_norm_g, sg_ln_g, sg_ln_b, w_spatial, b_spatial,
           attn_out_g, sg_out_g, w_out_bf, ln2_g):
    t = x2.shape[0]
    nsb = MIX_R // WINDOW
    kvw = 2 * D_KV
    qg = jnp.tile(q_norm_g, N_Q_HEADS).reshape(1, D_ATTN)
    kg = jnp.tile(k_norm_g, N_KV_HEADS).reshape(1, D_KV)
    bd = jnp.asarray(np.kron(np.eye(N_Q_HEADS), np.ones((HEAD_DIM, HEAD_DIM))), BF16)
    bsp = jnp.repeat(b_spatial.T, SG_HEAD_DIM, axis=1)
    row = lambda v, d: v.reshape(1, d)
    const = lambda shape: pl.BlockSpec(shape, lambda i, s: tuple(0 for _ in shape))
    grid_spec = pltpu.PrefetchScalarGridSpec(
        num_scalar_prefetch=1,
        grid=(t // MIX_R,),
        in_specs=[
            pl.BlockSpec((MIX_R, D_IN), lambda i, s: (i, 0)),
            pl.BlockSpec((WINDOW, kvw), lambda i, s: (jnp.maximum(i * nsb - 1, 0), D_ATTN // kvw)),
            pl.BlockSpec((MIX_R, D_MODEL), lambda i, s: (i, 0)),
            const((1, D_ATTN)), const((1, D_KV)), const((D_ATTN, D_ATTN)),
            const((1, D_SG)), const((1, D_SG)),
            const((N_SG_HEADS, WINDOW, WINDOW)), const((WINDOW, D_SG)),
            const((1, D_ATTN)), const((1, D_SG)),
            const((D_MIX, D_MODEL)), const((1, D_MODEL)),
        ],
        out_specs=[
            pl.BlockSpec((MIX_R, D_MODEL), lambda i, s: (i, 0)),
            pl.BlockSpec((MIX_R, D_MODEL), lambda i, s: (i, 0)),
        ],
        scratch_shapes=[pltpu.VMEM((MIX_R, D_ATTN), F32), pltpu.VMEM((MIX_R, D_MIX), BF16)],
    )
    return pl.pallas_call(
        _mixer_kernel,
        grid_spec=grid_spec,
        out_shape=[jax.ShapeDtypeStruct((t, D_MODEL), F32), jax.ShapeDtypeStruct((t, D_MODEL), F32)],
        compiler_params=_params("arbitrary"),
        name="mixer",
    )(sinks, proj, proj, x2, qg, kg, bd, row(sg_ln_g, D_SG), row(sg_ln_b, D_SG), w_spatial, bsp,
      row(attn_out_g, D_ATTN), row(sg_out_g, D_SG), w_out_bf, row(ln2_g, D_MODEL))


def _router_kernel(xn_ref, wr_ref, br_ref, idx_ref, gate_ref, rank_ref, cnt_ref, carry_ref):
    i = pl.program_id(0)
    tb = ROUTE_TB

    @pl.when(i == 0)
    def _():
        carry_ref[...] = jnp.zeros_like(carry_ref)

    logits = lax.dot_general(wr_ref[...], xn_ref[...], (((1,), (1,)), ((), ())),
                             precision=lax.Precision.HIGHEST, preferred_element_type=F32) + br_ref[...]
    eid = lax.broadcasted_iota(I32, (N_EXPERTS, tb), 0)
    work = logits
    vals, idxs = [], []
    for _ in range(TOP_K):
        m = jnp.max(work, axis=0, keepdims=True)
        sel = jnp.min(jnp.where(work == m, eid, N_EXPERTS), axis=0, keepdims=True)
        vals.append(m)
        idxs.append(sel)
        work = jnp.where(eid == sel, -jnp.inf, work)
    ex = [jnp.exp(v - vals[0]) for v in vals]
    den = ex[0] + ex[1] + ex[2] + ex[3]
    chosen = (eid == idxs[0]) | (eid == idxs[1]) | (eid == idxs[2]) | (eid == idxs[3])
    onehot = jnp.where(chosen, 1.0, 0.0)

    tr = lax.broadcasted_iota(I32, (tb, tb), 0)
    tc = lax.broadcasted_iota(I32, (tb, tb), 1)
    before = jnp.where(tr < tc, 1.0, 0.0).astype(BF16)
    cnt = jnp.dot(onehot.astype(BF16), before, preferred_element_type=F32) + carry_ref[:, :1]
    for k in range(TOP_K):
        idx_ref[k:k + 1, :] = idxs[k]
        gate_ref[k:k + 1, :] = ex[k] / den
        rank_ref[k:k + 1, :] = jnp.sum(jnp.where(eid == idxs[k], cnt, 0.0), axis=0, keepdims=True).astype(I32)
    carry_ref[...] = carry_ref[...] + jnp.sum(onehot, axis=1, keepdims=True)
    cnt_ref[...] = carry_ref[...].astype(I32)


def _router(xn2, w_router, b_router):
    t = xn2.shape[0]
    tb = ROUTE_TB
    return pl.pallas_call(
        _router_kernel,
        grid=(t // tb,),
        in_specs=[
            pl.BlockSpec((tb, D_MODEL), lambda i: (i, 0)),
            pl.BlockSpec((N_EXPERTS, D_MODEL), lambda i: (0, 0)),
            pl.BlockSpec((N_EXPERTS, 1), lambda i: (0, 0)),
        ],
        out_specs=[
            pl.BlockSpec((TOP_K, tb), lambda i: (0, i)),
            pl.BlockSpec((TOP_K, tb), lambda i: (0, i)),
            pl.BlockSpec((TOP_K, tb), lambda i: (0, i)),
            pl.BlockSpec((N_EXPERTS, LANES), lambda i: (0, 0)),
        ],
        out_shape=[
            jax.ShapeDtypeStruct((TOP_K, t), I32),
            jax.ShapeDtypeStruct((TOP_K, t), F32),
            jax.ShapeDtypeStruct((TOP_K, t), I32),
            jax.ShapeDtypeStruct((N_EXPERTS, LANES), I32),
        ],
        scratch_shapes=[pltpu.VMEM((N_EXPERTS, LANES), F32)],
        compiler_params=_params("arbitrary"),
        name="router",
    )(xn2, w_router.T, b_router.reshape(N_EXPERTS, 1))


def _row_copy(src, s, dst, d, sem):
    return pltpu.make_async_copy(src.at[pl.ds(s, 1)], dst.at[pl.ds(d, 1)], sem)


def _tile_copy(src, dst, d8, sem):
    return pltpu.make_async_copy(src, dst.at[pl.ds(pl.multiple_of(d8 * SUBLANES, SUBLANES), SUBLANES)], sem)


def _dispatch_kernel(dest_ref, zmeta_ref, xn_ref, xs_ref, zero_ref, sem, zsem, *, t):
    i = pl.program_id(0)
    tb = ROW_TB

    @pl.when(i == 0)
    def _():
        zero_ref[...] = jnp.zeros_like(zero_ref)

        def pads(start):
            def per_expert(e, c):
                def per_row(r, c2):
                    cp = _row_copy(zero_ref, 0, xs_ref, zmeta_ref[e] + r, zsem)
                    cp.start() if start else cp.wait()
                    return c2
                return lax.fori_loop(0, zmeta_ref[N_EXPERTS + e], per_row, c)
            lax.fori_loop(0, N_EXPERTS, per_expert, 0)

            def per_tile(r, c):
                cp = _tile_copy(zero_ref, xs_ref, zmeta_ref[2 * N_EXPERTS] + r, zsem)
                cp.start() if start else cp.wait()
                return c
            lax.fori_loop(0, zmeta_ref[2 * N_EXPERTS + 1], per_tile, 0)

        pads(True)
        pads(False)

    def issue(tok, c):
        for k in range(TOP_K):
            _row_copy(xn_ref, tok, xs_ref, dest_ref[k * t + i * tb + tok], sem).start()
        return c
    lax.fori_loop(0, tb, issue, 0)

    def drain(tok, c):
        for k in range(TOP_K):
            _row_copy(xn_ref, tok, xs_ref, dest_ref[k * t + i * tb + tok], sem).wait()
        return c
    lax.fori_loop(0, tb, drain, 0)


def _dispatch(dest_flat, zmeta, xn2, n_rows):
    t = xn2.shape[0]
    grid_spec = pltpu.PrefetchScalarGridSpec(
        num_scalar_prefetch=2,
        grid=(t // ROW_TB,),
        in_specs=[pl.BlockSpec((ROW_TB, D_MODEL), lambda i, d, z: (i, 0))],
        out_specs=pl.BlockSpec(memory_space=pl.ANY),
        scratch_shapes=[pltpu.VMEM((SUBLANES, D_MODEL), F32), pltpu.SemaphoreType.DMA, pltpu.SemaphoreType.DMA],
    )
    return pl.pallas_call(
        functools.partial(_dispatch_kernel, t=t),
        grid_spec=grid_spec,
        out_shape=jax.ShapeDtypeStruct((n_rows, D_MODEL), F32),
        compiler_params=_params("arbitrary"),
        name="dispatch",
    )(dest_flat, zmeta, xn2)


def _moe_kernel(ie_ref, ist_ref, ins_ref, tail_ref, xs_ref, wgu_ref, bgu_ref, wd_ref, bd_ref, ys_ref,
                xst_ref, xbf_ref, acc_ref, wgu_bf_ref, wdmix_ref, wd_bf_ref, zero_ref, xsem, osem, zsem,
                *, n_items, n_chunks):
    it = pl.program_id(0)
    j = pl.program_id(1)
    nsub = ins_ref[it]
    start = ist_ref[it]
    f = MOE_F
    half = f // 2

    def x_copy(item_start, s, slot):
        r = pl.multiple_of(item_start + s * MOE_SUB, SUBLANES)
        return pltpu.make_async_copy(xs_ref.at[pl.ds(r, MOE_SUB)], xst_ref.at[slot], xsem.at[slot])

    def y_copy(item_start, s):
        r = pl.multiple_of(item_start + s * MOE_SUB, SUBLANES)
        return pltpu.make_async_copy(acc_ref.at[pl.ds(s * MOE_SUB, MOE_SUB)], ys_ref.at[pl.ds(r, MOE_SUB)], osem)

    @pl.when((it == 0) & (j == 0))
    def _():
        for s in range(2):
            @pl.when(s < nsub)
            def _():
                x_copy(start, s, s).start()

    @pl.when(j == 0)
    def _():
        @pl.when(it > 0)
        def _():
            def drain(s, c):
                y_copy(ist_ref[it - 1], s).wait()
                return c
            lax.fori_loop(0, ins_ref[it - 1], drain, 0)

        def stage(s, c):
            slot = s & 1
            x_copy(start, s, slot).wait()
            xbf_ref[pl.ds(pl.multiple_of(s * MOE_SUB, MOE_SUB), MOE_SUB), :] = xst_ref[slot].astype(BF16)

            @pl.when(s + 2 < nsub)
            def _():
                x_copy(start, s + 2, slot).start()
            return c
        lax.fori_loop(0, nsub, stage, 0)

    @pl.when((j == 1) & (it + 1 < n_items))
    def _():
        for s in range(2):
            @pl.when(s < ins_ref[it + 1])
            def _():
                x_copy(ist_ref[it + 1], s, s).start()

    @pl.when(nsub > 0)
    def _():
        wgu_bf_ref[...] = wgu_ref[0].astype(BF16)
        for c in range(D_MODEL // LANES):
            cs = slice(c * LANES, (c + 1) * LANES)
            wdmix_ref[c, pl.ds(0, half, stride=2), :] = wd_ref[0, :half, cs]
            wdmix_ref[c, pl.ds(1, half, stride=2), :] = wd_ref[0, half:, cs]
            wd_bf_ref[:, cs] = wdmix_ref[c].astype(BF16)

        even = (lax.broadcasted_iota(I32, (MOE_SUB, f), 1) & 1) == 0

        def sub(s, c):
            r = pl.multiple_of(s * MOE_SUB, MOE_SUB)
            h = jnp.dot(xbf_ref[pl.ds(r, MOE_SUB), :], wgu_bf_ref[...], preferred_element_type=F32) + bgu_ref[0]
            gate = jnp.minimum(h, SWIGLU_LIMIT)
            glu = gate * jax.nn.sigmoid(gate * SWIGLU_ALPHA)
            up = jnp.clip(h, -SWIGLU_LIMIT, SWIGLU_LIMIT) + 1.0
            prod = glu * pltpu.roll(up, 2 * f - 1, axis=1)
            act = jnp.where(even, prod[:, :f], pltpu.roll(prod, 1, axis=1)[:, f:])
            y = jnp.dot(act.astype(BF16), wd_bf_ref[...], preferred_element_type=F32)

            @pl.when(j == 0)
            def _():
                acc_ref[pl.ds(r, MOE_SUB), :] = y + bd_ref[0]

            @pl.when(j > 0)
            def _():
                acc_ref[pl.ds(r, MOE_SUB), :] += y

            @pl.when(j == n_chunks - 1)
            def _():
                y_copy(start, s).start()
            return c
        lax.fori_loop(0, nsub, sub, 0)

    @pl.when((it == n_items - 1) & (j == n_chunks - 1))
    def _():
        def drain(s, c):
            y_copy(start, s).wait()
            return c
        lax.fori_loop(0, nsub, drain, 0)

        zero_ref[...] = jnp.zeros_like(zero_ref)

        def fill(start_dma):
            def per_tile(r, c):
                cp = _tile_copy(zero_ref, ys_ref, tail_ref[0] + r, zsem)
                cp.start() if start_dma else cp.wait()
                return c
            lax.fori_loop(0, tail_ref[1], per_tile, 0)
        fill(True)
        fill(False)


def _moe(item_e, item_start, item_nsub, tail, xs, w_gate_up, b_gate_up, w_down, b_down):
    n_items = item_e.shape[0]
    n_rows = xs.shape[0]
    n_chunks = D_FF // MOE_F
    last = n_chunks - 1

    def chunk(it, j, ins):
        return jnp.where(ins[it] > 0, j, last)

    grid_spec = pltpu.PrefetchScalarGridSpec(
        num_scalar_prefetch=4,
        grid=(n_items, n_chunks),
        in_specs=[
            pl.BlockSpec(memory_space=pl.ANY),
            pl.BlockSpec((1, D_MODEL, 2 * MOE_F), lambda it, j, ie, ist, ins, tl: (ie[it], 0, chunk(it, j, ins))),
            pl.BlockSpec((1, 1, 2 * MOE_F), lambda it, j, ie, ist, ins, tl: (ie[it], 0, chunk(it, j, ins))),
            pl.BlockSpec((1, MOE_F, D_MODEL), lambda it, j, ie, ist, ins, tl: (ie[it], chunk(it, j, ins), 0)),
            pl.BlockSpec((1, 1, D_MODEL), lambda it, j, ie, ist, ins, tl: (ie[it], 0, 0)),
        ],
        out_specs=pl.BlockSpec(memory_space=pl.ANY),
        scratch_shapes=[
            pltpu.VMEM((2, MOE_SUB, D_MODEL), F32),
            pltpu.VMEM((MOE_BM, D_MODEL), BF16),
            pltpu.VMEM((MOE_BM, D_MODEL), F32),
            pltpu.VMEM((D_MODEL, 2 * MOE_F), BF16),
            pltpu.VMEM((D_MODEL // LANES, MOE_F, LANES), F32),
            pltpu.VMEM((MOE_F, D_MODEL), BF16),
            pltpu.VMEM((SUBLANES, D_MODEL), F32),
            pltpu.SemaphoreType.DMA((2,)),
            pltpu.SemaphoreType.DMA,
            pltpu.SemaphoreType.DMA,
        ],
    )
    return pl.pallas_call(
        functools.partial(_moe_kernel, n_items=n_items, n_chunks=n_chunks),
        grid_spec=grid_spec,
        out_shape=jax.ShapeDtypeStruct((n_rows, D_MODEL), F32),
        compiler_params=_params("arbitrary", "arbitrary"),
        name="moe",
    )(item_e, item_start, item_nsub, tail, xs, w_gate_up, b_gate_up.reshape(N_EXPERTS, 1, 2 * D_FF),
      w_down, b_down.reshape(N_EXPERTS, 1, D_MODEL))


def _combine_kernel(dest_ref, ys_ref, h_ref, gate_ref, p_ref, wple_ref, pg_ref, wpg_ref, o_ref, ybuf_ref, sem, *, t):
    i = pl.program_id(0)
    tb = ROW_TB

    def issue(tok, c):
        for k in range(TOP_K):
            pltpu.make_async_copy(ys_ref.at[pl.ds(dest_ref[k * t + i * tb + tok], 1)],
                                  ybuf_ref.at[k, pl.ds(tok, 1)], sem).start()
        return c
    lax.fori_loop(0, tb, issue, 0)

    def drain(tok, c):
        for k in range(TOP_K):
            pltpu.make_async_copy(ys_ref.at[pl.ds(dest_ref[k * t + i * tb + tok], 1)],
                                  ybuf_ref.at[k, pl.ds(tok, 1)], sem).wait()
        return c
    lax.fori_loop(0, tb, drain, 0)

    gates = gate_ref[...]
    h2 = h_ref[...]
    for k in range(TOP_K):
        h2 = h2 + ybuf_ref[k] * gates[:, k:k + 1]
    gate = jax.nn.sigmoid(jnp.dot(_rms(h2, pg_ref[...]).astype(BF16), wpg_ref[...], preferred_element_type=F32))
    o_ref[...] = h2 + jnp.dot(p_ref[...].astype(BF16), wple_ref[...], preferred_element_type=F32) * gate


def _combine(dest_flat, ys, h, gates_tk, p2, w_ple_bf, ple_norm_g, w_pg_bf):
    t = h.shape[0]
    tb = ROW_TB
    grid_spec = pltpu.PrefetchScalarGridSpec(
        num_scalar_prefetch=1,
        grid=(t // tb,),
        in_specs=[
            pl.BlockSpec(memory_space=pl.ANY),
            pl.BlockSpec((tb, D_MODEL), lambda i, d: (i, 0)),
            pl.BlockSpec((tb, TOP_K), lambda i, d: (i, 0)),
            pl.BlockSpec((tb, D_PLE), lambda i, d: (i, 0)),
            pl.BlockSpec((D_PLE, D_MODEL), lambda i, d: (0, 0)),
            pl.BlockSpec((1, D_MODEL), lambda i, d: (0, 0)),
            pl.BlockSpec((D_MODEL, D_MODEL), lambda i, d: (0, 0)),
        ],
        out_specs=pl.BlockSpec((tb, D_MODEL), lambda i, d: (i, 0)),
        scratch_shapes=[pltpu.VMEM((TOP_K, tb, D_MODEL), F32), pltpu.SemaphoreType.DMA],
    )
    return pl.pallas_call(
        functools.partial(_combine_kernel, t=t),
        grid_spec=grid_spec,
        out_shape=jax.ShapeDtypeStruct((t, D_MODEL), F32),
        compiler_params=_params("arbitrary"),
        name="combine",
    )(dest_flat, ys, h, gates_tk, p2, w_ple_bf, ple_norm_g.reshape(1, D_MODEL), w_pg_bf)


def _plan(idx_t, rank_t, counts, t):
    tk = t * TOP_K
    n_items = tk // MOE_BM + N_EXPERTS
    n_rows = tk + N_EXPERTS * ROW_ALIGN + MOE_SUB
    seg = (counts + ROW_ALIGN - 1) // ROW_ALIGN * ROW_ALIGN
    seg_end = jnp.cumsum(seg)
    seg_start = seg_end - seg
    dest = (seg_start[idx_t] + rank_t).reshape(-1).astype(I32)

    per_e = (counts + MOE_BM - 1) // MOE_BM
    it_end = jnp.cumsum(per_e)
    it_off = it_end - per_e
    total = it_end[-1]
    its = jnp.arange(n_items, dtype=I32)
    valid = its < total
    its_c = jnp.minimum(its, total - 1)
    e_of = jnp.minimum(jnp.searchsorted(it_end, its_c, side="right"), N_EXPERTS - 1).astype(I32)
    li = its_c - it_off[e_of]
    rows = jnp.clip(counts[e_of] - li * MOE_BM, 0, MOE_BM)
    nsub = jnp.where(valid, (rows + MOE_SUB - 1) // MOE_SUB, 0).astype(I32)
    start = (seg_start[e_of] + li * MOE_BM).astype(I32)

    used_end = start[total - 1] + MOE_SUB * ((rows[total - 1] + MOE_SUB - 1) // MOE_SUB)
    y_tail = jnp.stack([used_end // SUBLANES, (n_rows - used_end) // SUBLANES]).astype(I32)
    zmeta = jnp.concatenate([
        seg_start + counts, seg - counts,
        jnp.stack([seg_end[-1] // SUBLANES, (n_rows - seg_end[-1]) // SUBLANES]),
    ]).astype(I32)
    return dest, zmeta, e_of, start, nsub, y_tail, n_rows


def kernel(x, p, ln1_g, w_in, q_norm_g, k_norm_g, sinks, sg_ln_g, sg_ln_b, w_spatial, b_spatial, attn_out_g,
           sg_out_g, w_out, ln2_g, w_router, b_router, w_gate_up, b_gate_up, w_down, b_down, w_ple, ple_norm_g,
           w_ple_gate):
    bsz, s, d = x.shape
    assert bsz == 1 and d == D_MODEL and p.shape[0] == 1
    t = bsz * s
    h = x.reshape(t, d)
    li = 0
    proj = _in_proj(h, ln1_g[li], w_in[li].astype(BF16))
    h, xn2 = _mixer(proj, h, sinks[li], q_norm_g[li], k_norm_g[li], sg_ln_g[li], sg_ln_b[li], w_spatial[li],
                    b_spatial[li], attn_out_g[li], sg_out_g[li], w_out[li].astype(BF16), ln2_g[li])
    idx_t, gate_t, rank_t, cnt = _router(xn2, w_router[li], b_router[li])
    dest, zmeta, item_e, item_start, item_nsub, y_tail, n_rows = _plan(idx_t, rank_t, cnt[:, 0], t)
    xs = _dispatch(dest, zmeta, xn2, n_rows)
    ys = _moe(item_e, item_start, item_nsub, y_tail, xs, w_gate_up[li], b_gate_up[li], w_down[li], b_down[li])
    out = _combine(dest, ys, h, gate_t.T, p[li].reshape(t, D_PLE), w_ple[li].astype(BF16), ple_norm_g[li],
                   w_ple_gate[li].astype(BF16))
    return out.reshape(bsz, s, d)
```

```python
import functools

import jax
import jax.numpy as jnp
import numpy as np
from jax import lax
from jax.experimental import pallas as pl
from jax.experimental.pallas import tpu as pltpu

F32 = jnp.float32
BF16 = jnp.bfloat16
I32 = jnp.int32

D_MODEL = 2048
D_PLE = 256
N_Q_HEADS = 16
N_KV_HEADS = 4
HEAD_DIM = 64
WINDOW = 128
N_SG_HEADS = 8
SG_HEAD_DIM = 128
D_ATTN = N_Q_HEADS * HEAD_DIM
D_KV = N_KV_HEADS * HEAD_DIM
D_SG = N_SG_HEADS * SG_HEAD_DIM
D_MIX = D_ATTN + D_SG
D_IN = D_ATTN + 2 * D_KV + 2 * D_SG
N_EXPERTS = 32
TOP_K = 4
D_FF = D_MODEL
SWIGLU_LIMIT = 7.0
SWIGLU_ALPHA = 1.702
EPS = 1e-6
NEG_INF = -1e30

SUBLANES = 8
LANES = 128
VMEM_LIMIT_BYTES = 56 * 1024 * 1024

IN_BM = 512
IN_BN = 1792
MIX_R = 256
ROUTE_TB = 512
ROW_TB = 256
MOE_SUB = 256
MOE_BM = 2048
MOE_F = 256
ROW_ALIGN = SUBLANES


def _rms(x, g):
    ms = jnp.mean(x * x, axis=-1, keepdims=True)
    return x * lax.rsqrt(ms + EPS) * g


def _gelu(x):
    return 0.5 * x * (1.0 + lax.erf(x * np.float32(np.sqrt(0.5))))


def _params(*sem):
    return pltpu.CompilerParams(dimension_semantics=sem, vmem_limit_bytes=VMEM_LIMIT_BYTES)


def _in_proj_kernel(x_ref, g_ref, w_ref, o_ref, hn_ref):
    @pl.when(pl.program_id(1) == 0)
    def _():
        hn_ref[...] = _rms(x_ref[...], g_ref[...]).astype(BF16)

    o_ref[...] = jnp.dot(hn_ref[...], w_ref[...], preferred_element_type=F32)


def _in_proj(x2, ln1_g, w_in_bf):
    t = x2.shape[0]
    return pl.pallas_call(
        _in_proj_kernel,
        grid=(t // IN_BM, D_IN // IN_BN),
        in_specs=[
            pl.BlockSpec((IN_BM, D_MODEL), lambda i, j: (i, 0)),
            pl.BlockSpec((1, D_MODEL), lambda i, j: (0, 0)),
            pl.BlockSpec((D_MODEL, IN_BN), lambda i, j: (0, j)),
        ],
        out_specs=pl.BlockSpec((IN_BM, IN_BN), lambda i, j: (i, j)),
        out_shape=jax.ShapeDtypeStruct((t, D_IN), F32),
        scratch_shapes=[pltpu.VMEM((IN_BM, D_MODEL), BF16)],
        compiler_params=_params("arbitrary", "arbitrary"),
        name="in_proj",
    )(x2, ln1_g.reshape(1, D_MODEL), w_in_bf)


def _head_mean_sq(t, bd):
    tt = t * t
    hi = tt.astype(BF16)
    lo = (tt - hi.astype(F32)).astype(BF16)
    s = jnp.dot(hi, bd, preferred_element_type=F32) + jnp.dot(lo, bd, preferred_element_type=F32)
    return s * (1.0 / HEAD_DIM)


def _mixer_kernel(sinks_ref, proj_ref, pkv_ref, x_ref, qg_ref, kg_ref, bd_ref, lng_ref, lnb_ref,
                  wsp_ref, bsp_ref, ag_ref, sgg_ref, wout_ref, ln2_ref,
                  h_ref, xn2_ref, a_ref, merged_ref):
    i = pl.program_id(0)
    nsb = MIX_R // WINDOW
    grp = N_Q_HEADS // N_KV_HEADS

    bd = bd_ref[...]
    kv_all = jnp.concatenate([pkv_ref[...], proj_ref[:, D_ATTN:D_ATTN + 2 * D_KV]], axis=0)
    k_all = kv_all[:, :D_KV]
    kn = (k_all * lax.rsqrt(_head_mean_sq(k_all, bd[:D_KV, :D_KV]) + EPS) * kg_ref[...]).astype(BF16)
    vb = kv_all[:, D_KV:].astype(BF16)
    q_all = proj_ref[:, :D_ATTN]
    qn = (q_all * lax.rsqrt(_head_mean_sq(q_all, bd) + EPS) * qg_ref[...] * (HEAD_DIM ** -0.5)).astype(BF16)

    rows = lax.broadcasted_iota(I32, (grp * WINDOW, 2 * WINDOW), 0)
    cols = lax.broadcasted_iota(I32, (grp * WINDOW, 2 * WINDOW), 1)
    rel = cols - (rows & (WINDOW - 1))
    band = (rel >= 1) & (rel <= WINDOW)
    cur = cols >= WINDOW
    rblk = lax.broadcasted_iota(I32, (grp * WINDOW, 1), 0) // WINDOW

    tr = lax.broadcasted_iota(I32, (WINDOW, WINDOW), 0)
    tc = lax.broadcasted_iota(I32, (WINDOW, WINDOW), 1)
    wsp = [jnp.where(tc <= tr, wsp_ref[g], 0.0).astype(BF16) for g in range(N_SG_HEADS)]

    for sb in range(nsb):
        r0 = sb * WINDOW
        has_prev = (i * nsb + sb) > 0
        ok = band & (cur | has_prev)
        for g in range(N_KV_HEADS):
            kc = kn[r0:r0 + 2 * WINDOW, g * HEAD_DIM:(g + 1) * HEAD_DIM]
            vc = vb[r0:r0 + 2 * WINDOW, g * HEAD_DIM:(g + 1) * HEAD_DIM]
            qg = jnp.concatenate(
                [qn[r0:r0 + WINDOW, (g * grp + j) * HEAD_DIM:(g * grp + j + 1) * HEAD_DIM] for j in range(grp)],
                axis=0)
            s = lax.dot_general(qg, kc, (((1,), (1,)), ((), ())), preferred_element_type=F32)
            s = jnp.where(ok, s, NEG_INF)
            sink = jnp.full((grp * WINDOW, 1), sinks_ref[g * grp], F32)
            for j in range(1, grp):
                sink = jnp.where(rblk == j, sinks_ref[g * grp + j], sink)
            m = jnp.maximum(jnp.max(s, axis=-1, keepdims=True), sink)
            p = jnp.exp(s - m)
            den = jnp.sum(p, axis=-1, keepdims=True) + jnp.exp(sink - m)
            o = jnp.dot((p / den).astype(BF16), vc, preferred_element_type=F32)
            for j in range(grp):
                hh = g * grp + j
                a_ref[r0:r0 + WINDOW, hh * HEAD_DIM:(hh + 1) * HEAD_DIM] = o[j * WINDOW:(j + 1) * WINDOW]
        merged_ref[r0:r0 + WINDOW, :D_ATTN] = _rms(a_ref[r0:r0 + WINDOW, :], ag_ref[...]).astype(BF16)

        u = _gelu(proj_ref[r0:r0 + WINDOW, D_ATTN + 2 * D_KV:D_ATTN + 2 * D_KV + D_SG])
        vg = _gelu(proj_ref[r0:r0 + WINDOW, D_ATTN + 2 * D_KV + D_SG:])
        mu = jnp.mean(vg, axis=-1, keepdims=True)
        vc0 = vg - mu
        var = jnp.mean(vc0 * vc0, axis=-1, keepdims=True)
        vn = (vc0 * lax.rsqrt(var + EPS) * lng_ref[...] + lnb_ref[...]).astype(BF16)
        mixed = jnp.concatenate(
            [jnp.dot(wsp[g], vn[:, g * SG_HEAD_DIM:(g + 1) * SG_HEAD_DIM], preferred_element_type=F32)
             for g in range(N_SG_HEADS)], axis=1) + bsp_ref[...]
        merged_ref[r0:r0 + WINDOW, D_ATTN:] = _rms(u * mixed, sgg_ref[...]).astype(BF16)

    hblk = x_ref[...] + jnp.dot(merged_ref[...], wout_ref[...], preferred_element_type=F32)
    h_ref[...] = hblk
    xn2_ref[...] = _rms(hblk, ln2_ref[...])


def _mixer(proj, x2, sinks, q_norm_g, k_norm_g, sg_ln_g, sg_ln_b, w_spatial, b_spatial,
           attn_out_g, sg_out_g, w_out_bf, ln2_g):
    t = x2.shape[0]
    nsb = MIX_R // WINDOW
    kvw = 2 * D_KV
    qg = jnp.tile(q_norm_g, N_Q_HEADS).reshape(1, D_ATTN)
    kg = jnp.tile(k_norm_g, N_KV_HEADS).reshape(1, D_KV)
    bd = jnp.asarray(np.kron(np.eye(N_Q_HEADS), np.ones((HEAD_DIM, HEAD_DIM))), BF16)
    bsp = jnp.repeat(b_spatial.T, SG_HEAD_DIM, axis=1)
    row = lambda v, d: v.reshape(1, d)
    const = lambda shape: pl.BlockSpec(shape, lambda i, s: tuple(0 for _ in shape))
    grid_spec = pltpu.PrefetchScalarGridSpec(
        num_scalar_prefetch=1,
        grid=(t // MIX_R,),
        in_specs=[
            pl.BlockSpec((MIX_R, D_IN), lambda i, s: (i, 0)),
            pl.BlockSpec((WINDOW, kvw), lambda i, s: (jnp.maximum(i * nsb - 1, 0), D_ATTN // kvw)),
            pl.BlockSpec((MIX_R, D_MODEL), lambda i, s: (i, 0)),
            const((1, D_ATTN)), const((1, D_KV)), const((D_ATTN, D_ATTN)),
            const((1, D_SG)), const((1, D_SG)),
            const((N_SG_HEADS, WINDOW, WINDOW)), const((WINDOW, D_SG)),
            const((1, D_ATTN)), const((1, D_SG)),
            const((D_MIX, D_MODEL)), const((1, D_MODEL)),
        ],
        out_specs=[
            pl.BlockSpec((MIX_R, D_MODEL), lambda i, s: (i, 0)),
            pl.BlockSpec((MIX_R, D_MODEL), lambda i, s: (i, 0)),
        ],
        scratch_shapes=[pltpu.VMEM((MIX_R, D_ATTN), F32), pltpu.VMEM((MIX_R, D_MIX), BF16)],
    )
    return pl.pallas_call(
        _mixer_kernel,
        grid_spec=grid_spec,
        out_shape=[jax.ShapeDtypeStruct((t, D_MODEL), F32), jax.ShapeDtypeStruct((t, D_MODEL), F32)],
        compiler_params=_params("arbitrary"),
        name="mixer",
    )(sinks, proj, proj, x2, qg, kg, bd, row(sg_ln_g, D_SG), row(sg_ln_b, D_SG), w_spatial, bsp,
      row(attn_out_g, D_ATTN), row(sg_out_g, D_SG), w_out_bf, row(ln2_g, D_MODEL))


def _router_kernel(xn_ref, wr_ref, br_ref, idx_ref, gate_ref, rank_ref, cnt_ref, carry_ref):
    i = pl.program_id(0)
    tb = ROUTE_TB

    @pl.when(i == 0)
    def _():
        carry_ref[...] = jnp.zeros_like(carry_ref)

    logits = lax.dot_general(wr_ref[...], xn_ref[...], (((1,), (1,)), ((), ())),
                             precision=lax.Precision.HIGHEST, preferred_element_type=F32) + br_ref[...]
    eid = lax.broadcasted_iota(I32, (N_EXPERTS, tb), 0)
    work = logits
    vals, idxs = [], []
    for _ in range(TOP_K):
        m = jnp.max(work, axis=0, keepdims=True)
        sel = jnp.min(jnp.where(work == m, eid, N_EXPERTS), axis=0, keepdims=True)
        vals.append(m)
        idxs.append(sel)
        work = jnp.where(eid == sel, -jnp.inf, work)
    ex = [jnp.exp(v - vals[0]) for v in vals]
    den = ex[0] + ex[1] + ex[2] + ex[3]
    chosen = (eid == idxs[0]) | (eid == idxs[1]) | (eid == idxs[2]) | (eid == idxs[3])
    onehot = jnp.where(chosen, 1.0, 0.0)

    tr = lax.broadcasted_iota(I32, (tb, tb), 0)
    tc = lax.broadcasted_iota(I32, (tb, tb), 1)
    before = jnp.where(tr < tc, 1.0, 0.0).astype(BF16)
    cnt = jnp.dot(onehot.astype(BF16), before, preferred_element_type=F32) + carry_ref[:, :1]
    for k in range(TOP_K):
        idx_ref[k:k + 1, :] = idxs[k]
        gate_ref[k:k + 1, :] = ex[k] / den
        rank_ref[k:k + 1, :] = jnp.sum(jnp.where(eid == idxs[k], cnt, 0.0), axis=0, keepdims=True).astype(I32)
    carry_ref[...] = carry_ref[...] + jnp.sum(onehot, axis=1, keepdims=True)
    cnt_ref[...] = carry_ref[...].astype(I32)


def _router(xn2, w_router, b_router):
    t = xn2.shape[0]
    tb = ROUTE_TB
    return pl.pallas_call(
        _router_kernel,
        grid=(t // tb,),
        in_specs=[
            pl.BlockSpec((tb, D_MODEL), lambda i: (i, 0)),
            pl.BlockSpec((N_EXPERTS, D_MODEL), lambda i: (0, 0)),
            pl.BlockSpec((N_EXPERTS, 1), lambda i: (0, 0)),
        ],
        out_specs=[
            pl.BlockSpec((TOP_K, tb), lambda i: (0, i)),
            pl.BlockSpec((TOP_K, tb), lambda i: (0, i)),
            pl.BlockSpec((TOP_K, tb), lambda i: (0, i)),
            pl.BlockSpec((N_EXPERTS, LANES), lambda i: (0, 0)),
        ],
        out_shape=[
            jax.ShapeDtypeStruct((TOP_K, t), I32),
            jax.ShapeDtypeStruct((TOP_K, t), F32),
            jax.ShapeDtypeStruct((TOP_K, t), I32),
            jax.ShapeDtypeStruct((N_EXPERTS, LANES), I32),
        ],
        scratch_shapes=[pltpu.VMEM((N_EXPERTS, LANES), F32)],
        compiler_params=_params("arbitrary"),
        name="router",
    )(xn2, w_router.T, b_router.reshape(N_EXPERTS, 1))


def _row_copy(src, s, dst, d, sem):
    return pltpu.make_async_copy(src.at[pl.ds(s, 1)], dst.at[pl.ds(d, 1)], sem)


def _tile_copy(src, dst, d8, sem):
    return pltpu.make_async_copy(src, dst.at[pl.ds(pl.multiple_of(d8 * SUBLANES, SUBLANES), SUBLANES)], sem)


def _dispatch_kernel(dest_ref, zmeta_ref, xn_ref, xs_ref, zero_ref, sem, zsem, *, t):
    i = pl.program_id(0)
    tb = ROW_TB

    @pl.when(i == 0)
    def _():
        zero_ref[...] = jnp.zeros_like(zero_ref)

        def pads(start):
            def per_expert(e, c):
                def per_row(r, c2):
                    cp = _row_copy(zero_ref, 0, xs_ref, zmeta_ref[e] + r, zsem)
                    cp.start() if start else cp.wait()
                    return c2
                return lax.fori_loop(0, zmeta_ref[N_EXPERTS + e], per_row, c)
            lax.fori_loop(0, N_EXPERTS, per_expert, 0)

            def per_tile(r, c):
                cp = _tile_copy(zero_ref, xs_ref, zmeta_ref[2 * N_EXPERTS] + r, zsem)
                cp.start() if start else cp.wait()
                return c
            lax.fori_loop(0, zmeta_ref[2 * N_EXPERTS + 1], per_tile, 0)

        pads(True)
        pads(False)

    def issue(tok, c):
        for k in range(TOP_K):
            _row_copy(xn_ref, tok, xs_ref, dest_ref[k * t + i * tb + tok], sem).start()
        return c
    lax.fori_loop(0, tb, issue, 0)

    def drain(tok, c):
        for k in range(TOP_K):
            _row_copy(xn_ref, tok, xs_ref, dest_ref[k * t + i * tb + tok], sem).wait()
        return c
    lax.fori_loop(0, tb, drain, 0)


def _dispatch(dest_flat, zmeta, xn2, n_rows):
    t = xn2.shape[0]
    grid_spec = pltpu.PrefetchScalarGridSpec(
        num_scalar_prefetch=2,
        grid=(t // ROW_TB,),
        in_specs=[pl.BlockSpec((ROW_TB, D_MODEL), lambda i, d, z: (i, 0))],
        out_specs=pl.BlockSpec(memory_space=pl.ANY),
        scratch_shapes=[pltpu.VMEM((SUBLANES, D_MODEL), F32), pltpu.SemaphoreType.DMA, pltpu.SemaphoreType.DMA],
    )
    return pl.pallas_call(
        functools.partial(_dispatch_kernel, t=t),
        grid_spec=grid_spec,
        out_shape=jax.ShapeDtypeStruct((n_rows, D_MODEL), F32),
        compiler_params=_params("arbitrary"),
        name="dispatch",
    )(dest_flat, zmeta, xn2)


def _moe_kernel(ie_ref, ist_ref, ins_ref, tail_ref, xs_ref, wgu_ref, bgu_ref, wd_ref, bd_ref, ys_ref,
                xst_ref, xbf_ref, acc_ref, wgu_bf_ref, wdmix_ref, wd_bf_ref, ha_ref, hb_ref, zero_ref,
                xsem, osem, zsem,
                *, n_items, n_chunks):
    it = pl.program_id(0)
    j = pl.program_id(1)
    nsub = ins_ref[it]
    start = ist_ref[it]
    f = MOE_F
    half = f // 2

    def x_copy(item_start, s, slot):
        r = pl.multiple_of(item_start + s * MOE_SUB, SUBLANES)
        return pltpu.make_async_copy(xs_ref.at[pl.ds(r, MOE_SUB)], xst_ref.at[slot], xsem.at[slot])

    def y_copy(item_start, s):
        r = pl.multiple_of(item_start + s * MOE_SUB, SUBLANES)
        return pltpu.make_async_copy(acc_ref.at[pl.ds(s * MOE_SUB, MOE_SUB)], ys_ref.at[pl.ds(r, MOE_SUB)], osem)

    @pl.when((it == 0) & (j == 0))
    def _():
        for s in range(2):
            @pl.when(s < nsub)
            def _():
                x_copy(start, s, s).start()

    @pl.when(j == 0)
    def _():
        @pl.when(it > 0)
        def _():
            def drain(s, c):
                y_copy(ist_ref[it - 1], s).wait()
                return c
            lax.fori_loop(0, ins_ref[it - 1], drain, 0)

        def stage(s, c):
            slot = s & 1
            r = pl.multiple_of(s * MOE_SUB, MOE_SUB)
            x_copy(start, s, slot).wait()
            xbf_ref[pl.ds(r, MOE_SUB), :] = xst_ref[slot].astype(BF16)
            acc_ref[pl.ds(r, MOE_SUB), :] = jnp.broadcast_to(bd_ref[0], (MOE_SUB, D_MODEL))

            @pl.when(s + 2 < nsub)
            def _():
                x_copy(start, s + 2, slot).start()
            return c
        lax.fori_loop(0, nsub, stage, 0)

    @pl.when((j == 1) & (it + 1 < n_items))
    def _():
        for s in range(2):
            @pl.when(s < ins_ref[it + 1])
            def _():
                x_copy(ist_ref[it + 1], s, s).start()

    @pl.when(nsub > 0)
    def _():
        wgu_bf_ref[...] = wgu_ref[0].astype(BF16)
        for c in range(D_MODEL // LANES):
            cs = slice(c * LANES, (c + 1) * LANES)
            wdmix_ref[c, pl.ds(0, half, stride=2), :] = wd_ref[0, :half, cs]
            wdmix_ref[c, pl.ds(1, half, stride=2), :] = wd_ref[0, half:, cs]
            wd_bf_ref[:, cs] = wdmix_ref[c].astype(BF16)

        even = (lax.broadcasted_iota(I32, (MOE_SUB, f), 1) & 1) == 0

        def up_proj(s, h_ref):
            r = pl.multiple_of(s * MOE_SUB, MOE_SUB)
            h_ref[...] = jnp.dot(xbf_ref[pl.ds(r, MOE_SUB), :], wgu_bf_ref[...], preferred_element_type=F32)

        def down_proj(s, h_ref):
            r = pl.multiple_of(s * MOE_SUB, MOE_SUB)
            h = h_ref[...] + bgu_ref[0]
            gate = jnp.minimum(h, SWIGLU_LIMIT)
            glu = gate * jax.nn.sigmoid(gate * SWIGLU_ALPHA)
            up = jnp.clip(h, -SWIGLU_LIMIT, SWIGLU_LIMIT) + 1.0
            prod = glu * pltpu.roll(up, 2 * f - 1, axis=1)
            act = jnp.where(even, prod[:, :f], pltpu.roll(prod, 1, axis=1)[:, f:])
            acc_ref[pl.ds(r, MOE_SUB), :] += jnp.dot(act.astype(BF16), wd_bf_ref[...], preferred_element_type=F32)

        def chunk_pass(send):
            def sends(*subs):
                if send:
                    for s in subs:
                        y_copy(start, s).start()

            up_proj(0, ha_ref)
            n_pairs = (nsub - 1) // 2

            def pair(i, c):
                s = 2 * i
                up_proj(s + 1, hb_ref)
                down_proj(s, ha_ref)
                up_proj(s + 2, ha_ref)
                down_proj(s + 1, hb_ref)
                sends(s, s + 1)
                return c
            lax.fori_loop(0, n_pairs, pair, 0)
            s = 2 * n_pairs

            @pl.when(s == nsub - 1)
            def _():
                down_proj(s, ha_ref)
                sends(s)

            @pl.when(s < nsub - 1)
            def _():
                up_proj(s + 1, hb_ref)
                down_proj(s, ha_ref)
                down_proj(s + 1, hb_ref)
                sends(s, s + 1)

        @pl.when(j < n_chunks - 1)
        def _():
            chunk_pass(False)

        @pl.when(j == n_chunks - 1)
        def _():
            chunk_pass(True)

    @pl.when((it == n_items - 1) & (j == n_chunks - 1))
    def _():
        def drain(s, c):
            y_copy(start, s).wait()
            return c
        lax.fori_loop(0, nsub, drain, 0)

        zero_ref[...] = jnp.zeros_like(zero_ref)

        def fill(start_dma):
            def per_tile(r, c):
                cp = _tile_copy(zero_ref, ys_ref, tail_ref[0] + r, zsem)
                cp.start() if start_dma else cp.wait()
                return c
            lax.fori_loop(0, tail_ref[1], per_tile, 0)
        fill(True)
        fill(False)


def _moe(item_e, item_start, item_nsub, tail, xs, w_gate_up, b_gate_up, w_down, b_down):
    n_items = item_e.shape[0]
    n_rows = xs.shape[0]
    n_chunks = D_FF // MOE_F
    last = n_chunks - 1

    def chunk(it, j, ins):
        return jnp.where(ins[it] > 0, j, last)

    grid_spec = pltpu.PrefetchScalarGridSpec(
        num_scalar_prefetch=4,
        grid=(n_items, n_chunks),
        in_specs=[
            pl.BlockSpec(memory_space=pl.ANY),
            pl.BlockSpec((1, D_MODEL, 2 * MOE_F), lambda it, j, ie, ist, ins, tl: (ie[it], 0, chunk(it, j, ins))),
            pl.BlockSpec((1, 1, 2 * MOE_F), lambda it, j, ie, ist, ins, tl: (ie[it], 0, chunk(it, j, ins))),
            pl.BlockSpec((1, MOE_F, D_MODEL), lambda it, j, ie, ist, ins, tl: (ie[it], chunk(it, j, ins), 0)),
            pl.BlockSpec((1, 1, D_MODEL), lambda it, j, ie, ist, ins, tl: (ie[it], 0, 0)),
        ],
        out_specs=pl.BlockSpec(memory_space=pl.ANY),
        scratch_shapes=[
            pltpu.VMEM((2, MOE_SUB, D_MODEL), F32),
            pltpu.VMEM((MOE_BM, D_MODEL), BF16),
            pltpu.VMEM((MOE_BM, D_MODEL), F32),
            pltpu.VMEM((D_MODEL, 2 * MOE_F), BF16),
            pltpu.VMEM((D_MODEL // LANES, MOE_F, LANES), F32),
            pltpu.VMEM((MOE_F, D_MODEL), BF16),
            pltpu.VMEM((MOE_SUB, 2 * MOE_F), F32),
            pltpu.VMEM((MOE_SUB, 2 * MOE_F), F32),
            pltpu.VMEM((SUBLANES, D_MODEL), F32),
            pltpu.SemaphoreType.DMA((2,)),
            pltpu.SemaphoreType.DMA,
            pltpu.SemaphoreType.DMA,
        ],
    )
    return pl.pallas_call(
        functools.partial(_moe_kernel, n_items=n_items, n_chunks=n_chunks),
        grid_spec=grid_spec,
        out_shape=jax.ShapeDtypeStruct((n_rows, D_MODEL), F32),
        compiler_params=_params("arbitrary", "arbitrary"),
        name="moe",
    )(item_e, item_start, item_nsub, tail, xs, w_gate_up, b_gate_up.reshape(N_EXPERTS, 1, 2 * D_FF),
      w_down, b_down.reshape(N_EXPERTS, 1, D_MODEL))


def _combine_kernel(dest_ref, ys_ref, h_ref, gate_ref, p_ref, wple_ref, pg_ref, wpg_ref, o_ref, ybuf_ref, sem, *, t):
    i = pl.program_id(0)
    tb = ROW_TB

    def issue(tok, c):
        for k in range(TOP_K):
            pltpu.make_async_copy(ys_ref.at[pl.ds(dest_ref[k * t + i * tb + tok], 1)],
                                  ybuf_ref.at[k, pl.ds(tok, 1)], sem).start()
        return c
    lax.fori_loop(0, tb, issue, 0)

    def drain(tok, c):
        for k in range(TOP_K):
            pltpu.make_async_copy(ys_ref.at[pl.ds(dest_ref[k * t + i * tb + tok], 1)],
                                  ybuf_ref.at[k, pl.ds(tok, 1)], sem).wait()
        return c
    lax.fori_loop(0, tb, drain, 0)

    gates = gate_ref[...]
    h2 = h_ref[...]
    for k in range(TOP_K):
        h2 = h2 + ybuf_ref[k] * gates[:, k:k + 1]
    gate = jax.nn.sigmoid(jnp.dot(_rms(h2, pg_ref[...]).astype(BF16), wpg_ref[...], preferred_element_type=F32))
    o_ref[...] = h2 + jnp.dot(p_ref[...].astype(BF16), wple_ref[...], preferred_element_type=F32) * gate


def _combine(dest_flat, ys, h, gates_tk, p2, w_ple_bf, ple_norm_g, w_pg_bf):
    t = h.shape[0]
    tb = ROW_TB
    grid_spec = pltpu.PrefetchScalarGridSpec(
        num_scalar_prefetch=1,
        grid=(t // tb,),
        in_specs=[
            pl.BlockSpec(memory_space=pl.ANY),
            pl.BlockSpec((tb, D_MODEL), lambda i, d: (i, 0)),
            pl.BlockSpec((tb, TOP_K), lambda i, d: (i, 0)),
            pl.BlockSpec((tb, D_PLE), lambda i, d: (i, 0)),
            pl.BlockSpec((D_PLE, D_MODEL), lambda i, d: (0, 0)),
            pl.BlockSpec((1, D_MODEL), lambda i, d: (0, 0)),
            pl.BlockSpec((D_MODEL, D_MODEL), lambda i, d: (0, 0)),
        ],
        out_specs=pl.BlockSpec((tb, D_MODEL), lambda i, d: (i, 0)),
        scratch_shapes=[pltpu.VMEM((TOP_K, tb, D_MODEL), F32), pltpu.SemaphoreType.DMA],
    )
    return pl.pallas_call(
        functools.partial(_combine_kernel, t=t),
        grid_spec=grid_spec,
        out_shape=jax.ShapeDtypeStruct((t, D_MODEL), F32),
        compiler_params=_params("arbitrary"),
        name="combine",
    )(dest_flat, ys, h, gates_tk, p2, w_ple_bf, ple_norm_g.reshape(1, D_MODEL), w_pg_bf)


def _plan(idx_t, rank_t, counts, t):
    tk = t * TOP_K
    n_items = tk // MOE_BM + N_EXPERTS
    n_rows = tk + N_EXPERTS * ROW_ALIGN + MOE_SUB
    seg = (counts + ROW_ALIGN - 1) // ROW_ALIGN * ROW_ALIGN
    seg_end = jnp.cumsum(seg)
    seg_start = seg_end - seg
    eids = jnp.arange(N_EXPERTS, dtype=I32)[:, None, None]
    first = jnp.sum(jnp.where(idx_t[None] == eids, seg_start[:, None, None], 0), axis=0)
    dest = (first + rank_t).reshape(-1).astype(I32)

    per_e = (counts + MOE_BM - 1) // MOE_BM
    it_end = jnp.cumsum(per_e)
    it_off = it_end - per_e
    total = it_end[-1]
    its = jnp.arange(n_items, dtype=I32)
    valid = its < total
    its_c = jnp.minimum(its, total - 1)
    e_of = jnp.minimum(jnp.searchsorted(it_end, its_c, side="right"), N_EXPERTS - 1).astype(I32)
    li = its_c - it_off[e_of]
    rows = jnp.clip(counts[e_of] - li * MOE_BM, 0, MOE_BM)
    nsub = jnp.where(valid, (rows + MOE_SUB - 1) // MOE_SUB, 0).astype(I32)
    start = (seg_start[e_of] + li * MOE_BM).astype(I32)

    used_end = start[total - 1] + MOE_SUB * ((rows[total - 1] + MOE_SUB - 1) // MOE_SUB)
    y_tail = jnp.stack([used_end // SUBLANES, (n_rows - used_end) // SUBLANES]).astype(I32)
    zmeta = jnp.concatenate([
        seg_start + counts, seg - counts,
        jnp.stack([seg_end[-1] // SUBLANES, (n_rows - seg_end[-1]) // SUBLANES]),
    ]).astype(I32)
    return dest, zmeta, e_of, start, nsub, y_tail, n_rows


def kernel(x, p, ln1_g, w_in, q_norm_g, k_norm_g, sinks, sg_ln_g, sg_ln_b, w_spatial, b_spatial, attn_out_g,
           sg_out_g, w_out, ln2_g, w_router, b_router, w_gate_up, b_gate_up, w_down, b_down, w_ple, ple_norm_g,
           w_ple_gate):
    bsz, s, d = x.shape
    assert bsz == 1 and d == D_MODEL and p.shape[0] == 1
    t = bsz * s
    h = x.reshape(t, d)
    li = 0
    proj = _in_proj(h, ln1_g[li], w_in[li].astype(BF16))
    h, xn2 = _mixer(proj, h, sinks[li], q_norm_g[li], k_norm_g[li], sg_ln_g[li], sg_ln_b[li], w_spatial[li],
                    b_spatial[li], attn_out_g[li], sg_out_g[li], w_out[li].astype(BF16), ln2_g[li])
    idx_t, gate_t, rank_t, cnt = _router(xn2, w_router[li], b_router[li])
    dest, zmeta, item_e, item_start, item_nsub, y_tail, n_rows = _plan(idx_t, rank_t, cnt[:, 0], t)
    xs = _dispatch(dest, zmeta, xn2, n_rows)
    ys = _moe(item_e, item_start, item_nsub, y_tail, xs, w_gate_up[li], b_gate_up[li], w_down[li], b_down[li])
    out = _combine(dest, ys, h, gate_t.T, p[li].reshape(t, D_PLE), w_ple[li].astype(BF16), ple_norm_g[li],
                   w_ple_gate[li].astype(BF16))
    return out.reshape(bsz, s, d)
```

```python
import functools

import jax
import jax.numpy as jnp
import numpy as np
from jax import lax
from jax.experimental import pallas as pl
from jax.experimental.pallas import tpu as pltpu

F32 = jnp.float32
BF16 = jnp.bfloat16
I32 = jnp.int32

D_MODEL = 2048
D_PLE = 256
N_Q_HEADS = 16
N_KV_HEADS = 4
HEAD_DIM = 64
WINDOW = 128
N_SG_HEADS = 8
SG_HEAD_DIM = 128
D_ATTN = N_Q_HEADS * HEAD_DIM
D_KV = N_KV_HEADS * HEAD_DIM
D_SG = N_SG_HEADS * SG_HEAD_DIM
D_MIX = D_ATTN + D_SG
D_IN = D_ATTN + 2 * D_KV + 2 * D_SG
N_EXPERTS = 32
TOP_K = 4
D_FF = D_MODEL
SWIGLU_LIMIT = 7.0
SWIGLU_ALPHA = 1.702
EPS = 1e-6
NEG_INF = -1e30

SUBLANES = 8
LANES = 128
MXU_DIM = 256
VMEM_LIMIT_BYTES = 56 * 1024 * 1024

IN_BM = 512
IN_BN = 1792
MIX_R = 256
ROUTE_TB = 512
ROW_TB = 256
MOE_SUB = 256
MOE_BM = 2048
MOE_F = 256
ROW_ALIGN = SUBLANES
ISSUE_UNROLL = 8


def _rms(x, g):
    ms = jnp.mean(x * x, axis=-1, keepdims=True)
    return x * lax.rsqrt(ms + EPS) * g


def _gelu(x):
    return 0.5 * x * (1.0 + lax.erf(x * np.float32(np.sqrt(0.5))))


def _params(*sem):
    return pltpu.CompilerParams(dimension_semantics=sem, vmem_limit_bytes=VMEM_LIMIT_BYTES)


def _in_proj_kernel(x_ref, g_ref, w_ref, o_ref, hn_ref):
    @pl.when(pl.program_id(1) == 0)
    def _():
        hn_ref[...] = _rms(x_ref[...], g_ref[...]).astype(BF16)

    o_ref[...] = jnp.dot(hn_ref[...], w_ref[...], preferred_element_type=F32)


def _in_proj(x2, ln1_g, w_in_bf):
    t = x2.shape[0]
    return pl.pallas_call(
        _in_proj_kernel,
        grid=(t // IN_BM, D_IN // IN_BN),
        in_specs=[
            pl.BlockSpec((IN_BM, D_MODEL), lambda i, j: (i, 0)),
            pl.BlockSpec((1, D_MODEL), lambda i, j: (0, 0)),
            pl.BlockSpec((D_MODEL, IN_BN), lambda i, j: (0, j)),
        ],
        out_specs=pl.BlockSpec((IN_BM, IN_BN), lambda i, j: (i, j)),
        out_shape=jax.ShapeDtypeStruct((t, D_IN), F32),
        scratch_shapes=[pltpu.VMEM((IN_BM, D_MODEL), BF16)],
        compiler_params=_params("arbitrary", "arbitrary"),
        name="in_proj",
    )(x2, ln1_g.reshape(1, D_MODEL), w_in_bf)


def _head_mean_sq(t, bd):
    tt = t * t
    hi = tt.astype(BF16)
    lo = (tt - hi.astype(F32)).astype(BF16)
    w = bd.shape[0]
    parts = []
    for c in range(t.shape[1] // w):
        cs = slice(c * w, (c + 1) * w)
        parts.append(jnp.dot(hi[:, cs], bd, preferred_element_type=F32)
                     + jnp.dot(lo[:, cs], bd, preferred_element_type=F32))
    return jnp.concatenate(parts, axis=1) * (1.0 / HEAD_DIM)


def _mixer_kernel(sinks_ref, proj_ref, pkv_ref, x_ref, qg_ref, kg_ref, bd_ref, lng_ref, lnb_ref,
                  wsp_ref, bsp_ref, ag_ref, sgg_ref, wout_ref, ln2_ref,
                  h_ref, xn2_ref, a_ref, merged_ref):
    i = pl.program_id(0)
    nsb = MIX_R // WINDOW
    grp = N_Q_HEADS // N_KV_HEADS

    bd = bd_ref[...]
    kv_all = jnp.concatenate([pkv_ref[...], proj_ref[:, D_ATTN:D_ATTN + 2 * D_KV]], axis=0)
    k_all = kv_all[:, :D_KV]
    kn = (k_all * lax.rsqrt(_head_mean_sq(k_all, bd) + EPS) * kg_ref[...]).astype(BF16)
    vb = kv_all[:, D_KV:].astype(BF16)
    q_all = proj_ref[:, :D_ATTN]
    qn = (q_all * lax.rsqrt(_head_mean_sq(q_all, bd) + EPS) * qg_ref[...] * (HEAD_DIM ** -0.5)).astype(BF16)

    rows = lax.broadcasted_iota(I32, (grp * WINDOW, 2 * WINDOW), 0)
    cols = lax.broadcasted_iota(I32, (grp * WINDOW, 2 * WINDOW), 1)
    rel = cols - (rows & (WINDOW - 1))
    band = (rel >= 1) & (rel <= WINDOW)
    cur = cols >= WINDOW
    rblk = lax.broadcasted_iota(I32, (grp * WINDOW, 1), 0) // WINDOW

    tr = lax.broadcasted_iota(I32, (WINDOW, WINDOW), 0)
    tc = lax.broadcasted_iota(I32, (WINDOW, WINDOW), 1)
    wsp = [jnp.where(tc <= tr, wsp_ref[g], 0.0).astype(BF16) for g in range(N_SG_HEADS)]

    for sb in range(nsb):
        r0 = sb * WINDOW
        has_prev = (i * nsb + sb) > 0
        ok = band & (cur | has_prev)
        for g in range(N_KV_HEADS):
            kc = kn[r0:r0 + 2 * WINDOW, g * HEAD_DIM:(g + 1) * HEAD_DIM]
            vc = vb[r0:r0 + 2 * WINDOW, g * HEAD_DIM:(g + 1) * HEAD_DIM]
            qg = jnp.concatenate(
                [qn[r0:r0 + WINDOW, (g * grp + j) * HEAD_DIM:(g * grp + j + 1) * HEAD_DIM] for j in range(grp)],
                axis=0)
            s = lax.dot_general(qg, kc, (((1,), (1,)), ((), ())), preferred_element_type=F32)
            s = jnp.where(ok, s, NEG_INF)
            sink = jnp.full((grp * WINDOW, 1), sinks_ref[g * grp], F32)
            for j in range(1, grp):
                sink = jnp.where(rblk == j, sinks_ref[g * grp + j], sink)
            m = jnp.maximum(jnp.max(s, axis=-1, keepdims=True), sink)
            p = jnp.exp(s - m)
            den = jnp.sum(p, axis=-1, keepdims=True) + jnp.exp(sink - m)
            o = jnp.dot((p / den).astype(BF16), vc, preferred_element_type=F32)
            for j in range(grp):
                hh = g * grp + j
                a_ref[r0:r0 + WINDOW, hh * HEAD_DIM:(hh + 1) * HEAD_DIM] = o[j * WINDOW:(j + 1) * WINDOW]
        merged_ref[r0:r0 + WINDOW, :D_ATTN] = _rms(a_ref[r0:r0 + WINDOW, :], ag_ref[...]).astype(BF16)

        u = _gelu(proj_ref[r0:r0 + WINDOW, D_ATTN + 2 * D_KV:D_ATTN + 2 * D_KV + D_SG])
        vg = _gelu(proj_ref[r0:r0 + WINDOW, D_ATTN + 2 * D_KV + D_SG:])
        mu = jnp.mean(vg, axis=-1, keepdims=True)
        vc0 = vg - mu
        var = jnp.mean(vc0 * vc0, axis=-1, keepdims=True)
        vn = (vc0 * lax.rsqrt(var + EPS) * lng_ref[...] + lnb_ref[...]).astype(BF16)
        mixed = jnp.concatenate(
            [jnp.dot(wsp[g], vn[:, g * SG_HEAD_DIM:(g + 1) * SG_HEAD_DIM], preferred_element_type=F32)
             for g in range(N_SG_HEADS)], axis=1) + bsp_ref[...]
        merged_ref[r0:r0 + WINDOW, D_ATTN:] = _rms(u * mixed, sgg_ref[...]).astype(BF16)

    hblk = x_ref[...] + jnp.dot(merged_ref[...], wout_ref[...], preferred_element_type=F32)
    h_ref[...] = hblk
    xn2_ref[...] = _rms(hblk, ln2_ref[...])


def _mixer(proj, x2, sinks, q_norm_g, k_norm_g, sg_ln_g, sg_ln_b, w_spatial, b_spatial,
           attn_out_g, sg_out_g, w_out_bf, ln2_g):
    t = x2.shape[0]
    nsb = MIX_R // WINDOW
    kvw = 2 * D_KV
    qg = jnp.tile(q_norm_g, N_Q_HEADS).reshape(1, D_ATTN)
    kg = jnp.tile(k_norm_g, N_KV_HEADS).reshape(1, D_KV)
    bd = jnp.asarray(np.kron(np.eye(MXU_DIM // HEAD_DIM), np.ones((HEAD_DIM, HEAD_DIM))), BF16)
    bsp = jnp.repeat(b_spatial.T, SG_HEAD_DIM, axis=1)
    row = lambda v, d: v.reshape(1, d)
    const = lambda shape: pl.BlockSpec(shape, lambda i, s: tuple(0 for _ in shape))
    grid_spec = pltpu.PrefetchScalarGridSpec(
        num_scalar_prefetch=1,
        grid=(t // MIX_R,),
        in_specs=[
            pl.BlockSpec((MIX_R, D_IN), lambda i, s: (i, 0)),
            pl.BlockSpec((WINDOW, kvw), lambda i, s: (jnp.maximum(i * nsb - 1, 0), D_ATTN // kvw)),
            pl.BlockSpec((MIX_R, D_MODEL), lambda i, s: (i, 0)),
            const((1, D_ATTN)), const((1, D_KV)), const((MXU_DIM, MXU_DIM)),
            const((1, D_SG)), const((1, D_SG)),
            const((N_SG_HEADS, WINDOW, WINDOW)), const((WINDOW, D_SG)),
            const((1, D_ATTN)), const((1, D_SG)),
            const((D_MIX, D_MODEL)), const((1, D_MODEL)),
        ],
        out_specs=[
            pl.BlockSpec((MIX_R, D_MODEL), lambda i, s: (i, 0)),
            pl.BlockSpec((MIX_R, D_MODEL), lambda i, s: (i, 0)),
        ],
        scratch_shapes=[pltpu.VMEM((MIX_R, D_ATTN), F32), pltpu.VMEM((MIX_R, D_MIX), BF16)],
    )
    return pl.pallas_call(
        _mixer_kernel,
        grid_spec=grid_spec,
        out_shape=[jax.ShapeDtypeStruct((t, D_MODEL), F32), jax.ShapeDtypeStruct((t, D_MODEL), F32)],
        compiler_params=_params("arbitrary"),
        name="mixer",
    )(sinks, proj, proj, x2, qg, kg, bd, row(sg_ln_g, D_SG), row(sg_ln_b, D_SG), w_spatial, bsp,
      row(attn_out_g, D_ATTN), row(sg_out_g, D_SG), w_out_bf, row(ln2_g, D_MODEL))


def _router_kernel(xn_ref, wr_ref, br_ref, idx_ref, gate_ref, rank_ref, cnt_ref, carry_ref):
    i = pl.program_id(0)
    tb = ROUTE_TB

    @pl.when(i == 0)
    def _():
        carry_ref[...] = jnp.zeros_like(carry_ref)

    def split(v):
        hi = v.astype(BF16)
        return hi, (v - hi.astype(F32)).astype(BF16)

    def nt(a, b):
        return lax.dot_general(a, b, (((1,), (1,)), ((), ())), preferred_element_type=F32)

    w_hi, w_lo = split(wr_ref[...])
    x_hi, x_lo = split(xn_ref[...])
    logits = nt(w_hi, x_hi) + (nt(w_hi, x_lo) + nt(w_lo, x_hi)) + br_ref[...]
    eid = lax.broadcasted_iota(I32, (N_EXPERTS, tb), 0)
    work = logits
    vals, idxs = [], []
    for _ in range(TOP_K):
        m = jnp.max(work, axis=0, keepdims=True)
        sel = jnp.min(jnp.where(work == m, eid, N_EXPERTS), axis=0, keepdims=True)
        vals.append(m)
        idxs.append(sel)
        work = jnp.where(eid == sel, -jnp.inf, work)
    ex = [jnp.exp(v - vals[0]) for v in vals]
    den = ex[0] + ex[1] + ex[2] + ex[3]
    chosen = (eid == idxs[0]) | (eid == idxs[1]) | (eid == idxs[2]) | (eid == idxs[3])
    onehot = jnp.where(chosen, 1.0, 0.0)

    tr = lax.broadcasted_iota(I32, (tb, tb), 0)
    tc = lax.broadcasted_iota(I32, (tb, tb), 1)
    before = jnp.where(tr < tc, 1.0, 0.0).astype(BF16)
    cnt = jnp.dot(onehot.astype(BF16), before, preferred_element_type=F32) + carry_ref[:, :1]
    for k in range(TOP_K):
        idx_ref[k:k + 1, :] = idxs[k]
        gate_ref[k:k + 1, :] = ex[k] / den
        rank_ref[k:k + 1, :] = jnp.sum(jnp.where(eid == idxs[k], cnt, 0.0), axis=0, keepdims=True).astype(I32)
    carry_ref[...] = carry_ref[...] + jnp.sum(onehot, axis=1, keepdims=True)
    cnt_ref[...] = carry_ref[...].astype(I32)


def _router(xn2, w_router, b_router):
    t = xn2.shape[0]
    tb = ROUTE_TB
    return pl.pallas_call(
        _router_kernel,
        grid=(t // tb,),
        in_specs=[
            pl.BlockSpec((tb, D_MODEL), lambda i: (i, 0)),
            pl.BlockSpec((N_EXPERTS, D_MODEL), lambda i: (0, 0)),
            pl.BlockSpec((N_EXPERTS, 1), lambda i: (0, 0)),
        ],
        out_specs=[
            pl.BlockSpec((TOP_K, tb), lambda i: (0, i)),
            pl.BlockSpec((TOP_K, tb), lambda i: (0, i)),
            pl.BlockSpec((TOP_K, tb), lambda i: (0, i)),
            pl.BlockSpec((N_EXPERTS, LANES), lambda i: (0, 0)),
        ],
        out_shape=[
            jax.ShapeDtypeStruct((TOP_K, t), I32),
            jax.ShapeDtypeStruct((TOP_K, t), F32),
            jax.ShapeDtypeStruct((TOP_K, t), I32),
            jax.ShapeDtypeStruct((N_EXPERTS, LANES), I32),
        ],
        scratch_shapes=[pltpu.VMEM((N_EXPERTS, LANES), F32)],
        compiler_params=_params("arbitrary"),
        name="router",
    )(xn2, w_router.T, b_router.reshape(N_EXPERTS, 1))


def _row_copy(src, s, dst, d, sem):
    return pltpu.make_async_copy(src.at[pl.ds(s, 1)], dst.at[pl.ds(d, 1)], sem)


def _tile_copy(src, dst, d8, sem):
    return pltpu.make_async_copy(src, dst.at[pl.ds(pl.multiple_of(d8 * SUBLANES, SUBLANES), SUBLANES)], sem)


def _dispatch_kernel(dest_ref, zmeta_ref, xn_ref, xs_ref, zero_ref, sem, zsem, *, t):
    i = pl.program_id(0)
    tb = ROW_TB

    @pl.when(i == 0)
    def _():
        zero_ref[...] = jnp.zeros_like(zero_ref)

        def pads(start):
            def per_expert(e, c):
                def per_row(r, c2):
                    cp = _row_copy(zero_ref, 0, xs_ref, zmeta_ref[e] + r, zsem)
                    cp.start() if start else cp.wait()
                    return c2
                return lax.fori_loop(0, zmeta_ref[N_EXPERTS + e], per_row, c)
            lax.fori_loop(0, N_EXPERTS, per_expert, 0)

            def per_tile(r, c):
                cp = _tile_copy(zero_ref, xs_ref, zmeta_ref[2 * N_EXPERTS] + r, zsem)
                cp.start() if start else cp.wait()
                return c
            lax.fori_loop(0, zmeta_ref[2 * N_EXPERTS + 1], per_tile, 0)

        pads(True)
        pads(False)

    def issue(tok, c):
        for k in range(TOP_K):
            _row_copy(xn_ref, tok, xs_ref, dest_ref[k * t + i * tb + tok], sem).start(priority=k % 2)
        return c
    lax.fori_loop(0, tb, issue, 0, unroll=ISSUE_UNROLL)

    def drain(tok, c):
        for k in range(TOP_K):
            _row_copy(xn_ref, tok, xs_ref, dest_ref[k * t + i * tb + tok], sem).wait()
        return c
    lax.fori_loop(0, tb, drain, 0, unroll=ISSUE_UNROLL)


def _dispatch(dest_flat, zmeta, xn2, n_rows):
    t = xn2.shape[0]
    grid_spec = pltpu.PrefetchScalarGridSpec(
        num_scalar_prefetch=2,
        grid=(t // ROW_TB,),
        in_specs=[pl.BlockSpec((ROW_TB, D_MODEL), lambda i, d, z: (i, 0))],
        out_specs=pl.BlockSpec(memory_space=pl.ANY),
        scratch_shapes=[pltpu.VMEM((SUBLANES, D_MODEL), F32), pltpu.SemaphoreType.DMA, pltpu.SemaphoreType.DMA],
    )
    return pl.pallas_call(
        functools.partial(_dispatch_kernel, t=t),
        grid_spec=grid_spec,
        out_shape=jax.ShapeDtypeStruct((n_rows, D_MODEL), F32),
        compiler_params=_params("arbitrary"),
        name="dispatch",
    )(dest_flat, zmeta, xn2)


def _moe_kernel(ie_ref, ist_ref, ins_ref, tail_ref, xs_ref, wgu_ref, bgu_ref, wd_ref, bd_ref, ys_ref,
                xst_ref, xbf_ref, acc_ref, wgu_bf_ref, wdmix_ref, wd_bf_ref, ha_ref, hb_ref, zero_ref,
                xsem, osem, zsem,
                *, n_items, n_chunks):
    it = pl.program_id(0)
    j = pl.program_id(1)
    nsub = ins_ref[it]
    start = ist_ref[it]
    f = MOE_F
    half = f // 2

    def x_copy(item_start, s, slot):
        r = pl.multiple_of(item_start + s * MOE_SUB, SUBLANES)
        return pltpu.make_async_copy(xs_ref.at[pl.ds(r, MOE_SUB)], xst_ref.at[slot], xsem.at[slot])

    def y_copy(item_start, s):
        r = pl.multiple_of(item_start + s * MOE_SUB, SUBLANES)
        return pltpu.make_async_copy(acc_ref.at[pl.ds(s * MOE_SUB, MOE_SUB)], ys_ref.at[pl.ds(r, MOE_SUB)], osem)

    @pl.when((it == 0) & (j == 0))
    def _():
        for s in range(2):
            @pl.when(s < nsub)
            def _():
                x_copy(start, s, s).start()

    @pl.when(j == 0)
    def _():
        @pl.when(it > 0)
        def _():
            def drain(s, c):
                y_copy(ist_ref[it - 1], s).wait()
                return c
            lax.fori_loop(0, ins_ref[it - 1], drain, 0)

        def stage(s, c):
            slot = s & 1
            r = pl.multiple_of(s * MOE_SUB, MOE_SUB)
            x_copy(start, s, slot).wait()
            xbf_ref[pl.ds(r, MOE_SUB), :] = xst_ref[slot].astype(BF16)
            acc_ref[pl.ds(r, MOE_SUB), :] = jnp.broadcast_to(bd_ref[0], (MOE_SUB, D_MODEL))

            @pl.when(s + 2 < nsub)
            def _():
                x_copy(start, s + 2, slot).start()
            return c
        lax.fori_loop(0, nsub, stage, 0)

    @pl.when((j == 1) & (it + 1 < n_items))
    def _():
        for s in range(2):
            @pl.when(s < ins_ref[it + 1])
            def _():
                x_copy(ist_ref[it + 1], s, s).start()

    @pl.when(nsub > 0)
    def _():
        wgu_bf_ref[...] = wgu_ref[0].astype(BF16)
        for c in range(D_MODEL // LANES):
            cs = slice(c * LANES, (c + 1) * LANES)
            wdmix_ref[c, pl.ds(0, half, stride=2), :] = wd_ref[0, :half, cs]
            wdmix_ref[c, pl.ds(1, half, stride=2), :] = wd_ref[0, half:, cs]
            wd_bf_ref[:, cs] = wdmix_ref[c].astype(BF16)

        even = (lax.broadcasted_iota(I32, (MOE_SUB, f), 1) & 1) == 0

        def up_proj(s, h_ref):
            r = pl.multiple_of(s * MOE_SUB, MOE_SUB)
            h_ref[...] = jnp.dot(xbf_ref[pl.ds(r, MOE_SUB), :], wgu_bf_ref[...], preferred_element_type=F32)

        def down_proj(s, h_ref):
            r = pl.multiple_of(s * MOE_SUB, MOE_SUB)
            h = h_ref[...] + bgu_ref[0]
            gate = jnp.minimum(h, SWIGLU_LIMIT)
            glu = gate * jax.nn.sigmoid(gate * SWIGLU_ALPHA)
            up = jnp.clip(h, -SWIGLU_LIMIT, SWIGLU_LIMIT) + 1.0
            prod = glu * pltpu.roll(up, 2 * f - 1, axis=1)
            act = jnp.where(even, prod[:, :f], pltpu.roll(prod, 1, axis=1)[:, f:])
            acc_ref[pl.ds(r, MOE_SUB), :] += jnp.dot(act.astype(BF16), wd_bf_ref[...], preferred_element_type=F32)

        def chunk_pass(send):
            def sends(*subs):
                if send:
                    for s in subs:
                        y_copy(start, s).start()

            up_proj(0, ha_ref)
            n_pairs = (nsub - 1) // 2

            def pair(i, c):
                s = 2 * i
                up_proj(s + 1, hb_ref)
                down_proj(s, ha_ref)
                up_proj(s + 2, ha_ref)
                down_proj(s + 1, hb_ref)
                sends(s, s + 1)
                return c
            lax.fori_loop(0, n_pairs, pair, 0)
            s = 2 * n_pairs

            @pl.when(s == nsub - 1)
            def _():
                down_proj(s, ha_ref)
                sends(s)

            @pl.when(s < nsub - 1)
            def _():
                up_proj(s + 1, hb_ref)
                down_proj(s, ha_ref)
                down_proj(s + 1, hb_ref)
                sends(s, s + 1)

        @pl.when(j < n_chunks - 1)
        def _():
            chunk_pass(False)

        @pl.when(j == n_chunks - 1)
        def _():
            chunk_pass(True)

    @pl.when((it == n_items - 1) & (j == n_chunks - 1))
    def _():
        def drain(s, c):
            y_copy(start, s).wait()
            return c
        lax.fori_loop(0, nsub, drain, 0)

        zero_ref[...] = jnp.zeros_like(zero_ref)

        def fill(start_dma):
            def per_tile(r, c):
                cp = _tile_copy(zero_ref, ys_ref, tail_ref[0] + r, zsem)
                cp.start() if start_dma else cp.wait()
                return c
            lax.fori_loop(0, tail_ref[1], per_tile, 0)
        fill(True)
        fill(False)


def _moe(item_e, item_start, item_nsub, tail, xs, w_gate_up, b_gate_up, w_down, b_down):
    n_items = item_e.shape[0]
    n_rows = xs.shape[0]
    n_chunks = D_FF // MOE_F
    last = n_chunks - 1

    def chunk(it, j, ins):
        return jnp.where(ins[it] > 0, j, last)

    grid_spec = pltpu.PrefetchScalarGridSpec(
        num_scalar_prefetch=4,
        grid=(n_items, n_chunks),
        in_specs=[
            pl.BlockSpec(memory_space=pl.ANY),
            pl.BlockSpec((1, D_MODEL, 2 * MOE_F), lambda it, j, ie, ist, ins, tl: (ie[it], 0, chunk(it, j, ins))),
            pl.BlockSpec((1, 1, 2 * MOE_F), lambda it, j, ie, ist, ins, tl: (ie[it], 0, chunk(it, j, ins))),
            pl.BlockSpec((1, MOE_F, D_MODEL), lambda it, j, ie, ist, ins, tl: (ie[it], chunk(it, j, ins), 0)),
            pl.BlockSpec((1, 1, D_MODEL), lambda it, j, ie, ist, ins, tl: (ie[it], 0, 0)),
        ],
        out_specs=pl.BlockSpec(memory_space=pl.ANY),
        scratch_shapes=[
            pltpu.VMEM((2, MOE_SUB, D_MODEL), F32),
            pltpu.VMEM((MOE_BM, D_MODEL), BF16),
            pltpu.VMEM((MOE_BM, D_MODEL), F32),
            pltpu.VMEM((D_MODEL, 2 * MOE_F), BF16),
            pltpu.VMEM((D_MODEL // LANES, MOE_F, LANES), F32),
            pltpu.VMEM((MOE_F, D_MODEL), BF16),
            pltpu.VMEM((MOE_SUB, 2 * MOE_F), F32),
            pltpu.VMEM((MOE_SUB, 2 * MOE_F), F32),
            pltpu.VMEM((SUBLANES, D_MODEL), F32),
            pltpu.SemaphoreType.DMA((2,)),
            pltpu.SemaphoreType.DMA,
            pltpu.SemaphoreType.DMA,
        ],
    )
    return pl.pallas_call(
        functools.partial(_moe_kernel, n_items=n_items, n_chunks=n_chunks),
        grid_spec=grid_spec,
        out_shape=jax.ShapeDtypeStruct((n_rows, D_MODEL), F32),
        compiler_params=_params("arbitrary", "arbitrary"),
        name="moe",
    )(item_e, item_start, item_nsub, tail, xs, w_gate_up, b_gate_up.reshape(N_EXPERTS, 1, 2 * D_FF),
      w_down, b_down.reshape(N_EXPERTS, 1, D_MODEL))


def _combine_kernel(dest_ref, ys_ref, h_ref, gate_ref, p_ref, wple_ref, pg_ref, wpg_ref, o_ref, ybuf_ref, sem, *, t):
    i = pl.program_id(0)
    n = pl.num_programs(0)
    tb = ROW_TB
    slot = i & 1

    def row_gather(blk, sl, tok, k):
        return pltpu.make_async_copy(ys_ref.at[pl.ds(dest_ref[k * t + blk * tb + tok], 1)],
                                     ybuf_ref.at[sl, k, pl.ds(tok, 1)], sem.at[sl])

    def gather(blk, sl, start):
        def per_token(tok, c):
            for k in range(TOP_K):
                cp = row_gather(blk, sl, tok, k)
                cp.start() if start else cp.wait()
            return c
        lax.fori_loop(0, tb, per_token, 0, unroll=ISSUE_UNROLL)

    @pl.when(i == 0)
    def _():
        gather(0, 0, True)

    @pl.when(i + 1 < n)
    def _():
        gather(i + 1, 1 - slot, True)

    gather(i, slot, False)

    gates = gate_ref[...]
    h2 = h_ref[...]
    for k in range(TOP_K):
        h2 = h2 + ybuf_ref[slot, k] * gates[:, k:k + 1]
    gate = jax.nn.sigmoid(jnp.dot(_rms(h2, pg_ref[...]).astype(BF16), wpg_ref[...], preferred_element_type=F32))
    o_ref[...] = h2 + jnp.dot(p_ref[...].astype(BF16), wple_ref[...], preferred_element_type=F32) * gate


def _combine(dest_flat, ys, h, gates_tk, p2, w_ple_bf, ple_norm_g, w_pg_bf):
    t = h.shape[0]
    tb = ROW_TB
    grid_spec = pltpu.PrefetchScalarGridSpec(
        num_scalar_prefetch=1,
        grid=(t // tb,),
        in_specs=[
            pl.BlockSpec(memory_space=pl.ANY),
            pl.BlockSpec((tb, D_MODEL), lambda i, d: (i, 0)),
            pl.BlockSpec((tb, TOP_K), lambda i, d: (i, 0)),
            pl.BlockSpec((tb, D_PLE), lambda i, d: (i, 0)),
            pl.BlockSpec((D_PLE, D_MODEL), lambda i, d: (0, 0)),
            pl.BlockSpec((1, D_MODEL), lambda i, d: (0, 0)),
            pl.BlockSpec((D_MODEL, D_MODEL), lambda i, d: (0, 0)),
        ],
        out_specs=pl.BlockSpec((tb, D_MODEL), lambda i, d: (i, 0)),
        scratch_shapes=[pltpu.VMEM((2, TOP_K, tb, D_MODEL), F32), pltpu.SemaphoreType.DMA((2,))],
    )
    return pl.pallas_call(
        functools.partial(_combine_kernel, t=t),
        grid_spec=grid_spec,
        out_shape=jax.ShapeDtypeStruct((t, D_MODEL), F32),
        compiler_params=_params("arbitrary"),
        name="combine",
    )(dest_flat, ys, h, gates_tk, p2, w_ple_bf, ple_norm_g.reshape(1, D_MODEL), w_pg_bf)


def _plan(idx_t, rank_t, counts, t):
    tk = t * TOP_K
    n_items = tk // MOE_BM + N_EXPERTS
    n_rows = tk + N_EXPERTS * ROW_ALIGN + MOE_SUB
    seg = (counts + ROW_ALIGN - 1) // ROW_ALIGN * ROW_ALIGN
    seg_end = jnp.cumsum(seg)
    seg_start = seg_end - seg
    eids = jnp.arange(N_EXPERTS, dtype=I32)[:, None, None]
    first = jnp.sum(jnp.where(idx_t[None] == eids, seg_start[:, None, None], 0), axis=0)
    dest = (first + rank_t).reshape(-1).astype(I32)

    per_e = (counts + MOE_BM - 1) // MOE_BM
    it_end = jnp.cumsum(per_e)
    it_off = it_end - per_e
    total = it_end[-1]
    its = jnp.arange(n_items, dtype=I32)
    valid = its < total
    its_c = jnp.minimum(its, total - 1)
    e_of = jnp.minimum(jnp.searchsorted(it_end, its_c, side="right"), N_EXPERTS - 1).astype(I32)
    li = its_c - it_off[e_of]
    rows = jnp.clip(counts[e_of] - li * MOE_BM, 0, MOE_BM)
    nsub = jnp.where(valid, (rows + MOE_SUB - 1) // MOE_SUB, 0).astype(I32)
    start = (seg_start[e_of] + li * MOE_BM).astype(I32)

    used_end = start[total - 1] + MOE_SUB * ((rows[total - 1] + MOE_SUB - 1) // MOE_SUB)
    y_tail = jnp.stack([used_end // SUBLANES, (n_rows - used_end) // SUBLANES]).astype(I32)
    zmeta = jnp.concatenate([
        seg_start + counts, seg - counts,
        jnp.stack([seg_end[-1] // SUBLANES, (n_rows - seg_end[-1]) // SUBLANES]),
    ]).astype(I32)
    return dest, zmeta, e_of, start, nsub, y_tail, n_rows


def kernel(x, p, ln1_g, w_in, q_norm_g, k_norm_g, sinks, sg_ln_g, sg_ln_b, w_spatial, b_spatial, attn_out_g,
           sg_out_g, w_out, ln2_g, w_router, b_router, w_gate_up, b_gate_up, w_down, b_down, w_ple, ple_norm_g,
           w_ple_gate):
    bsz, s, d = x.shape
    assert bsz == 1 and d == D_MODEL and p.shape[0] == 1
    t = bsz * s
    h = x.reshape(t, d)
    li = 0
    proj = _in_proj(h, ln1_g[li], w_in[li].astype(BF16))
    h, xn2 = _mixer(proj, h, sinks[li], q_norm_g[li], k_norm_g[li], sg_ln_g[li], sg_ln_b[li], w_spatial[li],
                    b_spatial[li], attn_out_g[li], sg_out_g[li], w_out[li].astype(BF16), ln2_g[li])
    idx_t, gate_t, rank_t, cnt = _router(xn2, w_router[li], b_router[li])
    dest, zmeta, item_e, item_start, item_nsub, y_tail, n_rows = _plan(idx_t, rank_t, cnt[:, 0], t)
    xs = _dispatch(dest, zmeta, xn2, n_rows)
    ys = _moe(item_e, item_start, item_nsub, y_tail, xs, w_gate_up[li], b_gate_up[li], w_down[li], b_down[li])
    out = _combine(dest, ys, h, gate_t.T, p[li].reshape(t, D_PLE), w_ple[li].astype(BF16), ple_norm_g[li],
                   w_ple_gate[li].astype(BF16))
    return out.reshape(bsz, s, d)
```

```python
import functools

import jax
import jax.numpy as jnp
import numpy as np
from jax import lax
from jax.experimental import pallas as pl
from jax.experimental.pallas import tpu as pltpu

F32 = jnp.float32
BF16 = jnp.bfloat16
I32 = jnp.int32

D_MODEL = 2048
D_PLE = 256
N_Q_HEADS = 16
N_KV_HEADS = 4
HEAD_DIM = 64
WINDOW = 128
N_SG_HEADS = 8
SG_HEAD_DIM = 128
D_ATTN = N_Q_HEADS * HEAD_DIM
D_KV = N_KV_HEADS * HEAD_DIM
D_SG = N_SG_HEADS * SG_HEAD_DIM
D_MIX = D_ATTN + D_SG
D_IN = D_ATTN + 2 * D_KV + 2 * D_SG
N_EXPERTS = 32
TOP_K = 4
D_FF = D_MODEL
SWIGLU_LIMIT = 7.0
SWIGLU_ALPHA = 1.702
EPS = 1e-6
NEG_INF = -1e30

SUBLANES = 8
LANES = 128
MXU_DIM = 256
VMEM_LIMIT_BYTES = 56 * 1024 * 1024

IN_BM = 512
IN_BN = 1792
MIX_R = 256
ROUTE_TB = 512
ROW_TB = 256
MOE_SUB = 256
MOE_BM = 1280
MOE_F = 256
ROW_ALIGN = SUBLANES
ISSUE_UNROLL = 8


def _rms(x, g):
    ms = jnp.mean(x * x, axis=-1, keepdims=True)
    return x * lax.rsqrt(ms + EPS) * g


def _gelu(x):
    return 0.5 * x * (1.0 + lax.erf(x * np.float32(np.sqrt(0.5))))


def _params(*sem):
    return pltpu.CompilerParams(dimension_semantics=sem, vmem_limit_bytes=VMEM_LIMIT_BYTES)


def _in_proj_kernel(x_ref, g_ref, w_ref, o_ref, hn_ref):
    @pl.when(pl.program_id(1) == 0)
    def _():
        hn_ref[...] = _rms(x_ref[...], g_ref[...]).astype(BF16)

    o_ref[...] = jnp.dot(hn_ref[...], w_ref[...], preferred_element_type=F32)


def _in_proj(x2, ln1_g, w_in_bf):
    t = x2.shape[0]
    return pl.pallas_call(
        _in_proj_kernel,
        grid=(t // IN_BM, D_IN // IN_BN),
        in_specs=[
            pl.BlockSpec((IN_BM, D_MODEL), lambda i, j: (i, 0)),
            pl.BlockSpec((1, D_MODEL), lambda i, j: (0, 0)),
            pl.BlockSpec((D_MODEL, IN_BN), lambda i, j: (0, j)),
        ],
        out_specs=pl.BlockSpec((IN_BM, IN_BN), lambda i, j: (i, j)),
        out_shape=jax.ShapeDtypeStruct((t, D_IN), F32),
        scratch_shapes=[pltpu.VMEM((IN_BM, D_MODEL), BF16)],
        compiler_params=_params("arbitrary", "arbitrary"),
        name="in_proj",
    )(x2, ln1_g.reshape(1, D_MODEL), w_in_bf)


def _head_mean_sq(t, bd):
    tt = t * t
    hi = tt.astype(BF16)
    lo = (tt - hi.astype(F32)).astype(BF16)
    w = bd.shape[0]
    parts = []
    for c in range(t.shape[1] // w):
        cs = slice(c * w, (c + 1) * w)
        parts.append(jnp.dot(hi[:, cs], bd, preferred_element_type=F32)
                     + jnp.dot(lo[:, cs], bd, preferred_element_type=F32))
    return jnp.concatenate(parts, axis=1) * (1.0 / HEAD_DIM)


def _mixer_kernel(sinks_ref, proj_ref, pkv_ref, x_ref, qg_ref, kg_ref, bd_ref, lng_ref, lnb_ref,
                  wsp_ref, bsp_ref, ag_ref, sgg_ref, wout_ref, ln2_ref,
                  h_ref, xn2_ref, a_ref, merged_ref):
    i = pl.program_id(0)
    nsb = MIX_R // WINDOW
    grp = N_Q_HEADS // N_KV_HEADS

    bd = bd_ref[...]
    kv_all = jnp.concatenate([pkv_ref[...], proj_ref[:, D_ATTN:D_ATTN + 2 * D_KV]], axis=0)
    k_all = kv_all[:, :D_KV]
    kn = (k_all * lax.rsqrt(_head_mean_sq(k_all, bd) + EPS) * kg_ref[...]).astype(BF16)
    vb = kv_all[:, D_KV:].astype(BF16)
    q_all = proj_ref[:, :D_ATTN]
    qn = (q_all * lax.rsqrt(_head_mean_sq(q_all, bd) + EPS) * qg_ref[...] * (HEAD_DIM ** -0.5)).astype(BF16)

    rows = lax.broadcasted_iota(I32, (grp * WINDOW, 2 * WINDOW), 0)
    cols = lax.broadcasted_iota(I32, (grp * WINDOW, 2 * WINDOW), 1)
    rel = cols - (rows & (WINDOW - 1))
    band = (rel >= 1) & (rel <= WINDOW)
    cur = cols >= WINDOW
    rblk = lax.broadcasted_iota(I32, (grp * WINDOW, 1), 0) // WINDOW

    tr = lax.broadcasted_iota(I32, (WINDOW, WINDOW), 0)
    tc = lax.broadcasted_iota(I32, (WINDOW, WINDOW), 1)
    wsp = [jnp.where(tc <= tr, wsp_ref[g], 0.0).astype(BF16) for g in range(N_SG_HEADS)]

    for sb in range(nsb):
        r0 = sb * WINDOW
        has_prev = (i * nsb + sb) > 0
        ok = band & (cur | has_prev)
        for g in range(N_KV_HEADS):
            kc = kn[r0:r0 + 2 * WINDOW, g * HEAD_DIM:(g + 1) * HEAD_DIM]
            vc = vb[r0:r0 + 2 * WINDOW, g * HEAD_DIM:(g + 1) * HEAD_DIM]
            qg = jnp.concatenate(
                [qn[r0:r0 + WINDOW, (g * grp + j) * HEAD_DIM:(g * grp + j + 1) * HEAD_DIM] for j in range(grp)],
                axis=0)
            s = lax.dot_general(qg, kc, (((1,), (1,)), ((), ())), preferred_element_type=F32)
            s = jnp.where(ok, s, NEG_INF)
            sink = jnp.full((grp * WINDOW, 1), sinks_ref[g * grp], F32)
            for j in range(1, grp):
                sink = jnp.where(rblk == j, sinks_ref[g * grp + j], sink)
            m = jnp.maximum(jnp.max(s, axis=-1, keepdims=True), sink)
            p = jnp.exp(s - m)
            den = jnp.sum(p, axis=-1, keepdims=True) + jnp.exp(sink - m)
            o = jnp.dot((p / den).astype(BF16), vc, preferred_element_type=F32)
            for j in range(grp):
                hh = g * grp + j
                a_ref[r0:r0 + WINDOW, hh * HEAD_DIM:(hh + 1) * HEAD_DIM] = o[j * WINDOW:(j + 1) * WINDOW]
        merged_ref[r0:r0 + WINDOW, :D_ATTN] = _rms(a_ref[r0:r0 + WINDOW, :], ag_ref[...]).astype(BF16)

        u = _gelu(proj_ref[r0:r0 + WINDOW, D_ATTN + 2 * D_KV:D_ATTN + 2 * D_KV + D_SG])
        vg = _gelu(proj_ref[r0:r0 + WINDOW, D_ATTN + 2 * D_KV + D_SG:])
        mu = jnp.mean(vg, axis=-1, keepdims=True)
        vc0 = vg - mu
        var = jnp.mean(vc0 * vc0, axis=-1, keepdims=True)
        vn = (vc0 * lax.rsqrt(var + EPS) * lng_ref[...] + lnb_ref[...]).astype(BF16)
        mixed = jnp.concatenate(
            [jnp.dot(wsp[g], vn[:, g * SG_HEAD_DIM:(g + 1) * SG_HEAD_DIM], preferred_element_type=F32)
             for g in range(N_SG_HEADS)], axis=1) + bsp_ref[...]
        merged_ref[r0:r0 + WINDOW, D_ATTN:] = _rms(u * mixed, sgg_ref[...]).astype(BF16)

    hblk = x_ref[...] + jnp.dot(merged_ref[...], wout_ref[...], preferred_element_type=F32)
    h_ref[...] = hblk
    xn2_ref[...] = _rms(hblk, ln2_ref[...])


def _mixer(proj, x2, sinks, q_norm_g, k_norm_g, sg_ln_g, sg_ln_b, w_spatial, b_spatial,
           attn_out_g, sg_out_g, w_out_bf, ln2_g):
    t = x2.shape[0]
    nsb = MIX_R // WINDOW
    kvw = 2 * D_KV
    qg = jnp.tile(q_norm_g, N_Q_HEADS).reshape(1, D_ATTN)
    kg = jnp.tile(k_norm_g, N_KV_HEADS).reshape(1, D_KV)
    bd = jnp.asarray(np.kron(np.eye(MXU_DIM // HEAD_DIM), np.ones((HEAD_DIM, HEAD_DIM))), BF16)
    bsp = jnp.repeat(b_spatial.T, SG_HEAD_DIM, axis=1)
    row = lambda v, d: v.reshape(1, d)
    const = lambda shape: pl.BlockSpec(shape, lambda i, s: tuple(0 for _ in shape))
    grid_spec = pltpu.PrefetchScalarGridSpec(
        num_scalar_prefetch=1,
        grid=(t // MIX_R,),
        in_specs=[
            pl.BlockSpec((MIX_R, D_IN), lambda i, s: (i, 0)),
            pl.BlockSpec((WINDOW, kvw), lambda i, s: (jnp.maximum(i * nsb - 1, 0), D_ATTN // kvw)),
            pl.BlockSpec((MIX_R, D_MODEL), lambda i, s: (i, 0)),
            const((1, D_ATTN)), const((1, D_KV)), const((MXU_DIM, MXU_DIM)),
            const((1, D_SG)), const((1, D_SG)),
            const((N_SG_HEADS, WINDOW, WINDOW)), const((WINDOW, D_SG)),
            const((1, D_ATTN)), const((1, D_SG)),
            const((D_MIX, D_MODEL)), const((1, D_MODEL)),
        ],
        out_specs=[
            pl.BlockSpec((MIX_R, D_MODEL), lambda i, s: (i, 0)),
            pl.BlockSpec((MIX_R, D_MODEL), lambda i, s: (i, 0)),
        ],
        scratch_shapes=[pltpu.VMEM((MIX_R, D_ATTN), F32), pltpu.VMEM((MIX_R, D_MIX), BF16)],
    )
    return pl.pallas_call(
        _mixer_kernel,
        grid_spec=grid_spec,
        out_shape=[jax.ShapeDtypeStruct((t, D_MODEL), F32), jax.ShapeDtypeStruct((t, D_MODEL), F32)],
        compiler_params=_params("arbitrary"),
        name="mixer",
    )(sinks, proj, proj, x2, qg, kg, bd, row(sg_ln_g, D_SG), row(sg_ln_b, D_SG), w_spatial, bsp,
      row(attn_out_g, D_ATTN), row(sg_out_g, D_SG), w_out_bf, row(ln2_g, D_MODEL))


def _router_kernel(xn_ref, wr_ref, br_ref, idx_ref, gate_ref, rank_ref, cnt_ref, carry_ref):
    i = pl.program_id(0)
    tb = ROUTE_TB

    @pl.when(i == 0)
    def _():
        carry_ref[...] = jnp.zeros_like(carry_ref)

    def split(v):
        hi = v.astype(BF16)
        return hi, (v - hi.astype(F32)).astype(BF16)

    def nt(a, b):
        return lax.dot_general(a, b, (((1,), (1,)), ((), ())), preferred_element_type=F32)

    w_hi, w_lo = split(wr_ref[...])
    x_hi, x_lo = split(xn_ref[...])
    logits = nt(w_hi, x_hi) + (nt(w_hi, x_lo) + nt(w_lo, x_hi)) + br_ref[...]
    eid = lax.broadcasted_iota(I32, (N_EXPERTS, tb), 0)
    work = logits
    vals, idxs = [], []
    for _ in range(TOP_K):
        m = jnp.max(work, axis=0, keepdims=True)
        sel = jnp.min(jnp.where(work == m, eid, N_EXPERTS), axis=0, keepdims=True)
        vals.append(m)
        idxs.append(sel)
        work = jnp.where(eid == sel, -jnp.inf, work)
    ex = [jnp.exp(v - vals[0]) for v in vals]
    den = ex[0] + ex[1] + ex[2] + ex[3]
    chosen = (eid == idxs[0]) | (eid == idxs[1]) | (eid == idxs[2]) | (eid == idxs[3])
    onehot = jnp.where(chosen, 1.0, 0.0)

    tr = lax.broadcasted_iota(I32, (tb, tb), 0)
    tc = lax.broadcasted_iota(I32, (tb, tb), 1)
    before = jnp.where(tr < tc, 1.0, 0.0).astype(BF16)
    cnt = jnp.dot(onehot.astype(BF16), before, preferred_element_type=F32) + carry_ref[:, :1]
    for k in range(TOP_K):
        idx_ref[k:k + 1, :] = idxs[k]
        gate_ref[k:k + 1, :] = ex[k] / den
        rank_ref[k:k + 1, :] = jnp.sum(jnp.where(eid == idxs[k], cnt, 0.0), axis=0, keepdims=True).astype(I32)
    carry_ref[...] = carry_ref[...] + jnp.sum(onehot, axis=1, keepdims=True)
    cnt_ref[...] = carry_ref[...].astype(I32)


def _router(xn2, w_router, b_router):
    t = xn2.shape[0]
    tb = ROUTE_TB
    return pl.pallas_call(
        _router_kernel,
        grid=(t // tb,),
        in_specs=[
            pl.BlockSpec((tb, D_MODEL), lambda i: (i, 0)),
            pl.BlockSpec((N_EXPERTS, D_MODEL), lambda i: (0, 0)),
            pl.BlockSpec((N_EXPERTS, 1), lambda i: (0, 0)),
        ],
        out_specs=[
            pl.BlockSpec((TOP_K, tb), lambda i: (0, i)),
            pl.BlockSpec((TOP_K, tb), lambda i: (0, i)),
            pl.BlockSpec((TOP_K, tb), lambda i: (0, i)),
            pl.BlockSpec((N_EXPERTS, LANES), lambda i: (0, 0)),
        ],
        out_shape=[
            jax.ShapeDtypeStruct((TOP_K, t), I32),
            jax.ShapeDtypeStruct((TOP_K, t), F32),
            jax.ShapeDtypeStruct((TOP_K, t), I32),
            jax.ShapeDtypeStruct((N_EXPERTS, LANES), I32),
        ],
        scratch_shapes=[pltpu.VMEM((N_EXPERTS, LANES), F32)],
        compiler_params=_params("arbitrary"),
        name="router",
    )(xn2, w_router.T, b_router.reshape(N_EXPERTS, 1))


def _row_copy(src, s, dst, d, sem):
    return pltpu.make_async_copy(src.at[pl.ds(s, 1)], dst.at[pl.ds(d, 1)], sem)


def _tile_copy(src, dst, d8, sem):
    return pltpu.make_async_copy(src, dst.at[pl.ds(pl.multiple_of(d8 * SUBLANES, SUBLANES), SUBLANES)], sem)


def _dispatch_kernel(dest_ref, zmeta_ref, xn_ref, xs_ref, zero_ref, sem, zsem, *, t):
    i = pl.program_id(0)
    tb = ROW_TB

    @pl.when(i == 0)
    def _():
        zero_ref[...] = jnp.zeros_like(zero_ref)

        def pads(start):
            def per_expert(e, c):
                def per_row(r, c2):
                    cp = _row_copy(zero_ref, 0, xs_ref, zmeta_ref[e] + r, zsem)
                    cp.start() if start else cp.wait()
                    return c2
                return lax.fori_loop(0, zmeta_ref[N_EXPERTS + e], per_row, c)
            lax.fori_loop(0, N_EXPERTS, per_expert, 0)

            def per_tile(r, c):
                cp = _tile_copy(zero_ref, xs_ref, zmeta_ref[2 * N_EXPERTS] + r, zsem)
                cp.start() if start else cp.wait()
                return c
            lax.fori_loop(0, zmeta_ref[2 * N_EXPERTS + 1], per_tile, 0)

        pads(True)
        pads(False)

    def issue(tok, c):
        for k in range(TOP_K):
            _row_copy(xn_ref, tok, xs_ref, dest_ref[k * t + i * tb + tok], sem).start(priority=k % 2)
        return c
    lax.fori_loop(0, tb, issue, 0, unroll=ISSUE_UNROLL)

    def drain(tok, c):
        for k in range(TOP_K):
            _row_copy(xn_ref, tok, xs_ref, dest_ref[k * t + i * tb + tok], sem).wait()
        return c
    lax.fori_loop(0, tb, drain, 0, unroll=ISSUE_UNROLL)


def _dispatch(dest_flat, zmeta, xn2, n_rows):
    t = xn2.shape[0]
    grid_spec = pltpu.PrefetchScalarGridSpec(
        num_scalar_prefetch=2,
        grid=(t // ROW_TB,),
        in_specs=[pl.BlockSpec((ROW_TB, D_MODEL), lambda i, d, z: (i, 0))],
        out_specs=pl.BlockSpec(memory_space=pl.ANY),
        scratch_shapes=[pltpu.VMEM((SUBLANES, D_MODEL), F32), pltpu.SemaphoreType.DMA, pltpu.SemaphoreType.DMA],
    )
    return pl.pallas_call(
        functools.partial(_dispatch_kernel, t=t),
        grid_spec=grid_spec,
        out_shape=jax.ShapeDtypeStruct((n_rows, D_MODEL), F32),
        compiler_params=_params("arbitrary"),
        name="dispatch",
    )(dest_flat, zmeta, xn2)


def _moe_kernel(ie_ref, ist_ref, ins_ref, meta_ref, xs_ref, wgu_hbm, bgu_ref, wd_hbm, bd_ref, ys_ref,
                rawg0_ref, rawg1_ref, rawd0_ref, rawd1_ref, wgu0_ref, wgu1_ref, wd0_ref, wd1_ref,
                wdmix_ref, acc_ref, xbf_ref, xst_ref, ha0_ref, hb0_ref, ha1_ref, hb1_ref,
                zero_ref, gsem, dsem, xsem, osem, zsem, *, n_items, n_chunks):
    h_refs = ((ha0_ref, hb0_ref), (ha1_ref, hb1_ref))
    rawg_ref, rawd_ref = (rawg0_ref, rawg1_ref), (rawd0_ref, rawd1_ref)
    wgu_bf_ref, wd_bf_ref = (wgu0_ref, wgu1_ref), (wd0_ref, wd1_ref)
    it = pl.program_id(0)
    nsub = ins_ref[it]
    start = ist_ref[it]
    total = meta_ref[2]
    slot_acc = it & 1
    f = MOE_F
    half = f // 2

    def x_copy(item_start, s, slot):
        r = pl.multiple_of(item_start + s * MOE_SUB, SUBLANES)
        return pltpu.make_async_copy(xs_ref.at[pl.ds(r, MOE_SUB)], xst_ref.at[slot], xsem.at[slot])

    def y_copy(item, s):
        r = pl.multiple_of(ist_ref[item] + s * MOE_SUB, SUBLANES)
        return pltpu.make_async_copy(acc_ref.at[item & 1, pl.ds(s * MOE_SUB, MOE_SUB)],
                                     ys_ref.at[pl.ds(r, MOE_SUB)], osem.at[item & 1])

    def y_drain(item):
        def one(s, c):
            y_copy(item, s).wait()
            return c
        lax.fori_loop(0, ins_ref[item], one, 0)

    def w_copies(item, j, slot):
        e = ie_ref[item]
        cg = pl.multiple_of(j * 2 * f, 2 * f)
        cd = pl.multiple_of(j * f, f)
        return (pltpu.make_async_copy(wgu_hbm.at[e, :, pl.ds(cg, 2 * f)], rawg_ref[slot], gsem.at[slot]),
                pltpu.make_async_copy(wd_hbm.at[e, pl.ds(cd, f), :], rawd_ref[slot], dsem.at[slot]))

    def cast_weights(slot):
        wgu_bf_ref[slot][...] = rawg_ref[slot][...].astype(BF16)
        for c in range(D_MODEL // LANES):
            cs = slice(c * LANES, (c + 1) * LANES)
            wdmix_ref[c, pl.ds(0, half, stride=2), :] = rawd_ref[slot][:half, cs]
            wdmix_ref[c, pl.ds(1, half, stride=2), :] = rawd_ref[slot][half:, cs]
            wd_bf_ref[slot][:, cs] = wdmix_ref[c].astype(BF16)

    @pl.when(it == 0)
    def _():
        for c in range(2):
            for cp in w_copies(0, c, c):
                cp.start()
        for cp in w_copies(0, 0, 0):
            cp.wait()
        cast_weights(0)
        for s in range(2):
            @pl.when(s < nsub)
            def _():
                x_copy(start, s, s).start()

    @pl.when(nsub > 0)
    def _():
        def stage(s, c):
            slot = s & 1
            r = pl.multiple_of(s * MOE_SUB, MOE_SUB)
            x_copy(start, s, slot).wait()
            xbf_ref[pl.ds(r, MOE_SUB), :] = xst_ref[slot].astype(BF16)
            acc_ref[slot_acc, pl.ds(r, MOE_SUB), :] = jnp.broadcast_to(bd_ref[0], (MOE_SUB, D_MODEL))

            @pl.when(s + 2 < nsub)
            def _():
                x_copy(start, s + 2, slot).start()
            return c
        lax.fori_loop(0, nsub, stage, 0)

        nxt = jnp.minimum(it + 1, n_items - 1)
        for s in range(2):
            @pl.when((it + 1 < n_items) & (s < ins_ref[nxt]))
            def _():
                x_copy(ist_ref[nxt], s, s).start()

        even = (lax.broadcasted_iota(I32, (MOE_SUB, f), 1) & 1) == 0

        def up_proj(s, h_ref, w):
            r = pl.multiple_of(s * MOE_SUB, MOE_SUB)
            h_ref[...] = jnp.dot(xbf_ref[pl.ds(r, MOE_SUB), :], wgu_bf_ref[w][...], preferred_element_type=F32)

        def down_proj(s, h_ref, w, j):
            r = pl.multiple_of(s * MOE_SUB, MOE_SUB)
            h = h_ref[...] + bgu_ref[0, pl.ds(j, 1), :]
            gate = jnp.minimum(h, SWIGLU_LIMIT)
            glu = gate * jax.nn.sigmoid(gate * SWIGLU_ALPHA)
            up = jnp.clip(h, -SWIGLU_LIMIT, SWIGLU_LIMIT) + 1.0
            prod = glu * pltpu.roll(up, 2 * f - 1, axis=1)
            act = jnp.where(even, prod[:, :f], pltpu.roll(prod, 1, axis=1)[:, f:])
            acc_ref[slot_acc, pl.ds(r, MOE_SUB), :] += jnp.dot(act.astype(BF16), wd_bf_ref[w][...],
                                                                preferred_element_type=F32)

        def chunk_pass(j, w, can_be_last):
            j2 = j + 2
            it2 = it + jnp.where(j2 >= n_chunks, 1, 0)

            @pl.when(it2 < total)
            def _():
                for cp in w_copies(jnp.minimum(it2, n_items - 1), j2 & (n_chunks - 1), w):
                    cp.start()

            j1 = j + 1
            it1 = it + jnp.where(j1 >= n_chunks, 1, 0)

            @pl.when(it1 < total)
            def _():
                for cp in w_copies(jnp.minimum(it1, n_items - 1), j1 & (n_chunks - 1), 1 - w):
                    cp.wait()

            cast_weights(1 - w)

            ha_ref, hb_ref = h_refs[w]
            ha_next = h_refs[1 - w][0]
            n_pairs = (nsub - 1) // 2

            def pair(i, c):
                s = 2 * i
                up_proj(s + 1, hb_ref, w)
                down_proj(s, ha_ref, w, j)
                up_proj(s + 2, ha_ref, w)
                down_proj(s + 1, hb_ref, w, j)
                return c
            lax.fori_loop(0, n_pairs, pair, 0)
            s = 2 * n_pairs

            def tail(with_next):
                @pl.when(s == nsub - 1)
                def _():
                    if with_next:
                        up_proj(0, ha_next, 1 - w)
                    down_proj(s, ha_ref, w, j)

                @pl.when(s < nsub - 1)
                def _():
                    up_proj(s + 1, hb_ref, w)
                    down_proj(s, ha_ref, w, j)
                    if with_next:
                        up_proj(0, ha_next, 1 - w)
                    down_proj(s + 1, hb_ref, w, j)

            if can_be_last:
                @pl.when(j < n_chunks - 1)
                def _():
                    tail(True)

                @pl.when(j == n_chunks - 1)
                def _():
                    tail(False)
            else:
                tail(True)

        up_proj(0, h_refs[0][0], 0)

        def chunk_pair(p, c):
            chunk_pass(2 * p, 0, False)
            chunk_pass(2 * p + 1, 1, True)
            return c
        lax.fori_loop(0, n_chunks // 2, chunk_pair, 0)

    @pl.when(it > 0)
    def _():
        y_drain(it - 1)

    def send(s, c):
        y_copy(it, s).start()
        return c
    lax.fori_loop(0, nsub, send, 0)

    @pl.when(it == n_items - 1)
    def _():
        y_drain(it)

        zero_ref[...] = jnp.zeros_like(zero_ref)

        def fill(start_dma):
            def per_tile(r, c):
                cp = _tile_copy(zero_ref, ys_ref, meta_ref[0] + r, zsem)
                cp.start() if start_dma else cp.wait()
                return c
            lax.fori_loop(0, meta_ref[1], per_tile, 0)
        fill(True)
        fill(False)


def _moe(item_e, item_start, item_nsub, meta, xs, w_gate_up, b_gate_up, w_down, b_down):
    n_items = item_e.shape[0]
    n_rows = xs.shape[0]
    n_chunks = D_FF // MOE_F
    grid_spec = pltpu.PrefetchScalarGridSpec(
        num_scalar_prefetch=4,
        grid=(n_items,),
        in_specs=[
            pl.BlockSpec(memory_space=pl.ANY),
            pl.BlockSpec(memory_space=pl.ANY),
            pl.BlockSpec((1, n_chunks, 2 * MOE_F), lambda it, ie, ist, ins, mt: (ie[it], 0, 0)),
            pl.BlockSpec(memory_space=pl.ANY),
            pl.BlockSpec((1, 1, D_MODEL), lambda it, ie, ist, ins, mt: (ie[it], 0, 0)),
        ],
        out_specs=pl.BlockSpec(memory_space=pl.ANY),
        scratch_shapes=[
            pltpu.VMEM((D_MODEL, 2 * MOE_F), F32), pltpu.VMEM((D_MODEL, 2 * MOE_F), F32),
            pltpu.VMEM((MOE_F, D_MODEL), F32), pltpu.VMEM((MOE_F, D_MODEL), F32),
            pltpu.VMEM((D_MODEL, 2 * MOE_F), BF16), pltpu.VMEM((D_MODEL, 2 * MOE_F), BF16),
            pltpu.VMEM((MOE_F, D_MODEL), BF16), pltpu.VMEM((MOE_F, D_MODEL), BF16),
            pltpu.VMEM((D_MODEL // LANES, MOE_F, LANES), F32),
            pltpu.VMEM((2, MOE_BM, D_MODEL), F32),
            pltpu.VMEM((MOE_BM, D_MODEL), BF16),
            pltpu.VMEM((2, MOE_SUB, D_MODEL), F32),
            pltpu.VMEM((MOE_SUB, 2 * MOE_F), F32), pltpu.VMEM((MOE_SUB, 2 * MOE_F), F32),
            pltpu.VMEM((MOE_SUB, 2 * MOE_F), F32), pltpu.VMEM((MOE_SUB, 2 * MOE_F), F32),
            pltpu.VMEM((SUBLANES, D_MODEL), F32),
            pltpu.SemaphoreType.DMA((2,)),
            pltpu.SemaphoreType.DMA((2,)),
            pltpu.SemaphoreType.DMA((2,)),
            pltpu.SemaphoreType.DMA((2,)),
            pltpu.SemaphoreType.DMA,
        ],
    )
    return pl.pallas_call(
        functools.partial(_moe_kernel, n_items=n_items, n_chunks=n_chunks),
        grid_spec=grid_spec,
        out_shape=jax.ShapeDtypeStruct((n_rows, D_MODEL), F32),
        compiler_params=_params("arbitrary"),
        name="moe",
    )(item_e, item_start, item_nsub, meta, xs, w_gate_up, b_gate_up.reshape(N_EXPERTS, n_chunks, 2 * MOE_F),
      w_down, b_down.reshape(N_EXPERTS, 1, D_MODEL))


def _combine_kernel(dest_ref, ys_ref, h_ref, gate_ref, p_ref, wple_ref, pg_ref, wpg_ref, o_ref, ybuf_ref, sem, *, t):
    i = pl.program_id(0)
    n = pl.num_programs(0)
    tb = ROW_TB
    slot = i & 1

    def row_gather(blk, sl, tok, k):
        return pltpu.make_async_copy(ys_ref.at[pl.ds(dest_ref[k * t + blk * tb + tok], 1)],
                                     ybuf_ref.at[sl, k, pl.ds(tok, 1)], sem.at[sl])

    def gather(blk, sl, start):
        def per_token(tok, c):
            for k in range(TOP_K):
                cp = row_gather(blk, sl, tok, k)
                cp.start() if start else cp.wait()
            return c
        lax.fori_loop(0, tb, per_token, 0, unroll=ISSUE_UNROLL)

    @pl.when(i == 0)
    def _():
        gather(0, 0, True)

    @pl.when(i + 1 < n)
    def _():
        gather(i + 1, 1 - slot, True)

    gather(i, slot, False)

    gates = gate_ref[...]
    h2 = h_ref[...]
    for k in range(TOP_K):
        h2 = h2 + ybuf_ref[slot, k] * gates[:, k:k + 1]
    gate = jax.nn.sigmoid(jnp.dot(_rms(h2, pg_ref[...]).astype(BF16), wpg_ref[...], preferred_element_type=F32))
    o_ref[...] = h2 + jnp.dot(p_ref[...].astype(BF16), wple_ref[...], preferred_element_type=F32) * gate


def _combine(dest_flat, ys, h, gates_tk, p2, w_ple_bf, ple_norm_g, w_pg_bf):
    t = h.shape[0]
    tb = ROW_TB
    grid_spec = pltpu.PrefetchScalarGridSpec(
        num_scalar_prefetch=1,
        grid=(t // tb,),
        in_specs=[
            pl.BlockSpec(memory_space=pl.ANY),
            pl.BlockSpec((tb, D_MODEL), lambda i, d: (i, 0)),
            pl.BlockSpec((tb, TOP_K), lambda i, d: (i, 0)),
            pl.BlockSpec((tb, D_PLE), lambda i, d: (i, 0)),
            pl.BlockSpec((D_PLE, D_MODEL), lambda i, d: (0, 0)),
            pl.BlockSpec((1, D_MODEL), lambda i, d: (0, 0)),
            pl.BlockSpec((D_MODEL, D_MODEL), lambda i, d: (0, 0)),
        ],
        out_specs=pl.BlockSpec((tb, D_MODEL), lambda i, d: (i, 0)),
        scratch_shapes=[pltpu.VMEM((2, TOP_K, tb, D_MODEL), F32), pltpu.SemaphoreType.DMA((2,))],
    )
    return pl.pallas_call(
        functools.partial(_combine_kernel, t=t),
        grid_spec=grid_spec,
        out_shape=jax.ShapeDtypeStruct((t, D_MODEL), F32),
        compiler_params=_params("arbitrary"),
        name="combine",
    )(dest_flat, ys, h, gates_tk, p2, w_ple_bf, ple_norm_g.reshape(1, D_MODEL), w_pg_bf)


def _plan(idx_t, rank_t, counts, t):
    tk = t * TOP_K
    n_items = tk // MOE_BM + N_EXPERTS
    n_rows = tk + N_EXPERTS * ROW_ALIGN + MOE_SUB
    seg = (counts + ROW_ALIGN - 1) // ROW_ALIGN * ROW_ALIGN
    seg_end = jnp.cumsum(seg)
    seg_start = seg_end - seg
    eids = jnp.arange(N_EXPERTS, dtype=I32)[:, None, None]
    first = jnp.sum(jnp.where(idx_t[None] == eids, seg_start[:, None, None], 0), axis=0)
    dest = (first + rank_t).reshape(-1).astype(I32)

    per_e = (counts + MOE_BM - 1) // MOE_BM
    it_end = jnp.cumsum(per_e)
    it_off = it_end - per_e
    total = it_end[-1]
    its = jnp.arange(n_items, dtype=I32)
    valid = its < total
    its_c = jnp.minimum(its, total - 1)
    e_of = jnp.minimum(jnp.searchsorted(it_end, its_c, side="right"), N_EXPERTS - 1).astype(I32)
    li = its_c - it_off[e_of]
    rows = jnp.clip(counts[e_of] - li * MOE_BM, 0, MOE_BM)
    nsub = jnp.where(valid, (rows + MOE_SUB - 1) // MOE_SUB, 0).astype(I32)
    start = (seg_start[e_of] + li * MOE_BM).astype(I32)

    used_end = start[total - 1] + MOE_SUB * ((rows[total - 1] + MOE_SUB - 1) // MOE_SUB)
    moe_meta = jnp.stack([used_end // SUBLANES, (n_rows - used_end) // SUBLANES, total]).astype(I32)
    zmeta = jnp.concatenate([
        seg_start + counts, seg - counts,
        jnp.stack([seg_end[-1] // SUBLANES, (n_rows - seg_end[-1]) // SUBLANES]),
    ]).astype(I32)
    return dest, zmeta, e_of, start, nsub, moe_meta, n_rows


def kernel(x, p, ln1_g, w_in, q_norm_g, k_norm_g, sinks, sg_ln_g, sg_ln_b, w_spatial, b_spatial, attn_out_g,
           sg_out_g, w_out, ln2_g, w_router, b_router, w_gate_up, b_gate_up, w_down, b_down, w_ple, ple_norm_g,
           w_ple_gate):
    bsz, s, d = x.shape
    assert bsz == 1 and d == D_MODEL and p.shape[0] == 1
    t = bsz * s
    h = x.reshape(t, d)
    li = 0
    proj = _in_proj(h, ln1_g[li], w_in[li].astype(BF16))
    h, xn2 = _mixer(proj, h, sinks[li], q_norm_g[li], k_norm_g[li], sg_ln_g[li], sg_ln_b[li], w_spatial[li],
                    b_spatial[li], attn_out_g[li], sg_out_g[li], w_out[li].astype(BF16), ln2_g[li])
    idx_t, gate_t, rank_t, cnt = _router(xn2, w_router[li], b_router[li])
    dest, zmeta, item_e, item_start, item_nsub, moe_meta, n_rows = _plan(idx_t, rank_t, cnt[:, 0], t)
    xs = _dispatch(dest, zmeta, xn2, n_rows)
    ys = _moe(item_e, item_start, item_nsub, moe_meta, xs, w_gate_up[li], b_gate_up[li], w_down[li], b_down[li])
    out = _combine(dest, ys, h, gate_t.T, p[li].reshape(t, D_PLE), w_ple[li].astype(BF16), ple_norm_g[li],
                   w_ple_gate[li].astype(BF16))
    return out.reshape(bsz, s, d)
```

```python
import functools

import jax
import jax.numpy as jnp
import numpy as np
from jax import lax
from jax.experimental import pallas as pl
from jax.experimental.pallas import tpu as pltpu

F32 = jnp.float32
BF16 = jnp.bfloat16
I32 = jnp.int32

D_MODEL = 2048
D_PLE = 256
N_Q_HEADS = 16
N_KV_HEADS = 4
HEAD_DIM = 64
WINDOW = 128
N_SG_HEADS = 8
SG_HEAD_DIM = 128
D_ATTN = N_Q_HEADS * HEAD_DIM
D_KV = N_KV_HEADS * HEAD_DIM
D_SG = N_SG_HEADS * SG_HEAD_DIM
D_MIX = D_ATTN + D_SG
D_IN = D_ATTN + 2 * D_KV + 2 * D_SG
N_EXPERTS = 32
TOP_K = 4
D_FF = D_MODEL
SWIGLU_LIMIT = 7.0
SWIGLU_ALPHA = 1.702
EPS = 1e-6
NEG_INF = -1e30

SUBLANES = 8
LANES = 128
MXU_DIM = 256
VMEM_LIMIT_BYTES = 56 * 1024 * 1024

IN_BM = 512
MIX_R = 512
ROUTE_TB = 512
ROW_TB = 256
MOE_SUB = 256
MOE_BM = 1280
MOE_F = 256
ROW_ALIGN = SUBLANES


def _rms(x, g):
    ms = jnp.mean(x * x, axis=-1, keepdims=True)
    return x * lax.rsqrt(ms + EPS) * g


def _gelu(x):
    return 0.5 * x * (1.0 + lax.erf(x * np.float32(np.sqrt(0.5))))


def _params(*sem):
    return pltpu.CompilerParams(dimension_semantics=sem, vmem_limit_bytes=VMEM_LIMIT_BYTES)


def _in_proj_kernel(x_ref, g_ref, w_ref, o_ref):
    o_ref[...] = jnp.dot(_rms(x_ref[...], g_ref[...]).astype(BF16), w_ref[...], preferred_element_type=F32)


def _in_proj(x2, ln1_g, w_in_bf):
    t = x2.shape[0]
    return pl.pallas_call(
        _in_proj_kernel,
        grid=(t // IN_BM,),
        in_specs=[
            pl.BlockSpec((IN_BM, D_MODEL), lambda i: (i, 0)),
            pl.BlockSpec((1, D_MODEL), lambda i: (0, 0)),
            pl.BlockSpec((D_MODEL, D_IN), lambda i: (0, 0), pipeline_mode=pl.Buffered(1)),
        ],
        out_specs=pl.BlockSpec((IN_BM, D_IN), lambda i: (i, 0)),
        out_shape=jax.ShapeDtypeStruct((t, D_IN), F32),
        compiler_params=_params("arbitrary"),
        name="in_proj",
    )(x2, ln1_g.reshape(1, D_MODEL), w_in_bf)


def _head_mean_sq(t, bd):
    tt = t * t
    hi = tt.astype(BF16)
    lo = (tt - hi.astype(F32)).astype(BF16)
    w = bd.shape[0]
    parts = []
    for c in range(t.shape[1] // w):
        cs = slice(c * w, (c + 1) * w)
        parts.append(jnp.dot(hi[:, cs], bd, preferred_element_type=F32)
                     + jnp.dot(lo[:, cs], bd, preferred_element_type=F32))
    return jnp.concatenate(parts, axis=1) * (1.0 / HEAD_DIM)


def _mixer_kernel(sinks_ref, proj_ref, pkv_ref, x_ref, qg_ref, kg_ref, bd_ref, lng_ref, lnb_ref,
                  wsp_ref, bsp_ref, ag_ref, sgg_ref, wout_ref, ln2_ref,
                  h_ref, xn2_ref, a_ref, merged_ref):
    i = pl.program_id(0)
    nsb = MIX_R // WINDOW
    grp = N_Q_HEADS // N_KV_HEADS

    bd = bd_ref[...]
    kv_all = jnp.concatenate([pkv_ref[...], proj_ref[:, D_ATTN:D_ATTN + 2 * D_KV]], axis=0)
    k_all = kv_all[:, :D_KV]
    kn = (k_all * lax.rsqrt(_head_mean_sq(k_all, bd) + EPS) * kg_ref[...]).astype(BF16)
    vb = kv_all[:, D_KV:].astype(BF16)
    q_all = proj_ref[:, :D_ATTN]
    qn = (q_all * lax.rsqrt(_head_mean_sq(q_all, bd) + EPS) * qg_ref[...] * (HEAD_DIM ** -0.5)).astype(BF16)

    rows = lax.broadcasted_iota(I32, (grp * WINDOW, 2 * WINDOW), 0)
    cols = lax.broadcasted_iota(I32, (grp * WINDOW, 2 * WINDOW), 1)
    rel = cols - (rows & (WINDOW - 1))
    band = (rel >= 1) & (rel <= WINDOW)
    cur = cols >= WINDOW
    rblk = lax.broadcasted_iota(I32, (grp * WINDOW, 1), 0) // WINDOW

    tr = lax.broadcasted_iota(I32, (WINDOW, WINDOW), 0)
    tc = lax.broadcasted_iota(I32, (WINDOW, WINDOW), 1)
    wsp = [jnp.where(tc <= tr, wsp_ref[g], 0.0).astype(BF16) for g in range(N_SG_HEADS)]

    for sb in range(nsb):
        r0 = sb * WINDOW
        has_prev = (i * nsb + sb) > 0
        ok = band & (cur | has_prev)
        for g in range(N_KV_HEADS):
            kc = kn[r0:r0 + 2 * WINDOW, g * HEAD_DIM:(g + 1) * HEAD_DIM]
            vc = vb[r0:r0 + 2 * WINDOW, g * HEAD_DIM:(g + 1) * HEAD_DIM]
            qg = jnp.concatenate(
                [qn[r0:r0 + WINDOW, (g * grp + j) * HEAD_DIM:(g * grp + j + 1) * HEAD_DIM] for j in range(grp)],
                axis=0)
            s = lax.dot_general(qg, kc, (((1,), (1,)), ((), ())), preferred_element_type=F32)
            s = jnp.where(ok, s, NEG_INF)
            sink = jnp.full((grp * WINDOW, 1), sinks_ref[g * grp], F32)
            for j in range(1, grp):
                sink = jnp.where(rblk == j, sinks_ref[g * grp + j], sink)
            m = jnp.maximum(jnp.max(s, axis=-1, keepdims=True), sink)
            p = jnp.exp(s - m)
            den = jnp.sum(p, axis=-1, keepdims=True) + jnp.exp(sink - m)
            o = jnp.dot((p / den).astype(BF16), vc, preferred_element_type=F32)
            for j in range(grp):
                hh = g * grp + j
                a_ref[r0:r0 + WINDOW, hh * HEAD_DIM:(hh + 1) * HEAD_DIM] = o[j * WINDOW:(j + 1) * WINDOW]
        merged_ref[r0:r0 + WINDOW, :D_ATTN] = _rms(a_ref[r0:r0 + WINDOW, :], ag_ref[...]).astype(BF16)

        u = _gelu(proj_ref[r0:r0 + WINDOW, D_ATTN + 2 * D_KV:D_ATTN + 2 * D_KV + D_SG])
        vg = _gelu(proj_ref[r0:r0 + WINDOW, D_ATTN + 2 * D_KV + D_SG:])
        mu = jnp.mean(vg, axis=-1, keepdims=True)
        vc0 = vg - mu
        var = jnp.mean(vc0 * vc0, axis=-1, keepdims=True)
        vn = (vc0 * lax.rsqrt(var + EPS) * lng_ref[...] + lnb_ref[...]).astype(BF16)
        mixed = jnp.concatenate(
            [jnp.dot(wsp[g], vn[:, g * SG_HEAD_DIM:(g + 1) * SG_HEAD_DIM], preferred_element_type=F32)
             for g in range(N_SG_HEADS)], axis=1) + bsp_ref[...]
        merged_ref[r0:r0 + WINDOW, D_ATTN:] = _rms(u * mixed, sgg_ref[...]).astype(BF16)

    hblk = x_ref[...] + jnp.dot(merged_ref[...], wout_ref[...], preferred_element_type=F32)
    h_ref[...] = hblk
    xn2_ref[...] = _rms(hblk, ln2_ref[...])


def _mixer(proj, x2, sinks, q_norm_g, k_norm_g, sg_ln_g, sg_ln_b, w_spatial, b_spatial,
           attn_out_g, sg_out_g, w_out_bf, ln2_g):
    t = x2.shape[0]
    nsb = MIX_R // WINDOW
    kvw = 2 * D_KV
    qg = jnp.tile(q_norm_g, N_Q_HEADS).reshape(1, D_ATTN)
    kg = jnp.tile(k_norm_g, N_KV_HEADS).reshape(1, D_KV)
    bd = jnp.asarray(np.kron(np.eye(MXU_DIM // HEAD_DIM), np.ones((HEAD_DIM, HEAD_DIM))), BF16)
    bsp = jnp.repeat(b_spatial.T, SG_HEAD_DIM, axis=1)
    row = lambda v, d: v.reshape(1, d)
    const = lambda shape: pl.BlockSpec(shape, lambda i, s: tuple(0 for _ in shape), pipeline_mode=pl.Buffered(1))
    grid_spec = pltpu.PrefetchScalarGridSpec(
        num_scalar_prefetch=1,
        grid=(t // MIX_R,),
        in_specs=[
            pl.BlockSpec((MIX_R, D_IN), lambda i, s: (i, 0)),
            pl.BlockSpec((WINDOW, kvw), lambda i, s: (jnp.maximum(i * nsb - 1, 0), D_ATTN // kvw)),
            pl.BlockSpec((MIX_R, D_MODEL), lambda i, s: (i, 0)),
            const((1, D_ATTN)), const((1, D_KV)), const((MXU_DIM, MXU_DIM)),
            const((1, D_SG)), const((1, D_SG)),
            const((N_SG_HEADS, WINDOW, WINDOW)), const((WINDOW, D_SG)),
            const((1, D_ATTN)), const((1, D_SG)),
            const((D_MIX, D_MODEL)), const((1, D_MODEL)),
        ],
        out_specs=[
            pl.BlockSpec((MIX_R, D_MODEL), lambda i, s: (i, 0)),
            pl.BlockSpec((MIX_R, D_MODEL), lambda i, s: (i, 0)),
        ],
        scratch_shapes=[pltpu.VMEM((MIX_R, D_ATTN), F32), pltpu.VMEM((MIX_R, D_MIX), BF16)],
    )
    return pl.pallas_call(
        _mixer_kernel,
        grid_spec=grid_spec,
        out_shape=[jax.ShapeDtypeStruct((t, D_MODEL), F32), jax.ShapeDtypeStruct((t, D_MODEL), F32)],
        compiler_params=_params("arbitrary"),
        name="mixer",
    )(sinks, proj, proj, x2, qg, kg, bd, row(sg_ln_g, D_SG), row(sg_ln_b, D_SG), w_spatial, bsp,
      row(attn_out_g, D_ATTN), row(sg_out_g, D_SG), w_out_bf, row(ln2_g, D_MODEL))


def _router_kernel(xn_ref, wr_ref, br_ref, idx_ref, gate_ref, rank_ref, cnt_ref, carry_ref):
    i = pl.program_id(0)
    tb = ROUTE_TB

    @pl.when(i == 0)
    def _():
        carry_ref[...] = jnp.zeros_like(carry_ref)

    def split(v):
        hi = v.astype(BF16)
        return hi, (v - hi.astype(F32)).astype(BF16)

    def nt(a, b):
        return lax.dot_general(a, b, (((1,), (1,)), ((), ())), preferred_element_type=F32)

    w_hi, w_lo = split(wr_ref[...])
    x_hi, x_lo = split(xn_ref[...])
    logits = nt(w_hi, x_hi) + (nt(w_hi, x_lo) + nt(w_lo, x_hi)) + br_ref[...]
    eid = lax.broadcasted_iota(I32, (N_EXPERTS, tb), 0)
    work = logits
    vals, idxs = [], []
    for _ in range(TOP_K):
        m = jnp.max(work, axis=0, keepdims=True)
        sel = jnp.min(jnp.where(work == m, eid, N_EXPERTS), axis=0, keepdims=True)
        vals.append(m)
        idxs.append(sel)
        work = jnp.where(eid == sel, -jnp.inf, work)
    ex = [jnp.exp(v - vals[0]) for v in vals]
    den = ex[0] + ex[1] + ex[2] + ex[3]
    chosen = (eid == idxs[0]) | (eid == idxs[1]) | (eid == idxs[2]) | (eid == idxs[3])
    onehot = jnp.where(chosen, 1.0, 0.0)

    tr = lax.broadcasted_iota(I32, (tb, tb), 0)
    tc = lax.broadcasted_iota(I32, (tb, tb), 1)
    before = jnp.where(tr < tc, 1.0, 0.0).astype(BF16)
    cnt = jnp.dot(onehot.astype(BF16), before, preferred_element_type=F32) + carry_ref[:, :1]
    for k in range(TOP_K):
        idx_ref[k:k + 1, :] = idxs[k]
        gate_ref[k:k + 1, :] = ex[k] / den
        rank_ref[k:k + 1, :] = jnp.sum(jnp.where(eid == idxs[k], cnt, 0.0), axis=0, keepdims=True).astype(I32)
    carry_ref[...] = carry_ref[...] + jnp.sum(onehot, axis=1, keepdims=True)
    cnt_ref[...] = carry_ref[...].astype(I32)


def _router(xn2, w_router, b_router):
    t = xn2.shape[0]
    tb = ROUTE_TB
    return pl.pallas_call(
        _router_kernel,
        grid=(t // tb,),
        in_specs=[
            pl.BlockSpec((tb, D_MODEL), lambda i: (i, 0)),
            pl.BlockSpec((N_EXPERTS, D_MODEL), lambda i: (0, 0)),
            pl.BlockSpec((N_EXPERTS, 1), lambda i: (0, 0)),
        ],
        out_specs=[
            pl.BlockSpec((TOP_K, tb), lambda i: (0, i)),
            pl.BlockSpec((TOP_K, tb), lambda i: (0, i)),
            pl.BlockSpec((TOP_K, tb), lambda i: (0, i)),
            pl.BlockSpec((N_EXPERTS, LANES), lambda i: (0, 0)),
        ],
        out_shape=[
            jax.ShapeDtypeStruct((TOP_K, t), I32),
            jax.ShapeDtypeStruct((TOP_K, t), F32),
            jax.ShapeDtypeStruct((TOP_K, t), I32),
            jax.ShapeDtypeStruct((N_EXPERTS, LANES), I32),
        ],
        scratch_shapes=[pltpu.VMEM((N_EXPERTS, LANES), F32)],
        compiler_params=_params("arbitrary"),
        name="router",
    )(xn2, w_router.T, b_router.reshape(N_EXPERTS, 1))


def _row_copy(src, s, dst, d, sem):
    return pltpu.make_async_copy(src.at[pl.ds(s, 1)], dst.at[pl.ds(d, 1)], sem)


def _tile_copy(src, dst, d8, sem):
    return pltpu.make_async_copy(src, dst.at[pl.ds(pl.multiple_of(d8 * SUBLANES, SUBLANES), SUBLANES)], sem)


def _dispatch_kernel(dest_ref, zmeta_ref, xn_ref, xs_ref, zero_ref, sem, zsem, *, t):
    i = pl.program_id(0)
    tb = ROW_TB

    @pl.when(i == 0)
    def _():
        zero_ref[...] = jnp.zeros_like(zero_ref)

        def pads(start):
            def per_expert(e, c):
                def per_row(r, c2):
                    cp = _row_copy(zero_ref, 0, xs_ref, zmeta_ref[e] + r, zsem)
                    cp.start() if start else cp.wait()
                    return c2
                return lax.fori_loop(0, zmeta_ref[N_EXPERTS + e], per_row, c)
            lax.fori_loop(0, N_EXPERTS, per_expert, 0)

            def per_tile(r, c):
                cp = _tile_copy(zero_ref, xs_ref, zmeta_ref[2 * N_EXPERTS] + r, zsem)
                cp.start() if start else cp.wait()
                return c
            lax.fori_loop(0, zmeta_ref[2 * N_EXPERTS + 1], per_tile, 0)

        pads(True)
        pads(False)

    def rows(start):
        def per_group(g, c):
            for u in range(SUBLANES):
                for k in range(TOP_K):
                    d = dest_ref[k * t + i * tb + g * SUBLANES + u]
                    cp = _row_copy(xn_ref.at[g], u, xs_ref, d, sem)
                    cp.start(priority=k % 2) if start else cp.wait()
            return c
        lax.fori_loop(0, tb // SUBLANES, per_group, 0)
    rows(True)
    rows(False)


def _dispatch(dest_flat, zmeta, xn2, n_rows):
    t = xn2.shape[0]
    grid_spec = pltpu.PrefetchScalarGridSpec(
        num_scalar_prefetch=2,
        grid=(t // ROW_TB,),
        in_specs=[pl.BlockSpec((ROW_TB // SUBLANES, SUBLANES, D_MODEL), lambda i, d, z: (i, 0, 0))],
        out_specs=pl.BlockSpec(memory_space=pl.ANY),
        scratch_shapes=[pltpu.VMEM((SUBLANES, D_MODEL), F32), pltpu.SemaphoreType.DMA, pltpu.SemaphoreType.DMA],
    )
    return pl.pallas_call(
        functools.partial(_dispatch_kernel, t=t),
        grid_spec=grid_spec,
        out_shape=jax.ShapeDtypeStruct((n_rows, D_MODEL), F32),
        compiler_params=_params("arbitrary"),
        name="dispatch",
    )(dest_flat, zmeta, xn2.reshape(t // SUBLANES, SUBLANES, D_MODEL))


def _moe_kernel(ie_ref, ist_ref, ins_ref, meta_ref, xs_ref, wgu_hbm, bgu_ref, wd_hbm, bd_ref, ys_ref,
                rawg0_ref, rawg1_ref, rawd0_ref, rawd1_ref, wgu0_ref, wgu1_ref, wd0_ref, wd1_ref,
                wdmix_ref, acc_ref, xbf_ref, xst_ref, ha0_ref, hb0_ref, ha1_ref, hb1_ref,
                zero_ref, gsem, dsem, xsem, osem, zsem, *, n_items, n_chunks):
    h_refs = ((ha0_ref, hb0_ref), (ha1_ref, hb1_ref))
    rawg_ref, rawd_ref = (rawg0_ref, rawg1_ref), (rawd0_ref, rawd1_ref)
    wgu_bf_ref, wd_bf_ref = (wgu0_ref, wgu1_ref), (wd0_ref, wd1_ref)
    it = pl.program_id(0)
    nsub = ins_ref[it]
    start = ist_ref[it]
    total = meta_ref[2]
    slot_acc = it & 1
    f = MOE_F
    half = f // 2

    def x_copy(item_start, s, slot):
        r = pl.multiple_of(item_start + s * MOE_SUB, SUBLANES)
        return pltpu.make_async_copy(xs_ref.at[pl.ds(r, MOE_SUB)], xst_ref.at[slot], xsem.at[slot])

    def y_copy(item, s):
        r = pl.multiple_of(ist_ref[item] + s * MOE_SUB, SUBLANES)
        return pltpu.make_async_copy(acc_ref.at[item & 1, pl.ds(s * MOE_SUB, MOE_SUB)],
                                     ys_ref.at[pl.ds(r, MOE_SUB)], osem.at[item & 1])

    def y_drain(item):
        def one(s, c):
            y_copy(item, s).wait()
            return c
        lax.fori_loop(0, ins_ref[item], one, 0)

    def w_copies(item, j, slot):
        e = ie_ref[item]
        cg = pl.multiple_of(j * 2 * f, 2 * f)
        cd = pl.multiple_of(j * f, f)
        return (pltpu.make_async_copy(wgu_hbm.at[e, :, pl.ds(cg, 2 * f)], rawg_ref[slot], gsem.at[slot]),
                pltpu.make_async_copy(wd_hbm.at[e, pl.ds(cd, f), :], rawd_ref[slot], dsem.at[slot]))

    def cast_weights(slot):
        wgu_bf_ref[slot][...] = rawg_ref[slot][...].astype(BF16)
        for c in range(D_MODEL // LANES):
            cs = slice(c * LANES, (c + 1) * LANES)
            wdmix_ref[c, pl.ds(0, half, stride=2), :] = rawd_ref[slot][:half, cs]
            wdmix_ref[c, pl.ds(1, half, stride=2), :] = rawd_ref[slot][half:, cs]
            wd_bf_ref[slot][:, cs] = wdmix_ref[c].astype(BF16)

    @pl.when(it == 0)
    def _():
        for c in range(2):
            for cp in w_copies(0, c, c):
                cp.start()
        for cp in w_copies(0, 0, 0):
            cp.wait()
        cast_weights(0)
        for s in range(2):
            @pl.when(s < nsub)
            def _():
                x_copy(start, s, s).start()

    @pl.when(nsub > 0)
    def _():
        def stage(s, c):
            slot = s & 1
            r = pl.multiple_of(s * MOE_SUB, MOE_SUB)
            x_copy(start, s, slot).wait()
            xbf_ref[pl.ds(r, MOE_SUB), :] = xst_ref[slot].astype(BF16)
            acc_ref[slot_acc, pl.ds(r, MOE_SUB), :] = jnp.broadcast_to(bd_ref[0], (MOE_SUB, D_MODEL))

            @pl.when(s + 2 < nsub)
            def _():
                x_copy(start, s + 2, slot).start()
            return c
        lax.fori_loop(0, nsub, stage, 0)

        nxt = jnp.minimum(it + 1, n_items - 1)
        for s in range(2):
            @pl.when((it + 1 < n_items) & (s < ins_ref[nxt]))
            def _():
                x_copy(ist_ref[nxt], s, s).start()

        even = (lax.broadcasted_iota(I32, (MOE_SUB, f), 1) & 1) == 0

        def up_proj(s, h_ref, w):
            r = pl.multiple_of(s * MOE_SUB, MOE_SUB)
            h_ref[...] = jnp.dot(xbf_ref[pl.ds(r, MOE_SUB), :], wgu_bf_ref[w][...], preferred_element_type=F32)

        def down_proj(s, h_ref, w, j):
            r = pl.multiple_of(s * MOE_SUB, MOE_SUB)
            h = h_ref[...] + bgu_ref[0, pl.ds(j, 1), :]
            gate = jnp.minimum(h, SWIGLU_LIMIT)
            glu = gate * jax.nn.sigmoid(gate * SWIGLU_ALPHA)
            up = jnp.clip(h, -SWIGLU_LIMIT, SWIGLU_LIMIT) + 1.0
            prod = glu * pltpu.roll(up, 2 * f - 1, axis=1)
            act = jnp.where(even, prod[:, :f], pltpu.roll(prod, 1, axis=1)[:, f:])
            acc_ref[slot_acc, pl.ds(r, MOE_SUB), :] += jnp.dot(act.astype(BF16), wd_bf_ref[w][...],
                                                                preferred_element_type=F32)

        def chunk_pass(j, w, can_be_last):
            j2 = j + 2
            it2 = it + jnp.where(j2 >= n_chunks, 1, 0)

            @pl.when(it2 < total)
            def _():
                for cp in w_copies(jnp.minimum(it2, n_items - 1), j2 & (n_chunks - 1), w):
                    cp.start()

            j1 = j + 1
            it1 = it + jnp.where(j1 >= n_chunks, 1, 0)

            @pl.when(it1 < total)
            def _():
                for cp in w_copies(jnp.minimum(it1, n_items - 1), j1 & (n_chunks - 1), 1 - w):
                    cp.wait()

            cast_weights(1 - w)

            ha_ref, hb_ref = h_refs[w]
            ha_next = h_refs[1 - w][0]
            n_pairs = (nsub - 1) // 2

            def pair(i, c):
                s = 2 * i
                up_proj(s + 1, hb_ref, w)
                down_proj(s, ha_ref, w, j)
                up_proj(s + 2, ha_ref, w)
                down_proj(s + 1, hb_ref, w, j)
                return c
            lax.fori_loop(0, n_pairs, pair, 0)
            s = 2 * n_pairs

            def tail(with_next):
                @pl.when(s == nsub - 1)
                def _():
                    if with_next:
                        up_proj(0, ha_next, 1 - w)
                    down_proj(s, ha_ref, w, j)

                @pl.when(s < nsub - 1)
                def _():
                    up_proj(s + 1, hb_ref, w)
                    down_proj(s, ha_ref, w, j)
                    if with_next:
                        up_proj(0, ha_next, 1 - w)
                    down_proj(s + 1, hb_ref, w, j)

            if can_be_last:
                @pl.when(j < n_chunks - 1)
                def _():
                    tail(True)

                @pl.when(j == n_chunks - 1)
                def _():
                    tail(False)
            else:
                tail(True)

        up_proj(0, h_refs[0][0], 0)

        def chunk_pair(p, c):
            chunk_pass(2 * p, 0, False)
            chunk_pass(2 * p + 1, 1, True)
            return c
        lax.fori_loop(0, n_chunks // 2, chunk_pair, 0)

    @pl.when(it > 0)
    def _():
        y_drain(it - 1)

    def send(s, c):
        y_copy(it, s).start()
        return c
    lax.fori_loop(0, nsub, send, 0)

    @pl.when(it == n_items - 1)
    def _():
        y_drain(it)

        zero_ref[...] = jnp.zeros_like(zero_ref)

        def fill(start_dma):
            def per_tile(r, c):
                cp = _tile_copy(zero_ref, ys_ref, meta_ref[0] + r, zsem)
                cp.start() if start_dma else cp.wait()
                return c
            lax.fori_loop(0, meta_ref[1], per_tile, 0)
        fill(True)
        fill(False)


def _moe(item_e, item_start, item_nsub, meta, xs, w_gate_up, b_gate_up, w_down, b_down):
    n_items = item_e.shape[0]
    n_rows = xs.shape[0]
    n_chunks = D_FF // MOE_F
    grid_spec = pltpu.PrefetchScalarGridSpec(
        num_scalar_prefetch=4,
        grid=(n_items,),
        in_specs=[
            pl.BlockSpec(memory_space=pl.ANY),
            pl.BlockSpec(memory_space=pl.ANY),
            pl.BlockSpec((1, n_chunks, 2 * MOE_F), lambda it, ie, ist, ins, mt: (ie[it], 0, 0)),
            pl.BlockSpec(memory_space=pl.ANY),
            pl.BlockSpec((1, 1, D_MODEL), lambda it, ie, ist, ins, mt: (ie[it], 0, 0)),
        ],
        out_specs=pl.BlockSpec(memory_space=pl.ANY),
        scratch_shapes=[
            pltpu.VMEM((D_MODEL, 2 * MOE_F), F32), pltpu.VMEM((D_MODEL, 2 * MOE_F), F32),
            pltpu.VMEM((MOE_F, D_MODEL), F32), pltpu.VMEM((MOE_F, D_MODEL), F32),
            pltpu.VMEM((D_MODEL, 2 * MOE_F), BF16), pltpu.VMEM((D_MODEL, 2 * MOE_F), BF16),
            pltpu.VMEM((MOE_F, D_MODEL), BF16), pltpu.VMEM((MOE_F, D_MODEL), BF16),
            pltpu.VMEM((D_MODEL // LANES, MOE_F, LANES), F32),
            pltpu.VMEM((2, MOE_BM, D_MODEL), F32),
            pltpu.VMEM((MOE_BM, D_MODEL), BF16),
            pltpu.VMEM((2, MOE_SUB, D_MODEL), F32),
            pltpu.VMEM((MOE_SUB, 2 * MOE_F), F32), pltpu.VMEM((MOE_SUB, 2 * MOE_F), F32),
            pltpu.VMEM((MOE_SUB, 2 * MOE_F), F32), pltpu.VMEM((MOE_SUB, 2 * MOE_F), F32),
            pltpu.VMEM((SUBLANES, D_MODEL), F32),
            pltpu.SemaphoreType.DMA((2,)),
            pltpu.SemaphoreType.DMA((2,)),
            pltpu.SemaphoreType.DMA((2,)),
            pltpu.SemaphoreType.DMA((2,)),
            pltpu.SemaphoreType.DMA,
        ],
    )
    return pl.pallas_call(
        functools.partial(_moe_kernel, n_items=n_items, n_chunks=n_chunks),
        grid_spec=grid_spec,
        out_shape=jax.ShapeDtypeStruct((n_rows, D_MODEL), F32),
        compiler_params=_params("arbitrary"),
        name="moe",
    )(item_e, item_start, item_nsub, meta, xs, w_gate_up, b_gate_up.reshape(N_EXPERTS, n_chunks, 2 * MOE_F),
      w_down, b_down.reshape(N_EXPERTS, 1, D_MODEL))


def _combine_kernel(dest_ref, ys_ref, h_ref, gate_ref, p_ref, wple_ref, pg_ref, wpg_ref, o_ref, ybuf_ref, sem, *, t):
    i = pl.program_id(0)
    n = pl.num_programs(0)
    tb = ROW_TB
    slot = i & 1

    def gather(blk, sl, start):
        def per_group(g, c):
            for u in range(SUBLANES):
                for k in range(TOP_K):
                    d = dest_ref[k * t + blk * tb + g * SUBLANES + u]
                    cp = _row_copy(ys_ref, d, ybuf_ref.at[sl, k, g], u, sem.at[sl])
                    cp.start() if start else cp.wait()
            return c
        lax.fori_loop(0, tb // SUBLANES, per_group, 0)

    def on_slot(cond, blk, rel, start):
        for sl in range(2):
            @pl.when(cond & (slot == (sl ^ rel)))
            def _():
                gather(blk, sl, start)

    on_slot(i == 0, 0, 0, True)
    on_slot(i + 1 < n, i + 1, 1, True)
    on_slot(i >= 0, i, 0, False)

    gates = gate_ref[...]
    h2 = h_ref[...]
    for k in range(TOP_K):
        h2 = h2 + ybuf_ref[slot, k].reshape(tb, D_MODEL) * gates[:, k:k + 1]
    gate = jax.nn.sigmoid(jnp.dot(_rms(h2, pg_ref[...]).astype(BF16), wpg_ref[...], preferred_element_type=F32))
    o_ref[...] = h2 + jnp.dot(p_ref[...].astype(BF16), wple_ref[...], preferred_element_type=F32) * gate


def _combine(dest_flat, ys, h, gates_tk, p2, w_ple_bf, ple_norm_g, w_pg_bf):
    t = h.shape[0]
    tb = ROW_TB
    grid_spec = pltpu.PrefetchScalarGridSpec(
        num_scalar_prefetch=1,
        grid=(t // tb,),
        in_specs=[
            pl.BlockSpec(memory_space=pl.ANY),
            pl.BlockSpec((tb, D_MODEL), lambda i, d: (i, 0)),
            pl.BlockSpec((tb, TOP_K), lambda i, d: (i, 0)),
            pl.BlockSpec((tb, D_PLE), lambda i, d: (i, 0)),
            pl.BlockSpec((D_PLE, D_MODEL), lambda i, d: (0, 0)),
            pl.BlockSpec((1, D_MODEL), lambda i, d: (0, 0)),
            pl.BlockSpec((D_MODEL, D_MODEL), lambda i, d: (0, 0)),
        ],
        out_specs=pl.BlockSpec((tb, D_MODEL), lambda i, d: (i, 0)),
        scratch_shapes=[pltpu.VMEM((2, TOP_K, tb // SUBLANES, SUBLANES, D_MODEL), F32),
                        pltpu.SemaphoreType.DMA((2,))],
    )
    return pl.pallas_call(
        functools.partial(_combine_kernel, t=t),
        grid_spec=grid_spec,
        out_shape=jax.ShapeDtypeStruct((t, D_MODEL), F32),
        compiler_params=_params("arbitrary"),
        name="combine",
    )(dest_flat, ys, h, gates_tk, p2, w_ple_bf, ple_norm_g.reshape(1, D_MODEL), w_pg_bf)


def _plan(idx_t, rank_t, counts, t):
    tk = t * TOP_K
    n_items = tk // MOE_BM + N_EXPERTS
    n_rows = tk + N_EXPERTS * ROW_ALIGN + MOE_SUB
    seg = (counts + ROW_ALIGN - 1) // ROW_ALIGN * ROW_ALIGN
    seg_end = jnp.cumsum(seg)
    seg_start = seg_end - seg
    eids = jnp.arange(N_EXPERTS, dtype=I32)[:, None, None]
    first = jnp.sum(jnp.where(idx_t[None] == eids, seg_start[:, None, None], 0), axis=0)
    dest = (first + rank_t).reshape(-1).astype(I32)

    per_e = (counts + MOE_BM - 1) // MOE_BM
    it_end = jnp.cumsum(per_e)
    it_off = it_end - per_e
    total = it_end[-1]
    its = jnp.arange(n_items, dtype=I32)
    valid = its < total
    its_c = jnp.minimum(its, total - 1)
    e_of = jnp.minimum(jnp.searchsorted(it_end, its_c, side="right"), N_EXPERTS - 1).astype(I32)
    li = its_c - it_off[e_of]
    rows = jnp.clip(counts[e_of] - li * MOE_BM, 0, MOE_BM)
    nsub = jnp.where(valid, (rows + MOE_SUB - 1) // MOE_SUB, 0).astype(I32)
    start = (seg_start[e_of] + li * MOE_BM).astype(I32)

    used_end = start[total - 1] + MOE_SUB * ((rows[total - 1] + MOE_SUB - 1) // MOE_SUB)
    moe_meta = jnp.stack([used_end // SUBLANES, (n_rows - used_end) // SUBLANES, total]).astype(I32)
    zmeta = jnp.concatenate([
        seg_start + counts, seg - counts,
        jnp.stack([seg_end[-1] // SUBLANES, (n_rows - seg_end[-1]) // SUBLANES]),
    ]).astype(I32)
    return dest, zmeta, e_of, start, nsub, moe_meta, n_rows


def kernel(x, p, ln1_g, w_in, q_norm_g, k_norm_g, sinks, sg_ln_g, sg_ln_b, w_spatial, b_spatial, attn_out_g,
           sg_out_g, w_out, ln2_g, w_router, b_router, w_gate_up, b_gate_up, w_down, b_down, w_ple, ple_norm_g,
           w_ple_gate):
    bsz, s, d = x.shape
    assert bsz == 1 and d == D_MODEL and p.shape[0] == 1
    t = bsz * s
    h = x.reshape(t, d)
    li = 0
    proj = _in_proj(h, ln1_g[li], w_in[li].astype(BF16))
    h, xn2 = _mixer(proj, h, sinks[li], q_norm_g[li], k_norm_g[li], sg_ln_g[li], sg_ln_b[li], w_spatial[li],
                    b_spatial[li], attn_out_g[li], sg_out_g[li], w_out[li].astype(BF16), ln2_g[li])
    idx_t, gate_t, rank_t, cnt = _router(xn2, w_router[li], b_router[li])
    dest, zmeta, item_e, item_start, item_nsub, moe_meta, n_rows = _plan(idx_t, rank_t, cnt[:, 0], t)
    xs = _dispatch(dest, zmeta, xn2, n_rows)
    ys = _moe(item_e, item_start, item_nsub, moe_meta, xs, w_gate_up[li], b_gate_up[li], w_down[li], b_down[li])
    out = _combine(dest, ys, h, gate_t.T, p[li].reshape(t, D_PLE), w_ple[li].astype(BF16), ple_norm_g[li],
                   w_ple_gate[li].astype(BF16))
    return out.reshape(bsz, s, d)
```

```python
import functools

import jax
import jax.numpy as jnp
import numpy as np
from jax import lax
from jax.experimental import pallas as pl
from jax.experimental.pallas import tpu as pltpu

F32 = jnp.float32
BF16 = jnp.bfloat16
I32 = jnp.int32

D_MODEL = 2048
D_PLE = 256
N_Q_HEADS = 16
N_KV_HEADS = 4
HEAD_DIM = 64
WINDOW = 128
N_SG_HEADS = 8
SG_HEAD_DIM = 128
D_ATTN = N_Q_HEADS * HEAD_DIM
D_KV = N_KV_HEADS * HEAD_DIM
D_SG = N_SG_HEADS * SG_HEAD_DIM
D_MIX = D_ATTN + D_SG
D_IN = D_ATTN + 2 * D_KV + 2 * D_SG
N_EXPERTS = 32
TOP_K = 4
D_FF = D_MODEL
SWIGLU_LIMIT = 7.0
SWIGLU_ALPHA = 1.702
EPS = 1e-6
NEG_INF = -1e30

SUBLANES = 8
LANES = 128
MXU_DIM = 256
VMEM_LIMIT_BYTES = 56 * 1024 * 1024

IN_BM = 512
MIX_R = 512
ROUTE_TB = 512
ROW_TB = 256
MOE_SUB = 256
MOE_BM = 1280
MOE_F = 256
N_GU_SLOTS = 3
ROW_ALIGN = SUBLANES


def _rms(x, g):
    ms = jnp.mean(x * x, axis=-1, keepdims=True)
    return x * lax.rsqrt(ms + EPS) * g


def _gelu(x):
    return 0.5 * x * (1.0 + lax.erf(x * np.float32(np.sqrt(0.5))))


def _params(*sem):
    return pltpu.CompilerParams(dimension_semantics=sem, vmem_limit_bytes=VMEM_LIMIT_BYTES)


def _in_proj_kernel(x_ref, g_ref, w_ref, o_ref):
    o_ref[...] = jnp.dot(_rms(x_ref[...], g_ref[...]).astype(BF16), w_ref[...], preferred_element_type=F32)


def _in_proj(x2, ln1_g, w_in_bf):
    t = x2.shape[0]
    return pl.pallas_call(
        _in_proj_kernel,
        grid=(t // IN_BM,),
        in_specs=[
            pl.BlockSpec((IN_BM, D_MODEL), lambda i: (i, 0)),
            pl.BlockSpec((1, D_MODEL), lambda i: (0, 0)),
            pl.BlockSpec((D_MODEL, D_IN), lambda i: (0, 0), pipeline_mode=pl.Buffered(1)),
        ],
        out_specs=pl.BlockSpec((IN_BM, D_IN), lambda i: (i, 0)),
        out_shape=jax.ShapeDtypeStruct((t, D_IN), F32),
        compiler_params=_params("arbitrary"),
        name="in_proj",
    )(x2, ln1_g.reshape(1, D_MODEL), w_in_bf)


def _head_mean_sq(t, bd):
    tt = t * t
    hi = tt.astype(BF16)
    lo = (tt - hi.astype(F32)).astype(BF16)
    w = bd.shape[0]
    parts = []
    for c in range(t.shape[1] // w):
        cs = slice(c * w, (c + 1) * w)
        parts.append(jnp.dot(hi[:, cs], bd, preferred_element_type=F32)
                     + jnp.dot(lo[:, cs], bd, preferred_element_type=F32))
    return jnp.concatenate(parts, axis=1) * (1.0 / HEAD_DIM)


def _mixer_kernel(sinks_ref, proj_ref, pkv_ref, x_ref, qg_ref, kg_ref, bd_ref, lng_ref, lnb_ref,
                  wsp_ref, bsp_ref, ag_ref, sgg_ref, wout_ref, ln2_ref,
                  h_ref, xn2_ref, a_ref, merged_ref):
    i = pl.program_id(0)
    nsb = MIX_R // WINDOW
    grp = N_Q_HEADS // N_KV_HEADS

    bd = bd_ref[...]
    kv_all = jnp.concatenate([pkv_ref[...], proj_ref[:, D_ATTN:D_ATTN + 2 * D_KV]], axis=0)
    k_all = kv_all[:, :D_KV]
    kn = (k_all * lax.rsqrt(_head_mean_sq(k_all, bd) + EPS) * kg_ref[...]).astype(BF16)
    vb = kv_all[:, D_KV:].astype(BF16)
    q_all = proj_ref[:, :D_ATTN]
    qn = (q_all * lax.rsqrt(_head_mean_sq(q_all, bd) + EPS) * qg_ref[...] * (HEAD_DIM ** -0.5)).astype(BF16)

    rows = lax.broadcasted_iota(I32, (grp * WINDOW, 2 * WINDOW), 0)
    cols = lax.broadcasted_iota(I32, (grp * WINDOW, 2 * WINDOW), 1)
    rel = cols - (rows & (WINDOW - 1))
    band = (rel >= 1) & (rel <= WINDOW)
    cur = cols >= WINDOW
    rblk = lax.broadcasted_iota(I32, (grp * WINDOW, 1), 0) // WINDOW

    tr = lax.broadcasted_iota(I32, (WINDOW, WINDOW), 0)
    tc = lax.broadcasted_iota(I32, (WINDOW, WINDOW), 1)
    wsp = [jnp.where(tc <= tr, wsp_ref[g], 0.0).astype(BF16) for g in range(N_SG_HEADS)]

    for sb in range(nsb):
        r0 = sb * WINDOW
        has_prev = (i * nsb + sb) > 0
        ok = band & (cur | has_prev)
        for g in range(N_KV_HEADS):
            kc = kn[r0:r0 + 2 * WINDOW, g * HEAD_DIM:(g + 1) * HEAD_DIM]
            vc = vb[r0:r0 + 2 * WINDOW, g * HEAD_DIM:(g + 1) * HEAD_DIM]
            qg = jnp.concatenate(
                [qn[r0:r0 + WINDOW, (g * grp + j) * HEAD_DIM:(g * grp + j + 1) * HEAD_DIM] for j in range(grp)],
                axis=0)
            s = lax.dot_general(qg, kc, (((1,), (1,)), ((), ())), preferred_element_type=F32)
            s = jnp.where(ok, s, NEG_INF)
            sink = jnp.full((grp * WINDOW, 1), sinks_ref[g * grp], F32)
            for j in range(1, grp):
                sink = jnp.where(rblk == j, sinks_ref[g * grp + j], sink)
            m = jnp.maximum(jnp.max(s, axis=-1, keepdims=True), sink)
            p = jnp.exp(s - m)
            den = jnp.sum(p, axis=-1, keepdims=True) + jnp.exp(sink - m)
            o = jnp.dot((p / den).astype(BF16), vc, preferred_element_type=F32)
            for j in range(grp):
                hh = g * grp + j
                a_ref[r0:r0 + WINDOW, hh * HEAD_DIM:(hh + 1) * HEAD_DIM] = o[j * WINDOW:(j + 1) * WINDOW]
        merged_ref[r0:r0 + WINDOW, :D_ATTN] = _rms(a_ref[r0:r0 + WINDOW, :], ag_ref[...]).astype(BF16)

        u = _gelu(proj_ref[r0:r0 + WINDOW, D_ATTN + 2 * D_KV:D_ATTN + 2 * D_KV + D_SG])
        vg = _gelu(proj_ref[r0:r0 + WINDOW, D_ATTN + 2 * D_KV + D_SG:])
        mu = jnp.mean(vg, axis=-1, keepdims=True)
        vc0 = vg - mu
        var = jnp.mean(vc0 * vc0, axis=-1, keepdims=True)
        vn = (vc0 * lax.rsqrt(var + EPS) * lng_ref[...] + lnb_ref[...]).astype(BF16)
        mixed = jnp.concatenate(
            [jnp.dot(wsp[g], vn[:, g * SG_HEAD_DIM:(g + 1) * SG_HEAD_DIM], preferred_element_type=F32)
             for g in range(N_SG_HEADS)], axis=1) + bsp_ref[...]
        merged_ref[r0:r0 + WINDOW, D_ATTN:] = _rms(u * mixed, sgg_ref[...]).astype(BF16)

    hblk = x_ref[...] + jnp.dot(merged_ref[...], wout_ref[...], preferred_element_type=F32)
    h_ref[...] = hblk
    xn2_ref[...] = _rms(hblk, ln2_ref[...])


def _mixer(proj, x2, sinks, q_norm_g, k_norm_g, sg_ln_g, sg_ln_b, w_spatial, b_spatial,
           attn_out_g, sg_out_g, w_out_bf, ln2_g):
    t = x2.shape[0]
    nsb = MIX_R // WINDOW
    kvw = 2 * D_KV
    qg = jnp.tile(q_norm_g, N_Q_HEADS).reshape(1, D_ATTN)
    kg = jnp.tile(k_norm_g, N_KV_HEADS).reshape(1, D_KV)
    bd = jnp.asarray(np.kron(np.eye(MXU_DIM // HEAD_DIM), np.ones((HEAD_DIM, HEAD_DIM))), BF16)
    bsp = jnp.repeat(b_spatial.T, SG_HEAD_DIM, axis=1)
    row = lambda v, d: v.reshape(1, d)
    const = lambda shape: pl.BlockSpec(shape, lambda i, s: tuple(0 for _ in shape), pipeline_mode=pl.Buffered(1))
    grid_spec = pltpu.PrefetchScalarGridSpec(
        num_scalar_prefetch=1,
        grid=(t // MIX_R,),
        in_specs=[
            pl.BlockSpec((MIX_R, D_IN), lambda i, s: (i, 0)),
            pl.BlockSpec((WINDOW, kvw), lambda i, s: (jnp.maximum(i * nsb - 1, 0), D_ATTN // kvw)),
            pl.BlockSpec((MIX_R, D_MODEL), lambda i, s: (i, 0)),
            const((1, D_ATTN)), const((1, D_KV)), const((MXU_DIM, MXU_DIM)),
            const((1, D_SG)), const((1, D_SG)),
            const((N_SG_HEADS, WINDOW, WINDOW)), const((WINDOW, D_SG)),
            const((1, D_ATTN)), const((1, D_SG)),
            const((D_MIX, D_MODEL)), const((1, D_MODEL)),
        ],
        out_specs=[
            pl.BlockSpec((MIX_R, D_MODEL), lambda i, s: (i, 0)),
            pl.BlockSpec((MIX_R, D_MODEL), lambda i, s: (i, 0)),
        ],
        scratch_shapes=[pltpu.VMEM((MIX_R, D_ATTN), F32), pltpu.VMEM((MIX_R, D_MIX), BF16)],
    )
    return pl.pallas_call(
        _mixer_kernel,
        grid_spec=grid_spec,
        out_shape=[jax.ShapeDtypeStruct((t, D_MODEL), F32), jax.ShapeDtypeStruct((t, D_MODEL), F32)],
        compiler_params=_params("arbitrary"),
        name="mixer",
    )(sinks, proj, proj, x2, qg, kg, bd, row(sg_ln_g, D_SG), row(sg_ln_b, D_SG), w_spatial, bsp,
      row(attn_out_g, D_ATTN), row(sg_out_g, D_SG), w_out_bf, row(ln2_g, D_MODEL))


def _router_kernel(xn_ref, wr_ref, br_ref, idx_ref, gate_ref, rank_ref, cnt_ref, carry_ref):
    i = pl.program_id(0)
    tb = ROUTE_TB

    @pl.when(i == 0)
    def _():
        carry_ref[...] = jnp.zeros_like(carry_ref)

    def split(v):
        hi = v.astype(BF16)
        return hi, (v - hi.astype(F32)).astype(BF16)

    def nt(a, b):
        return lax.dot_general(a, b, (((1,), (1,)), ((), ())), preferred_element_type=F32)

    w_hi, w_lo = split(wr_ref[...])
    x_hi, x_lo = split(xn_ref[...])
    logits = nt(w_hi, x_hi) + (nt(w_hi, x_lo) + nt(w_lo, x_hi)) + br_ref[...]
    eid = lax.broadcasted_iota(I32, (N_EXPERTS, tb), 0)
    work = logits
    vals, idxs = [], []
    for _ in range(TOP_K):
        m = jnp.max(work, axis=0, keepdims=True)
        sel = jnp.min(jnp.where(work == m, eid, N_EXPERTS), axis=0, keepdims=True)
        vals.append(m)
        idxs.append(sel)
        work = jnp.where(eid == sel, -jnp.inf, work)
    ex = [jnp.exp(v - vals[0]) for v in vals]
    den = ex[0] + ex[1] + ex[2] + ex[3]
    chosen = (eid == idxs[0]) | (eid == idxs[1]) | (eid == idxs[2]) | (eid == idxs[3])
    onehot = jnp.where(chosen, 1.0, 0.0)

    tr = lax.broadcasted_iota(I32, (tb, tb), 0)
    tc = lax.broadcasted_iota(I32, (tb, tb), 1)
    before = jnp.where(tr < tc, 1.0, 0.0).astype(BF16)
    cnt = jnp.dot(onehot.astype(BF16), before, preferred_element_type=F32) + carry_ref[:, :1]
    for k in range(TOP_K):
        idx_ref[k:k + 1, :] = idxs[k]
        gate_ref[k:k + 1, :] = ex[k] / den
        rank_ref[k:k + 1, :] = jnp.sum(jnp.where(eid == idxs[k], cnt, 0.0), axis=0, keepdims=True).astype(I32)
    carry_ref[...] = carry_ref[...] + jnp.sum(onehot, axis=1, keepdims=True)
    cnt_ref[...] = carry_ref[...].astype(I32)


def _router(xn2, w_router, b_router):
    t = xn2.shape[0]
    tb = ROUTE_TB
    return pl.pallas_call(
        _router_kernel,
        grid=(t // tb,),
        in_specs=[
            pl.BlockSpec((tb, D_MODEL), lambda i: (i, 0)),
            pl.BlockSpec((N_EXPERTS, D_MODEL), lambda i: (0, 0)),
            pl.BlockSpec((N_EXPERTS, 1), lambda i: (0, 0)),
        ],
        out_specs=[
            pl.BlockSpec((TOP_K, tb), lambda i: (0, i)),
            pl.BlockSpec((TOP_K, tb), lambda i: (0, i)),
            pl.BlockSpec((TOP_K, tb), lambda i: (0, i)),
            pl.BlockSpec((N_EXPERTS, LANES), lambda i: (0, 0)),
        ],
        out_shape=[
            jax.ShapeDtypeStruct((TOP_K, t), I32),
            jax.ShapeDtypeStruct((TOP_K, t), F32),
            jax.ShapeDtypeStruct((TOP_K, t), I32),
            jax.ShapeDtypeStruct((N_EXPERTS, LANES), I32),
        ],
        scratch_shapes=[pltpu.VMEM((N_EXPERTS, LANES), F32)],
        compiler_params=_params("arbitrary"),
        name="router",
    )(xn2, w_router.T, b_router.reshape(N_EXPERTS, 1))


def _row_copy(src, s, dst, d, sem):
    return pltpu.make_async_copy(src.at[pl.ds(s, 1)], dst.at[pl.ds(d, 1)], sem)


def _tile_copy(src, dst, d8, sem):
    return pltpu.make_async_copy(src, dst.at[pl.ds(pl.multiple_of(d8 * SUBLANES, SUBLANES), SUBLANES)], sem)


def _dispatch_kernel(dest_ref, zmeta_ref, xn_ref, xs_ref, zero_ref, sem, zsem, *, t):
    i = pl.program_id(0)
    tb = ROW_TB

    @pl.when(i == 0)
    def _():
        zero_ref[...] = jnp.zeros_like(zero_ref)

        def pads(start):
            def per_expert(e, c):
                def per_row(r, c2):
                    cp = _row_copy(zero_ref, 0, xs_ref, zmeta_ref[e] + r, zsem)
                    cp.start() if start else cp.wait()
                    return c2
                return lax.fori_loop(0, zmeta_ref[N_EXPERTS + e], per_row, c)
            lax.fori_loop(0, N_EXPERTS, per_expert, 0)

            def per_tile(r, c):
                cp = _tile_copy(zero_ref, xs_ref, zmeta_ref[2 * N_EXPERTS] + r, zsem)
                cp.start() if start else cp.wait()
                return c
            lax.fori_loop(0, zmeta_ref[2 * N_EXPERTS + 1], per_tile, 0)

        pads(True)
        pads(False)

    def rows(start):
        def per_group(g, c):
            for u in range(SUBLANES):
                for k in range(TOP_K):
                    d = dest_ref[k * t + i * tb + g * SUBLANES + u]
                    cp = _row_copy(xn_ref.at[g], u, xs_ref, d, sem)
                    cp.start(priority=k % 2) if start else cp.wait()
            return c
        lax.fori_loop(0, tb // SUBLANES, per_group, 0)
    rows(True)
    rows(False)


def _dispatch(dest_flat, zmeta, xn2, n_rows):
    t = xn2.shape[0]
    grid_spec = pltpu.PrefetchScalarGridSpec(
        num_scalar_prefetch=2,
        grid=(t // ROW_TB,),
        in_specs=[pl.BlockSpec((ROW_TB // SUBLANES, SUBLANES, D_MODEL), lambda i, d, z: (i, 0, 0))],
        out_specs=pl.BlockSpec(memory_space=pl.ANY),
        scratch_shapes=[pltpu.VMEM((SUBLANES, D_MODEL), F32), pltpu.SemaphoreType.DMA, pltpu.SemaphoreType.DMA],
    )
    return pl.pallas_call(
        functools.partial(_dispatch_kernel, t=t),
        grid_spec=grid_spec,
        out_shape=jax.ShapeDtypeStruct((n_rows, D_MODEL), F32),
        compiler_params=_params("arbitrary"),
        name="dispatch",
    )(dest_flat, zmeta, xn2.reshape(t // SUBLANES, SUBLANES, D_MODEL))


def _moe_kernel(ie_ref, ist_ref, ins_ref, meta_ref, xs_ref, wgu_hbm, bgu_ref, wd_hbm, bd_ref, ys_ref,
                rawg_ref, rawd0_ref, rawd1_ref, wdm0_ref, wdm1_ref,
                acc_ref, xbf_ref, xst_ref, ha0_ref, hb0_ref, ha1_ref, hb1_ref,
                zero_ref, gsem, dsem, xsem, osem, zsem, *, n_items, n_chunks):
    h_refs = ((ha0_ref, hb0_ref), (ha1_ref, hb1_ref))
    rawd_ref, wdm_ref = (rawd0_ref, rawd1_ref), (wdm0_ref, wdm1_ref)
    it = pl.program_id(0)
    nsub = ins_ref[it]
    start = ist_ref[it]
    total = meta_ref[2]
    slot_acc = it & 1
    f = MOE_F
    half = f // 2

    def x_copy(item_start, s, slot):
        r = pl.multiple_of(item_start + s * MOE_SUB, SUBLANES)
        return pltpu.make_async_copy(xs_ref.at[pl.ds(r, MOE_SUB)], xst_ref.at[slot], xsem.at[slot])

    def y_copy(item, s):
        r = pl.multiple_of(ist_ref[item] + s * MOE_SUB, SUBLANES)
        return pltpu.make_async_copy(acc_ref.at[item & 1, pl.ds(s * MOE_SUB, MOE_SUB)],
                                     ys_ref.at[pl.ds(r, MOE_SUB)], osem.at[item & 1])

    def y_drain(item):
        def one(s, c):
            y_copy(item, s).wait()
            return c
        lax.fori_loop(0, ins_ref[item], one, 0)

    def gslot(item, j):
        return lax.rem(item * n_chunks + j, N_GU_SLOTS)

    def w_copies(item, j, slot):
        e = ie_ref[item]
        g = gslot(item, j)
        cg = pl.multiple_of(j * 2 * f, 2 * f)
        cd = pl.multiple_of(j * f, f)
        return (pltpu.make_async_copy(wgu_hbm.at[e, :, pl.ds(cg, 2 * f)], rawg_ref.at[g], gsem.at[g]),
                pltpu.make_async_copy(wd_hbm.at[e, pl.ds(cd, f), :], rawd_ref[slot], dsem.at[slot]))

    def interleave_down(slot):
        for c in range(D_MODEL // LANES):
            cs = slice(c * LANES, (c + 1) * LANES)
            wdm_ref[slot][c, pl.ds(0, half, stride=2), :] = rawd_ref[slot][:half, cs]
            wdm_ref[slot][c, pl.ds(1, half, stride=2), :] = rawd_ref[slot][half:, cs]

    @pl.when(it == 0)
    def _():
        for c in range(2):
            for cp in w_copies(0, c, c):
                cp.start()
        for cp in w_copies(0, 0, 0):
            cp.wait()
        interleave_down(0)
        for s in range(2):
            @pl.when(s < nsub)
            def _():
                x_copy(start, s, s).start()

    @pl.when(nsub > 0)
    def _():
        def stage(s, c):
            slot = s & 1
            r = pl.multiple_of(s * MOE_SUB, MOE_SUB)
            x_copy(start, s, slot).wait()
            xbf_ref[pl.ds(r, MOE_SUB), :] = xst_ref[slot].astype(BF16)
            acc_ref[slot_acc, pl.ds(r, MOE_SUB), :] = jnp.broadcast_to(bd_ref[0], (MOE_SUB, D_MODEL))

            @pl.when(s + 2 < nsub)
            def _():
                x_copy(start, s + 2, slot).start()
            return c
        lax.fori_loop(0, nsub, stage, 0)

        nxt = jnp.minimum(it + 1, n_items - 1)
        for s in range(2):
            @pl.when((it + 1 < n_items) & (s < ins_ref[nxt]))
            def _():
                x_copy(ist_ref[nxt], s, s).start()

        even = (lax.broadcasted_iota(I32, (MOE_SUB, f), 1) & 1) == 0

        def up_proj(s, h_ref, j):
            r = pl.multiple_of(s * MOE_SUB, MOE_SUB)
            h_ref[...] = jnp.dot(xbf_ref[pl.ds(r, MOE_SUB), :], rawg_ref[gslot(it, j)].astype(BF16),
                                 preferred_element_type=F32)

        def down_proj(s, h_ref, w, j):
            r = pl.multiple_of(s * MOE_SUB, MOE_SUB)
            h = h_ref[...] + bgu_ref[0, pl.ds(j, 1), :]
            gate = jnp.minimum(h, SWIGLU_LIMIT)
            glu = gate * jax.nn.sigmoid(gate * SWIGLU_ALPHA)
            up = jnp.clip(h, -SWIGLU_LIMIT, SWIGLU_LIMIT) + 1.0
            prod = glu * pltpu.roll(up, 2 * f - 1, axis=1)
            act = jnp.where(even, prod[:, :f], pltpu.roll(prod, 1, axis=1)[:, f:])
            wd = jnp.concatenate([wdm_ref[w][c] for c in range(D_MODEL // LANES)], axis=1).astype(BF16)
            acc_ref[slot_acc, pl.ds(r, MOE_SUB), :] += jnp.dot(act.astype(BF16), wd, preferred_element_type=F32)

        def chunk_pass(j, w, can_be_last):
            j2 = j + 2
            it2 = it + jnp.where(j2 >= n_chunks, 1, 0)

            @pl.when(it2 < total)
            def _():
                for cp in w_copies(jnp.minimum(it2, n_items - 1), j2 & (n_chunks - 1), w):
                    cp.start()

            j1 = j + 1
            it1 = it + jnp.where(j1 >= n_chunks, 1, 0)

            @pl.when(it1 < total)
            def _():
                for cp in w_copies(jnp.minimum(it1, n_items - 1), j1 & (n_chunks - 1), 1 - w):
                    cp.wait()

            interleave_down(1 - w)

            ha_ref, hb_ref = h_refs[w]
            ha_next = h_refs[1 - w][0]
            n_pairs = (nsub - 1) // 2

            def pair(i, c):
                s = 2 * i
                up_proj(s + 1, hb_ref, j)
                down_proj(s, ha_ref, w, j)
                up_proj(s + 2, ha_ref, j)
                down_proj(s + 1, hb_ref, w, j)
                return c
            lax.fori_loop(0, n_pairs, pair, 0)
            s = 2 * n_pairs

            def tail(with_next):
                @pl.when(s == nsub - 1)
                def _():
                    if with_next:
                        up_proj(0, ha_next, j + 1)
                    down_proj(s, ha_ref, w, j)

                @pl.when(s < nsub - 1)
                def _():
                    up_proj(s + 1, hb_ref, j)
                    down_proj(s, ha_ref, w, j)
                    if with_next:
                        up_proj(0, ha_next, j + 1)
                    down_proj(s + 1, hb_ref, w, j)

            if can_be_last:
                @pl.when(j < n_chunks - 1)
                def _():
                    tail(True)

                @pl.when(j == n_chunks - 1)
                def _():
                    tail(False)
            else:
                tail(True)

        up_proj(0, h_refs[0][0], 0)

        def chunk_pair(p, c):
            chunk_pass(2 * p, 0, False)
            chunk_pass(2 * p + 1, 1, True)
            return c
        lax.fori_loop(0, n_chunks // 2, chunk_pair, 0)

    @pl.when(it > 0)
    def _():
        y_drain(it - 1)

    def send(s, c):
        y_copy(it, s).start()
        return c
    lax.fori_loop(0, nsub, send, 0)

    @pl.when(it == n_items - 1)
    def _():
        y_drain(it)

        zero_ref[...] = jnp.zeros_like(zero_ref)

        def fill(start_dma):
            def per_tile(r, c):
                cp = _tile_copy(zero_ref, ys_ref, meta_ref[0] + r, zsem)
                cp.start() if start_dma else cp.wait()
                return c
            lax.fori_loop(0, meta_ref[1], per_tile, 0)
        fill(True)
        fill(False)


def _moe(item_e, item_start, item_nsub, meta, xs, w_gate_up, b_gate_up, w_down, b_down):
    n_items = item_e.shape[0]
    n_rows = xs.shape[0]
    n_chunks = D_FF // MOE_F
    grid_spec = pltpu.PrefetchScalarGridSpec(
        num_scalar_prefetch=4,
        grid=(n_items,),
        in_specs=[
            pl.BlockSpec(memory_space=pl.ANY),
            pl.BlockSpec(memory_space=pl.ANY),
            pl.BlockSpec((1, n_chunks, 2 * MOE_F), lambda it, ie, ist, ins, mt: (ie[it], 0, 0)),
            pl.BlockSpec(memory_space=pl.ANY),
            pl.BlockSpec((1, 1, D_MODEL), lambda it, ie, ist, ins, mt: (ie[it], 0, 0)),
        ],
        out_specs=pl.BlockSpec(memory_space=pl.ANY),
        scratch_shapes=[
            pltpu.VMEM((N_GU_SLOTS, D_MODEL, 2 * MOE_F), F32),
            pltpu.VMEM((MOE_F, D_MODEL), F32), pltpu.VMEM((MOE_F, D_MODEL), F32),
            pltpu.VMEM((D_MODEL // LANES, MOE_F, LANES), F32), pltpu.VMEM((D_MODEL // LANES, MOE_F, LANES), F32),
            pltpu.VMEM((2, MOE_BM, D_MODEL), F32),
            pltpu.VMEM((MOE_BM, D_MODEL), BF16),
            pltpu.VMEM((2, MOE_SUB, D_MODEL), F32),
            pltpu.VMEM((MOE_SUB, 2 * MOE_F), F32), pltpu.VMEM((MOE_SUB, 2 * MOE_F), F32),
            pltpu.VMEM((MOE_SUB, 2 * MOE_F), F32), pltpu.VMEM((MOE_SUB, 2 * MOE_F), F32),
            pltpu.VMEM((SUBLANES, D_MODEL), F32),
            pltpu.SemaphoreType.DMA((N_GU_SLOTS,)),
            pltpu.SemaphoreType.DMA((2,)),
            pltpu.SemaphoreType.DMA((2,)),
            pltpu.SemaphoreType.DMA((2,)),
            pltpu.SemaphoreType.DMA,
        ],
    )
    return pl.pallas_call(
        functools.partial(_moe_kernel, n_items=n_items, n_chunks=n_chunks),
        grid_spec=grid_spec,
        out_shape=jax.ShapeDtypeStruct((n_rows, D_MODEL), F32),
        compiler_params=_params("arbitrary"),
        name="moe",
    )(item_e, item_start, item_nsub, meta, xs, w_gate_up, b_gate_up.reshape(N_EXPERTS, n_chunks, 2 * MOE_F),
      w_down, b_down.reshape(N_EXPERTS, 1, D_MODEL))


def _combine_kernel(dest_ref, ys_ref, h_ref, gate_ref, p_ref, wple_ref, pg_ref, wpg_ref, o_ref, ybuf_ref, sem, *, t):
    i = pl.program_id(0)
    n = pl.num_programs(0)
    tb = ROW_TB
    slot = i & 1

    def gather(blk, sl, start):
        def per_group(g, c):
            for u in range(SUBLANES):
                for k in range(TOP_K):
                    d = dest_ref[k * t + blk * tb + g * SUBLANES + u]
                    cp = _row_copy(ys_ref, d, ybuf_ref.at[sl, k, g], u, sem.at[sl])
                    cp.start() if start else cp.wait()
            return c
        lax.fori_loop(0, tb // SUBLANES, per_group, 0)

    def on_slot(cond, blk, rel, start):
        for sl in range(2):
            @pl.when(cond & (slot == (sl ^ rel)))
            def _():
                gather(blk, sl, start)

    on_slot(i == 0, 0, 0, True)
    on_slot(i + 1 < n, i + 1, 1, True)
    on_slot(i >= 0, i, 0, False)

    gates = gate_ref[...]
    h2 = h_ref[...]
    for k in range(TOP_K):
        h2 = h2 + ybuf_ref[slot, k].reshape(tb, D_MODEL) * gates[:, k:k + 1]
    gate = jax.nn.sigmoid(jnp.dot(_rms(h2, pg_ref[...]).astype(BF16), wpg_ref[...], preferred_element_type=F32))
    o_ref[...] = h2 + jnp.dot(p_ref[...].astype(BF16), wple_ref[...], preferred_element_type=F32) * gate


def _combine(dest_flat, ys, h, gates_tk, p2, w_ple_bf, ple_norm_g, w_pg_bf):
    t = h.shape[0]
    tb = ROW_TB
    grid_spec = pltpu.PrefetchScalarGridSpec(
        num_scalar_prefetch=1,
        grid=(t // tb,),
        in_specs=[
            pl.BlockSpec(memory_space=pl.ANY),
            pl.BlockSpec((tb, D_MODEL), lambda i, d: (i, 0)),
            pl.BlockSpec((tb, TOP_K), lambda i, d: (i, 0)),
            pl.BlockSpec((tb, D_PLE), lambda i, d: (i, 0)),
            pl.BlockSpec((D_PLE, D_MODEL), lambda i, d: (0, 0)),
            pl.BlockSpec((1, D_MODEL), lambda i, d: (0, 0)),
            pl.BlockSpec((D_MODEL, D_MODEL), lambda i, d: (0, 0)),
        ],
        out_specs=pl.BlockSpec((tb, D_MODEL), lambda i, d: (i, 0)),
        scratch_shapes=[pltpu.VMEM((2, TOP_K, tb // SUBLANES, SUBLANES, D_MODEL), F32),
                        pltpu.SemaphoreType.DMA((2,))],
    )
    return pl.pallas_call(
        functools.partial(_combine_kernel, t=t),
        grid_spec=grid_spec,
        out_shape=jax.ShapeDtypeStruct((t, D_MODEL), F32),
        compiler_params=_params("arbitrary"),
        name="combine",
    )(dest_flat, ys, h, gates_tk, p2, w_ple_bf, ple_norm_g.reshape(1, D_MODEL), w_pg_bf)


def _plan(idx_t, rank_t, counts, t):
    tk = t * TOP_K
    n_items = tk // MOE_BM + N_EXPERTS
    n_rows = tk + N_EXPERTS * ROW_ALIGN + MOE_SUB
    seg = (counts + ROW_ALIGN - 1) // ROW_ALIGN * ROW_ALIGN
    seg_end = jnp.cumsum(seg)
    seg_start = seg_end - seg
    eids = jnp.arange(N_EXPERTS, dtype=I32)[:, None, None]
    first = jnp.sum(jnp.where(idx_t[None] == eids, seg_start[:, None, None], 0), axis=0)
    dest = (first + rank_t).reshape(-1).astype(I32)

    per_e = (counts + MOE_BM - 1) // MOE_BM
    it_end = jnp.cumsum(per_e)
    it_off = it_end - per_e
    total = it_end[-1]
    its = jnp.arange(n_items, dtype=I32)
    valid = its < total
    its_c = jnp.minimum(its, total - 1)
    e_of = jnp.minimum(jnp.searchsorted(it_end, its_c, side="right"), N_EXPERTS - 1).astype(I32)
    li = its_c - it_off[e_of]
    rows = jnp.clip(counts[e_of] - li * MOE_BM, 0, MOE_BM)
    nsub = jnp.where(valid, (rows + MOE_SUB - 1) // MOE_SUB, 0).astype(I32)
    start = (seg_start[e_of] + li * MOE_BM).astype(I32)

    used_end = start[total - 1] + MOE_SUB * ((rows[total - 1] + MOE_SUB - 1) // MOE_SUB)
    moe_meta = jnp.stack([used_end // SUBLANES, (n_rows - used_end) // SUBLANES, total]).astype(I32)
    zmeta = jnp.concatenate([
        seg_start + counts, seg - counts,
        jnp.stack([seg_end[-1] // SUBLANES, (n_rows - seg_end[-1]) // SUBLANES]),
    ]).astype(I32)
    return dest, zmeta, e_of, start, nsub, moe_meta, n_rows


def kernel(x, p, ln1_g, w_in, q_norm_g, k_norm_g, sinks, sg_ln_g, sg_ln_b, w_spatial, b_spatial, attn_out_g,
           sg_out_g, w_out, ln2_g, w_router, b_router, w_gate_up, b_gate_up, w_down, b_down, w_ple, ple_norm_g,
           w_ple_gate):
    bsz, s, d = x.shape
    assert bsz == 1 and d == D_MODEL and p.shape[0] == 1
    t = bsz * s
    h = x.reshape(t, d)
    li = 0
    proj = _in_proj(h, ln1_g[li], w_in[li].astype(BF16))
    h, xn2 = _mixer(proj, h, sinks[li], q_norm_g[li], k_norm_g[li], sg_ln_g[li], sg_ln_b[li], w_spatial[li],
                    b_spatial[li], attn_out_g[li], sg_out_g[li], w_out[li].astype(BF16), ln2_g[li])
    idx_t, gate_t, rank_t, cnt = _router(xn2, w_router[li], b_router[li])
    dest, zmeta, item_e, item_start, item_nsub, moe_meta, n_rows = _plan(idx_t, rank_t, cnt[:, 0], t)
    xs = _dispatch(dest, zmeta, xn2, n_rows)
    ys = _moe(item_e, item_start, item_nsub, moe_meta, xs, w_gate_up[li], b_gate_up[li], w_down[li], b_down[li])
    out = _combine(dest, ys, h, gate_t.T, p[li].reshape(t, D_PLE), w_ple[li].astype(BF16), ple_norm_g[li],
                   w_ple_gate[li].astype(BF16))
    return out.reshape(bsz, s, d)
```

```python
import functools

import jax
import jax.numpy as jnp
import numpy as np
from jax import lax
from jax.experimental import pallas as pl
from jax.experimental.pallas import tpu as pltpu

F32 = jnp.float32
BF16 = jnp.bfloat16
I32 = jnp.int32

D_MODEL = 2048
D_PLE = 256
N_Q_HEADS = 16
N_KV_HEADS = 4
HEAD_DIM = 64
WINDOW = 128
N_SG_HEADS = 8
SG_HEAD_DIM = 128
D_ATTN = N_Q_HEADS * HEAD_DIM
D_KV = N_KV_HEADS * HEAD_DIM
D_SG = N_SG_HEADS * SG_HEAD_DIM
D_MIX = D_ATTN + D_SG
D_IN = D_ATTN + 2 * D_KV + 2 * D_SG
N_EXPERTS = 32
TOP_K = 4
D_FF = D_MODEL
SWIGLU_LIMIT = 7.0
SWIGLU_ALPHA = 1.702
EPS = 1e-6
NEG_INF = -1e30

SUBLANES = 8
LANES = 128
MXU_DIM = 256
VMEM_LIMIT_BYTES = 56 * 1024 * 1024

IN_BM = 512
MIX_R = 512
ROUTE_TB = 512
ROW_TB = 256
MOE_SUB = 256
MOE_SUBS = (256, 224)
ROW_PACK = 16
MOE_BM = 1280
MOE_F = 256
N_GU_SLOTS = 3
ROW_ALIGN = SUBLANES


def _rms(x, g):
    ms = jnp.mean(x * x, axis=-1, keepdims=True)
    return x * lax.rsqrt(ms + EPS) * g


def _gelu(x):
    return 0.5 * x * (1.0 + lax.erf(x * np.float32(np.sqrt(0.5))))


def _params(*sem):
    return pltpu.CompilerParams(dimension_semantics=sem, vmem_limit_bytes=VMEM_LIMIT_BYTES)


def _in_proj_kernel(x_ref, g_ref, w_ref, o_ref):
    o_ref[...] = jnp.dot(_rms(x_ref[...], g_ref[...]).astype(BF16), w_ref[...], preferred_element_type=F32)


def _in_proj(x2, ln1_g, w_in_bf):
    t = x2.shape[0]
    return pl.pallas_call(
        _in_proj_kernel,
        grid=(t // IN_BM,),
        in_specs=[
            pl.BlockSpec((IN_BM, D_MODEL), lambda i: (i, 0)),
            pl.BlockSpec((1, D_MODEL), lambda i: (0, 0)),
            pl.BlockSpec((D_MODEL, D_IN), lambda i: (0, 0), pipeline_mode=pl.Buffered(1)),
        ],
        out_specs=pl.BlockSpec((IN_BM, D_IN), lambda i: (i, 0)),
        out_shape=jax.ShapeDtypeStruct((t, D_IN), F32),
        compiler_params=_params("arbitrary"),
        name="in_proj",
    )(x2, ln1_g.reshape(1, D_MODEL), w_in_bf)


def _head_mean_sq(t, bd):
    tt = t * t
    hi = tt.astype(BF16)
    lo = (tt - hi.astype(F32)).astype(BF16)
    w = bd.shape[0]
    parts = []
    for c in range(t.shape[1] // w):
        cs = slice(c * w, (c + 1) * w)
        parts.append(jnp.dot(hi[:, cs], bd, preferred_element_type=F32)
                     + jnp.dot(lo[:, cs], bd, preferred_element_type=F32))
    return jnp.concatenate(parts, axis=1) * (1.0 / HEAD_DIM)


def _mixer_kernel(sinks_ref, proj_ref, pkv_ref, x_ref, qg_ref, kg_ref, bd_ref, lng_ref, lnb_ref,
                  wsp_ref, bsp_ref, ag_ref, sgg_ref, wout_ref, ln2_ref,
                  h_ref, xn2_ref, a_ref, merged_ref):
    i = pl.program_id(0)
    nsb = MIX_R // WINDOW
    grp = N_Q_HEADS // N_KV_HEADS

    bd = bd_ref[...]
    kv_all = jnp.concatenate([pkv_ref[...], proj_ref[:, D_ATTN:D_ATTN + 2 * D_KV]], axis=0)
    k_all = kv_all[:, :D_KV]
    kn = (k_all * lax.rsqrt(_head_mean_sq(k_all, bd) + EPS) * kg_ref[...]).astype(BF16)
    vb = kv_all[:, D_KV:].astype(BF16)
    q_all = proj_ref[:, :D_ATTN]
    qn = (q_all * lax.rsqrt(_head_mean_sq(q_all, bd) + EPS) * qg_ref[...] * (HEAD_DIM ** -0.5)).astype(BF16)

    rows = lax.broadcasted_iota(I32, (grp * WINDOW, 2 * WINDOW), 0)
    cols = lax.broadcasted_iota(I32, (grp * WINDOW, 2 * WINDOW), 1)
    rel = cols - (rows & (WINDOW - 1))
    band = (rel >= 1) & (rel <= WINDOW)
    cur = cols >= WINDOW
    rblk = lax.broadcasted_iota(I32, (grp * WINDOW, 1), 0) // WINDOW

    tr = lax.broadcasted_iota(I32, (WINDOW, WINDOW), 0)
    tc = lax.broadcasted_iota(I32, (WINDOW, WINDOW), 1)
    wsp = [jnp.where(tc <= tr, wsp_ref[g], 0.0).astype(BF16) for g in range(N_SG_HEADS)]

    for sb in range(nsb):
        r0 = sb * WINDOW
        has_prev = (i * nsb + sb) > 0
        ok = band & (cur | has_prev)
        for g in range(N_KV_HEADS):
            kc = kn[r0:r0 + 2 * WINDOW, g * HEAD_DIM:(g + 1) * HEAD_DIM]
            vc = vb[r0:r0 + 2 * WINDOW, g * HEAD_DIM:(g + 1) * HEAD_DIM]
            qg = jnp.concatenate(
                [qn[r0:r0 + WINDOW, (g * grp + j) * HEAD_DIM:(g * grp + j + 1) * HEAD_DIM] for j in range(grp)],
                axis=0)
            s = lax.dot_general(qg, kc, (((1,), (1,)), ((), ())), preferred_element_type=F32)
            s = jnp.where(ok, s, NEG_INF)
            sink = jnp.full((grp * WINDOW, 1), sinks_ref[g * grp], F32)
            for j in range(1, grp):
                sink = jnp.where(rblk == j, sinks_ref[g * grp + j], sink)
            m = jnp.maximum(jnp.max(s, axis=-1, keepdims=True), sink)
            p = jnp.exp(s - m)
            den = jnp.sum(p, axis=-1, keepdims=True) + jnp.exp(sink - m)
            o = jnp.dot((p / den).astype(BF16), vc, preferred_element_type=F32)
            for j in range(grp):
                hh = g * grp + j
                a_ref[r0:r0 + WINDOW, hh * HEAD_DIM:(hh + 1) * HEAD_DIM] = o[j * WINDOW:(j + 1) * WINDOW]
        merged_ref[r0:r0 + WINDOW, :D_ATTN] = _rms(a_ref[r0:r0 + WINDOW, :], ag_ref[...]).astype(BF16)

        u = _gelu(proj_ref[r0:r0 + WINDOW, D_ATTN + 2 * D_KV:D_ATTN + 2 * D_KV + D_SG])
        vg = _gelu(proj_ref[r0:r0 + WINDOW, D_ATTN + 2 * D_KV + D_SG:])
        mu = jnp.mean(vg, axis=-1, keepdims=True)
        vc0 = vg - mu
        var = jnp.mean(vc0 * vc0, axis=-1, keepdims=True)
        vn = (vc0 * lax.rsqrt(var + EPS) * lng_ref[...] + lnb_ref[...]).astype(BF16)
        mixed = jnp.concatenate(
            [jnp.dot(wsp[g], vn[:, g * SG_HEAD_DIM:(g + 1) * SG_HEAD_DIM], preferred_element_type=F32)
             for g in range(N_SG_HEADS)], axis=1) + bsp_ref[...]
        merged_ref[r0:r0 + WINDOW, D_ATTN:] = _rms(u * mixed, sgg_ref[...]).astype(BF16)

    hblk = x_ref[...] + jnp.dot(merged_ref[...], wout_ref[...], preferred_element_type=F32)
    h_ref[...] = hblk
    xn2_ref[...] = _rms(hblk, ln2_ref[...])


def _mixer(proj, x2, sinks, q_norm_g, k_norm_g, sg_ln_g, sg_ln_b, w_spatial, b_spatial,
           attn_out_g, sg_out_g, w_out_bf, ln2_g):
    t = x2.shape[0]
    nsb = MIX_R // WINDOW
    kvw = 2 * D_KV
    qg = jnp.tile(q_norm_g, N_Q_HEADS).reshape(1, D_ATTN)
    kg = jnp.tile(k_norm_g, N_KV_HEADS).reshape(1, D_KV)
    bd = jnp.asarray(np.kron(np.eye(MXU_DIM // HEAD_DIM), np.ones((HEAD_DIM, HEAD_DIM))), BF16)
    bsp = jnp.repeat(b_spatial.T, SG_HEAD_DIM, axis=1)
    row = lambda v, d: v.reshape(1, d)
    const = lambda shape: pl.BlockSpec(shape, lambda i, s: tuple(0 for _ in shape), pipeline_mode=pl.Buffered(1))
    grid_spec = pltpu.PrefetchScalarGridSpec(
        num_scalar_prefetch=1,
        grid=(t // MIX_R,),
        in_specs=[
            pl.BlockSpec((MIX_R, D_IN), lambda i, s: (i, 0)),
            pl.BlockSpec((WINDOW, kvw), lambda i, s: (jnp.maximum(i * nsb - 1, 0), D_ATTN // kvw)),
            pl.BlockSpec((MIX_R, D_MODEL), lambda i, s: (i, 0)),
            const((1, D_ATTN)), const((1, D_KV)), const((MXU_DIM, MXU_DIM)),
            const((1, D_SG)), const((1, D_SG)),
            const((N_SG_HEADS, WINDOW, WINDOW)), const((WINDOW, D_SG)),
            const((1, D_ATTN)), const((1, D_SG)),
            const((D_MIX, D_MODEL)), const((1, D_MODEL)),
        ],
        out_specs=[
            pl.BlockSpec((MIX_R, D_MODEL), lambda i, s: (i, 0)),
            pl.BlockSpec((MIX_R, D_MODEL), lambda i, s: (i, 0)),
        ],
        scratch_shapes=[pltpu.VMEM((MIX_R, D_ATTN), F32), pltpu.VMEM((MIX_R, D_MIX), BF16)],
    )
    return pl.pallas_call(
        _mixer_kernel,
        grid_spec=grid_spec,
        out_shape=[jax.ShapeDtypeStruct((t, D_MODEL), F32), jax.ShapeDtypeStruct((t, D_MODEL), F32)],
        compiler_params=_params("arbitrary"),
        name="mixer",
    )(sinks, proj, proj, x2, qg, kg, bd, row(sg_ln_g, D_SG), row(sg_ln_b, D_SG), w_spatial, bsp,
      row(attn_out_g, D_ATTN), row(sg_out_g, D_SG), w_out_bf, row(ln2_g, D_MODEL))


def _router_kernel(xn_ref, wr_ref, br_ref, idx_ref, gate_ref, rank_ref, cnt_ref, carry_ref):
    i = pl.program_id(0)
    tb = ROUTE_TB

    @pl.when(i == 0)
    def _():
        carry_ref[...] = jnp.zeros_like(carry_ref)

    def split(v):
        hi = v.astype(BF16)
        return hi, (v - hi.astype(F32)).astype(BF16)

    def nt(a, b):
        return lax.dot_general(a, b, (((1,), (1,)), ((), ())), preferred_element_type=F32)

    w_hi, w_lo = split(wr_ref[...])
    x_hi, x_lo = split(xn_ref[...])
    logits = nt(w_hi, x_hi) + (nt(w_hi, x_lo) + nt(w_lo, x_hi)) + br_ref[...]
    eid = lax.broadcasted_iota(I32, (N_EXPERTS, tb), 0)
    work = logits
    vals, idxs = [], []
    for _ in range(TOP_K):
        m = jnp.max(work, axis=0, keepdims=True)
        sel = jnp.min(jnp.where(work == m, eid, N_EXPERTS), axis=0, keepdims=True)
        vals.append(m)
        idxs.append(sel)
        work = jnp.where(eid == sel, -jnp.inf, work)
    ex = [jnp.exp(v - vals[0]) for v in vals]
    den = ex[0] + ex[1] + ex[2] + ex[3]
    chosen = (eid == idxs[0]) | (eid == idxs[1]) | (eid == idxs[2]) | (eid == idxs[3])
    onehot = jnp.where(chosen, 1.0, 0.0)

    tr = lax.broadcasted_iota(I32, (tb, tb), 0)
    tc = lax.broadcasted_iota(I32, (tb, tb), 1)
    before = jnp.where(tr < tc, 1.0, 0.0).astype(BF16)
    cnt = jnp.dot(onehot.astype(BF16), before, preferred_element_type=F32) + carry_ref[:, :1]
    for k in range(TOP_K):
        idx_ref[k:k + 1, :] = idxs[k]
        gate_ref[k:k + 1, :] = ex[k] / den
        rank_ref[k:k + 1, :] = jnp.sum(jnp.where(eid == idxs[k], cnt, 0.0), axis=0, keepdims=True).astype(I32)
    carry_ref[...] = carry_ref[...] + jnp.sum(onehot, axis=1, keepdims=True)
    cnt_ref[...] = carry_ref[...].astype(I32)


def _router(xn2, w_router, b_router):
    t = xn2.shape[0]
    tb = ROUTE_TB
    return pl.pallas_call(
        _router_kernel,
        grid=(t // tb,),
        in_specs=[
            pl.BlockSpec((tb, D_MODEL), lambda i: (i, 0)),
            pl.BlockSpec((N_EXPERTS, D_MODEL), lambda i: (0, 0)),
            pl.BlockSpec((N_EXPERTS, 1), lambda i: (0, 0)),
        ],
        out_specs=[
            pl.BlockSpec((TOP_K, tb), lambda i: (0, i)),
            pl.BlockSpec((TOP_K, tb), lambda i: (0, i)),
            pl.BlockSpec((TOP_K, tb), lambda i: (0, i)),
            pl.BlockSpec((N_EXPERTS, LANES), lambda i: (0, 0)),
        ],
        out_shape=[
            jax.ShapeDtypeStruct((TOP_K, t), I32),
            jax.ShapeDtypeStruct((TOP_K, t), F32),
            jax.ShapeDtypeStruct((TOP_K, t), I32),
            jax.ShapeDtypeStruct((N_EXPERTS, LANES), I32),
        ],
        scratch_shapes=[pltpu.VMEM((N_EXPERTS, LANES), F32)],
        compiler_params=_params("arbitrary"),
        name="router",
    )(xn2, w_router.T, b_router.reshape(N_EXPERTS, 1))


def _row_copy(src, s, dst, d, sem):
    return pltpu.make_async_copy(src.at[pl.ds(s, 1)], dst.at[pl.ds(d, 1)], sem)


def _tile_copy(src, dst, d8, sem):
    return pltpu.make_async_copy(src, dst.at[pl.ds(pl.multiple_of(d8 * SUBLANES, SUBLANES), SUBLANES)], sem)


def _dispatch_kernel(dest_ref, zmeta_ref, xn_ref, xs_ref, zero_ref, sem, zsem, *, t):
    i = pl.program_id(0)
    tb = ROW_TB

    @pl.when(i == 0)
    def _():
        zero_ref[...] = jnp.zeros_like(zero_ref)

        def pads(start):
            def per_expert(e, c):
                def per_row(r, c2):
                    cp = _row_copy(zero_ref, 0, xs_ref, zmeta_ref[e] + r, zsem)
                    cp.start() if start else cp.wait()
                    return c2
                return lax.fori_loop(0, zmeta_ref[N_EXPERTS + e], per_row, c)
            lax.fori_loop(0, N_EXPERTS, per_expert, 0)

            def per_tile(r, c):
                cp = _tile_copy(zero_ref, xs_ref, zmeta_ref[2 * N_EXPERTS] + r, zsem)
                cp.start() if start else cp.wait()
                return c
            lax.fori_loop(0, zmeta_ref[2 * N_EXPERTS + 1], per_tile, 0)

        pads(True)
        pads(False)

    def rows(start):
        def per_group(g, c):
            for u in range(SUBLANES):
                for k in range(TOP_K):
                    d = dest_ref[k * t + i * tb + g * SUBLANES + u]
                    cp = _row_copy(xn_ref.at[g], u, xs_ref, d, sem)
                    cp.start(priority=k % 2) if start else cp.wait()
            return c
        lax.fori_loop(0, tb // SUBLANES, per_group, 0)
    rows(True)
    rows(False)


def _dispatch(dest_flat, zmeta, xn2, n_rows):
    t = xn2.shape[0]
    grid_spec = pltpu.PrefetchScalarGridSpec(
        num_scalar_prefetch=2,
        grid=(t // ROW_TB,),
        in_specs=[pl.BlockSpec((ROW_TB // SUBLANES, SUBLANES, D_MODEL), lambda i, d, z: (i, 0, 0))],
        out_specs=pl.BlockSpec(memory_space=pl.ANY),
        scratch_shapes=[pltpu.VMEM((SUBLANES, D_MODEL), F32), pltpu.SemaphoreType.DMA, pltpu.SemaphoreType.DMA],
    )
    return pl.pallas_call(
        functools.partial(_dispatch_kernel, t=t),
        grid_spec=grid_spec,
        out_shape=jax.ShapeDtypeStruct((n_rows, D_MODEL), F32),
        compiler_params=_params("arbitrary"),
        name="dispatch",
    )(dest_flat, zmeta, xn2.reshape(t // SUBLANES, SUBLANES, D_MODEL))


def _moe_kernel(ie_ref, ist_ref, ins_ref, isz_ref, meta_ref, xs_ref, wgu_hbm, bgu_ref, wd_hbm, bd_ref, ys_ref,
                rawg_ref, rawd0_ref, rawd1_ref, wdm0_ref, wdm1_ref,
                acc_ref, xbf_ref, xst_ref, ha0_ref, hb0_ref, ha1_ref, hb1_ref,
                zero_ref, gsem, dsem, xsem, osem, zsem, *, n_items, n_chunks):
    h_refs = ((ha0_ref, hb0_ref), (ha1_ref, hb1_ref))
    rawd_ref, wdm_ref = (rawd0_ref, rawd1_ref), (wdm0_ref, wdm1_ref)
    it = pl.program_id(0)
    nsub = ins_ref[it]
    start = ist_ref[it]
    total = meta_ref[2]
    slot_acc = it & 1
    f = MOE_F
    half = f // 2

    def by_size(item, fn):
        for v, sub in enumerate(MOE_SUBS):
            @pl.when(isz_ref[item] == v)
            def _():
                fn(sub)

    def x_copy(item_start, s, slot, sub):
        r = pl.multiple_of(item_start + s * sub, SUBLANES)
        return pltpu.make_async_copy(xs_ref.at[pl.ds(r, sub)], xst_ref.at[slot, pl.ds(0, sub)], xsem.at[slot])

    def x_prefetch(item, sub):
        for s in range(2):
            @pl.when(s < ins_ref[item])
            def _():
                x_copy(ist_ref[item], s, s, sub).start()

    def y_copies(item, start_dma):
        def sized(sub):
            def one(s, c):
                r = pl.multiple_of(ist_ref[item] + s * sub, SUBLANES)
                cp = pltpu.make_async_copy(acc_ref.at[item & 1, pl.ds(s * sub, sub)],
                                           ys_ref.at[pl.ds(r, sub)], osem.at[item & 1])
                cp.start() if start_dma else cp.wait()
                return c
            lax.fori_loop(0, ins_ref[item], one, 0)
        by_size(item, sized)

    def gslot(item, j):
        return lax.rem(item * n_chunks + j, N_GU_SLOTS)

    def w_copies(item, j, slot):
        e = ie_ref[item]
        g = gslot(item, j)
        cg = pl.multiple_of(j * 2 * f, 2 * f)
        cd = pl.multiple_of(j * f, f)
        return (pltpu.make_async_copy(wgu_hbm.at[e, :, pl.ds(cg, 2 * f)], rawg_ref.at[g], gsem.at[g]),
                pltpu.make_async_copy(wd_hbm.at[e, pl.ds(cd, f), :], rawd_ref[slot], dsem.at[slot]))

    def interleave_down(slot):
        for c in range(D_MODEL // LANES):
            cs = slice(c * LANES, (c + 1) * LANES)
            wdm_ref[slot][c, pl.ds(0, half, stride=2), :] = rawd_ref[slot][:half, cs]
            wdm_ref[slot][c, pl.ds(1, half, stride=2), :] = rawd_ref[slot][half:, cs]

    @pl.when(it == 0)
    def _():
        for c in range(2):
            for cp in w_copies(0, c, c):
                cp.start()
        for cp in w_copies(0, 0, 0):
            cp.wait()
        interleave_down(0)
        by_size(0, lambda sub: x_prefetch(0, sub))

    def item_body(sub):
        def stage(s, c):
            slot = s & 1
            r = pl.multiple_of(s * sub, ROW_PACK)
            x_copy(start, s, slot, sub).wait()
            xbf_ref[pl.ds(r, sub), :] = xst_ref[slot, :sub, :].astype(BF16)
            acc_ref[slot_acc, pl.ds(r, sub), :] = jnp.broadcast_to(bd_ref[0], (sub, D_MODEL))

            @pl.when(s + 2 < nsub)
            def _():
                x_copy(start, s + 2, slot, sub).start()
            return c
        lax.fori_loop(0, nsub, stage, 0)

        nxt = jnp.minimum(it + 1, n_items - 1)

        @pl.when(it + 1 < n_items)
        def _():
            by_size(nxt, lambda nsz: x_prefetch(nxt, nsz))

        even = (lax.broadcasted_iota(I32, (sub, f), 1) & 1) == 0

        def up_proj(s, h_ref, j):
            r = pl.multiple_of(s * sub, ROW_PACK)
            h_ref[:sub, :] = jnp.dot(xbf_ref[pl.ds(r, sub), :], rawg_ref[gslot(it, j)].astype(BF16),
                                     preferred_element_type=F32)

        def down_proj(s, h_ref, w, j):
            r = pl.multiple_of(s * sub, ROW_PACK)
            h = h_ref[:sub, :] + bgu_ref[0, pl.ds(j, 1), :]
            gate = jnp.minimum(h, SWIGLU_LIMIT)
            glu = gate * jax.nn.sigmoid(gate * SWIGLU_ALPHA)
            up = jnp.clip(h, -SWIGLU_LIMIT, SWIGLU_LIMIT) + 1.0
            prod = glu * pltpu.roll(up, 2 * f - 1, axis=1)
            act = jnp.where(even, prod[:, :f], pltpu.roll(prod, 1, axis=1)[:, f:])
            wd = jnp.concatenate([wdm_ref[w][c] for c in range(D_MODEL // LANES)], axis=1).astype(BF16)
            acc_ref[slot_acc, pl.ds(r, sub), :] += jnp.dot(act.astype(BF16), wd, preferred_element_type=F32)

        def chunk_pass(j, w, can_be_last):
            j2 = j + 2
            it2 = it + jnp.where(j2 >= n_chunks, 1, 0)

            @pl.when(it2 < total)
            def _():
                for cp in w_copies(jnp.minimum(it2, n_items - 1), j2 & (n_chunks - 1), w):
                    cp.start()

            j1 = j + 1
            it1 = it + jnp.where(j1 >= n_chunks, 1, 0)

            @pl.when(it1 < total)
            def _():
                for cp in w_copies(jnp.minimum(it1, n_items - 1), j1 & (n_chunks - 1), 1 - w):
                    cp.wait()

            interleave_down(1 - w)

            ha_ref, hb_ref = h_refs[w]
            ha_next = h_refs[1 - w][0]
            n_pairs = (nsub - 1) // 2

            def pair(i, c):
                s = 2 * i
                up_proj(s + 1, hb_ref, j)
                down_proj(s, ha_ref, w, j)
                up_proj(s + 2, ha_ref, j)
                down_proj(s + 1, hb_ref, w, j)
                return c
            lax.fori_loop(0, n_pairs, pair, 0)
            s = 2 * n_pairs

            def tail(with_next):
                @pl.when(s == nsub - 1)
                def _():
                    if with_next:
                        up_proj(0, ha_next, j + 1)
                    down_proj(s, ha_ref, w, j)

                @pl.when(s < nsub - 1)
                def _():
                    up_proj(s + 1, hb_ref, j)
                    down_proj(s, ha_ref, w, j)
                    if with_next:
                        up_proj(0, ha_next, j + 1)
                    down_proj(s + 1, hb_ref, w, j)

            if can_be_last:
                @pl.when(j < n_chunks - 1)
                def _():
                    tail(True)

                @pl.when(j == n_chunks - 1)
                def _():
                    tail(False)
            else:
                tail(True)

        up_proj(0, h_refs[0][0], 0)

        def chunk_pair(p, c):
            chunk_pass(2 * p, 0, False)
            chunk_pass(2 * p + 1, 1, True)
            return c
        lax.fori_loop(0, n_chunks // 2, chunk_pair, 0)

    @pl.when(nsub > 0)
    def _():
        by_size(it, item_body)

    @pl.when(it > 0)
    def _():
        y_copies(it - 1, False)

    y_copies(it, True)

    @pl.when(it == n_items - 1)
    def _():
        y_copies(it, False)

        zero_ref[...] = jnp.zeros_like(zero_ref)

        def fill(start_dma):
            def per_tile(r, c):
                cp = _tile_copy(zero_ref, ys_ref, meta_ref[0] + r, zsem)
                cp.start() if start_dma else cp.wait()
                return c
            lax.fori_loop(0, meta_ref[1], per_tile, 0)
        fill(True)
        fill(False)


def _moe(item_e, item_start, item_nsub, item_size, meta, xs, w_gate_up, b_gate_up, w_down, b_down):
    n_items = item_e.shape[0]
    n_rows = xs.shape[0]
    n_chunks = D_FF // MOE_F
    grid_spec = pltpu.PrefetchScalarGridSpec(
        num_scalar_prefetch=5,
        grid=(n_items,),
        in_specs=[
            pl.BlockSpec(memory_space=pl.ANY),
            pl.BlockSpec(memory_space=pl.ANY),
            pl.BlockSpec((1, n_chunks, 2 * MOE_F), lambda it, ie, ist, ins, isz, mt: (ie[it], 0, 0)),
            pl.BlockSpec(memory_space=pl.ANY),
            pl.BlockSpec((1, 1, D_MODEL), lambda it, ie, ist, ins, isz, mt: (ie[it], 0, 0)),
        ],
        out_specs=pl.BlockSpec(memory_space=pl.ANY),
        scratch_shapes=[
            pltpu.VMEM((N_GU_SLOTS, D_MODEL, 2 * MOE_F), F32),
            pltpu.VMEM((MOE_F, D_MODEL), F32), pltpu.VMEM((MOE_F, D_MODEL), F32),
            pltpu.VMEM((D_MODEL // LANES, MOE_F, LANES), F32), pltpu.VMEM((D_MODEL // LANES, MOE_F, LANES), F32),
            pltpu.VMEM((2, MOE_BM, D_MODEL), F32),
            pltpu.VMEM((MOE_BM, D_MODEL), BF16),
            pltpu.VMEM((2, MOE_SUB, D_MODEL), F32),
            pltpu.VMEM((MOE_SUB, 2 * MOE_F), F32), pltpu.VMEM((MOE_SUB, 2 * MOE_F), F32),
            pltpu.VMEM((MOE_SUB, 2 * MOE_F), F32), pltpu.VMEM((MOE_SUB, 2 * MOE_F), F32),
            pltpu.VMEM((SUBLANES, D_MODEL), F32),
            pltpu.SemaphoreType.DMA((N_GU_SLOTS,)),
            pltpu.SemaphoreType.DMA((2,)),
            pltpu.SemaphoreType.DMA((2,)),
            pltpu.SemaphoreType.DMA((2,)),
            pltpu.SemaphoreType.DMA,
        ],
    )
    return pl.pallas_call(
        functools.partial(_moe_kernel, n_items=n_items, n_chunks=n_chunks),
        grid_spec=grid_spec,
        out_shape=jax.ShapeDtypeStruct((n_rows, D_MODEL), F32),
        compiler_params=_params("arbitrary"),
        name="moe",
    )(item_e, item_start, item_nsub, item_size, meta, xs, w_gate_up,
      b_gate_up.reshape(N_EXPERTS, n_chunks, 2 * MOE_F), w_down, b_down.reshape(N_EXPERTS, 1, D_MODEL))


def _combine_kernel(dest_ref, ys_ref, h_ref, gate_ref, p_ref, wple_ref, pg_ref, wpg_ref, o_ref, ybuf_ref, sem, *, t):
    i = pl.program_id(0)
    n = pl.num_programs(0)
    tb = ROW_TB
    slot = i & 1

    def gather(blk, sl, start):
        def per_group(g, c):
            for u in range(SUBLANES):
                for k in range(TOP_K):
                    d = dest_ref[k * t + blk * tb + g * SUBLANES + u]
                    cp = _row_copy(ys_ref, d, ybuf_ref.at[sl, k, g], u, sem.at[sl])
                    cp.start() if start else cp.wait()
            return c
        lax.fori_loop(0, tb // SUBLANES, per_group, 0)

    def on_slot(cond, blk, rel, start):
        for sl in range(2):
            @pl.when(cond & (slot == (sl ^ rel)))
            def _():
                gather(blk, sl, start)

    on_slot(i == 0, 0, 0, True)
    on_slot(i + 1 < n, i + 1, 1, True)
    on_slot(i >= 0, i, 0, False)

    gates = gate_ref[...]
    h2 = h_ref[...]
    for k in range(TOP_K):
        h2 = h2 + ybuf_ref[slot, k].reshape(tb, D_MODEL) * gates[:, k:k + 1]
    gate = jax.nn.sigmoid(jnp.dot(_rms(h2, pg_ref[...]).astype(BF16), wpg_ref[...], preferred_element_type=F32))
    o_ref[...] = h2 + jnp.dot(p_ref[...].astype(BF16), wple_ref[...], preferred_element_type=F32) * gate


def _combine(dest_flat, ys, h, gates_tk, p2, w_ple_bf, ple_norm_g, w_pg_bf):
    t = h.shape[0]
    tb = ROW_TB
    grid_spec = pltpu.PrefetchScalarGridSpec(
        num_scalar_prefetch=1,
        grid=(t // tb,),
        in_specs=[
            pl.BlockSpec(memory_space=pl.ANY),
            pl.BlockSpec((tb, D_MODEL), lambda i, d: (i, 0)),
            pl.BlockSpec((tb, TOP_K), lambda i, d: (i, 0)),
            pl.BlockSpec((tb, D_PLE), lambda i, d: (i, 0)),
            pl.BlockSpec((D_PLE, D_MODEL), lambda i, d: (0, 0)),
            pl.BlockSpec((1, D_MODEL), lambda i, d: (0, 0)),
            pl.BlockSpec((D_MODEL, D_MODEL), lambda i, d: (0, 0)),
        ],
        out_specs=pl.BlockSpec((tb, D_MODEL), lambda i, d: (i, 0)),
        scratch_shapes=[pltpu.VMEM((2, TOP_K, tb // SUBLANES, SUBLANES, D_MODEL), F32),
                        pltpu.SemaphoreType.DMA((2,))],
    )
    return pl.pallas_call(
        functools.partial(_combine_kernel, t=t),
        grid_spec=grid_spec,
        out_shape=jax.ShapeDtypeStruct((t, D_MODEL), F32),
        compiler_params=_params("arbitrary"),
        name="combine",
    )(dest_flat, ys, h, gates_tk, p2, w_ple_bf, ple_norm_g.reshape(1, D_MODEL), w_pg_bf)


def _plan(idx_t, rank_t, counts, t):
    tk = t * TOP_K
    n_items = tk // MOE_BM + N_EXPERTS
    n_rows = tk + N_EXPERTS * ROW_ALIGN + MOE_SUB
    seg = (counts + ROW_ALIGN - 1) // ROW_ALIGN * ROW_ALIGN
    seg_end = jnp.cumsum(seg)
    seg_start = seg_end - seg
    eids = jnp.arange(N_EXPERTS, dtype=I32)[:, None, None]
    first = jnp.sum(jnp.where(idx_t[None] == eids, seg_start[:, None, None], 0), axis=0)
    dest = (first + rank_t).reshape(-1).astype(I32)

    per_e = (counts + MOE_BM - 1) // MOE_BM
    it_end = jnp.cumsum(per_e)
    it_off = it_end - per_e
    total = it_end[-1]
    its = jnp.arange(n_items, dtype=I32)
    valid = its < total
    its_c = jnp.minimum(its, total - 1)
    e_of = jnp.minimum(jnp.searchsorted(it_end, its_c, side="right"), N_EXPERTS - 1).astype(I32)
    li = its_c - it_off[e_of]
    rows = jnp.clip(counts[e_of] - li * MOE_BM, 0, MOE_BM)
    nsub_by_size = [(rows + sz - 1) // sz for sz in MOE_SUBS]
    size_id = jnp.zeros_like(rows)
    for v in range(1, len(MOE_SUBS)):
        size_id = jnp.where(nsub_by_size[v] <= nsub_by_size[0], v, size_id)
    sub_rows = jnp.asarray(MOE_SUBS, I32)[size_id]
    nsub = jnp.where(valid, (rows + sub_rows - 1) // sub_rows, 0).astype(I32)
    start = (seg_start[e_of] + li * MOE_BM).astype(I32)

    used_end = start[total - 1] + sub_rows[total - 1] * nsub[total - 1]
    moe_meta = jnp.stack([used_end // SUBLANES, (n_rows - used_end) // SUBLANES, total]).astype(I32)
    zmeta = jnp.concatenate([
        seg_start + counts, seg - counts,
        jnp.stack([seg_end[-1] // SUBLANES, (n_rows - seg_end[-1]) // SUBLANES]),
    ]).astype(I32)
    return dest, zmeta, e_of, start, nsub, size_id.astype(I32), moe_meta, n_rows


def kernel(x, p, ln1_g, w_in, q_norm_g, k_norm_g, sinks, sg_ln_g, sg_ln_b, w_spatial, b_spatial, attn_out_g,
           sg_out_g, w_out, ln2_g, w_router, b_router, w_gate_up, b_gate_up, w_down, b_down, w_ple, ple_norm_g,
           w_ple_gate):
    bsz, s, d = x.shape
    assert bsz == 1 and d == D_MODEL and p.shape[0] == 1
    t = bsz * s
    h = x.reshape(t, d)
    li = 0
    proj = _in_proj(h, ln1_g[li], w_in[li].astype(BF16))
    h, xn2 = _mixer(proj, h, sinks[li], q_norm_g[li], k_norm_g[li], sg_ln_g[li], sg_ln_b[li], w_spatial[li],
                    b_spatial[li], attn_out_g[li], sg_out_g[li], w_out[li].astype(BF16), ln2_g[li])
    idx_t, gate_t, rank_t, cnt = _router(xn2, w_router[li], b_router[li])
    dest, zmeta, item_e, item_start, item_nsub, item_size, moe_meta, n_rows = _plan(idx_t, rank_t, cnt[:, 0], t)
    xs = _dispatch(dest, zmeta, xn2, n_rows)
    ys = _moe(item_e, item_start, item_nsub, item_size, moe_meta, xs, w_gate_up[li], b_gate_up[li], w_down[li],
              b_down[li])
    out = _combine(dest, ys, h, gate_t.T, p[li].reshape(t, D_PLE), w_ple[li].astype(BF16), ple_norm_g[li],
                   w_ple_gate[li].astype(BF16))
    return out.reshape(bsz, s, d)
```

```python
import functools

import jax
import jax.numpy as jnp
import numpy as np
from jax import lax
from jax.experimental import pallas as pl
from jax.experimental.pallas import tpu as pltpu

F32 = jnp.float32
BF16 = jnp.bfloat16
I32 = jnp.int32

D_MODEL = 2048
D_PLE = 256
N_Q_HEADS = 16
N_KV_HEADS = 4
HEAD_DIM = 64
WINDOW = 128
N_SG_HEADS = 8
SG_HEAD_DIM = 128
D_ATTN = N_Q_HEADS * HEAD_DIM
D_KV = N_KV_HEADS * HEAD_DIM
D_SG = N_SG_HEADS * SG_HEAD_DIM
D_MIX = D_ATTN + D_SG
D_IN = D_ATTN + 2 * D_KV + 2 * D_SG
N_EXPERTS = 32
TOP_K = 4
D_FF = D_MODEL
SWIGLU_LIMIT = 7.0
SWIGLU_ALPHA = 1.702
EPS = 1e-6
NEG_INF = -1e30

SUBLANES = 8
LANES = 128
MXU_DIM = 256
VMEM_LIMIT_BYTES = 56 * 1024 * 1024

IN_BM = 512
MIX_R = 512
ROUTE_TB = 512
ROW_TB = 256
MOE_SUB = 256
MOE_SUBS = (256, 224)
ROW_PACK = 16
MOE_BM = 1280
MOE_F = 256
N_GU_SLOTS = 3
ROW_ALIGN = SUBLANES


def _rms(x, g):
    ms = jnp.mean(x * x, axis=-1, keepdims=True)
    return x * lax.rsqrt(ms + EPS) * g


def _gelu(x):
    return 0.5 * x * (1.0 + lax.erf(x * np.float32(np.sqrt(0.5))))


def _params(*sem):
    return pltpu.CompilerParams(dimension_semantics=sem, vmem_limit_bytes=VMEM_LIMIT_BYTES)


def _in_proj_kernel(x_ref, g_ref, w_ref, o_ref):
    o_ref[...] = jnp.dot(_rms(x_ref[...], g_ref[...]).astype(BF16), w_ref[...].astype(BF16),
                         preferred_element_type=F32)


def _in_proj(x2, ln1_g, w_in):
    t = x2.shape[0]
    return pl.pallas_call(
        _in_proj_kernel,
        grid=(t // IN_BM,),
        in_specs=[
            pl.BlockSpec((IN_BM, D_MODEL), lambda i: (i, 0)),
            pl.BlockSpec((1, D_MODEL), lambda i: (0, 0)),
            pl.BlockSpec((D_MODEL, D_IN), lambda i: (0, 0), pipeline_mode=pl.Buffered(1)),
        ],
        out_specs=pl.BlockSpec((IN_BM, D_IN), lambda i: (i, 0)),
        out_shape=jax.ShapeDtypeStruct((t, D_IN), F32),
        compiler_params=_params("arbitrary"),
        name="in_proj",
    )(x2, ln1_g.reshape(1, D_MODEL), w_in)


def _head_mean_sq(t, bd):
    tt = t * t
    hi = tt.astype(BF16)
    lo = (tt - hi.astype(F32)).astype(BF16)
    w = bd.shape[0]
    parts = []
    for c in range(t.shape[1] // w):
        cs = slice(c * w, (c + 1) * w)
        parts.append(jnp.dot(hi[:, cs], bd, preferred_element_type=F32)
                     + jnp.dot(lo[:, cs], bd, preferred_element_type=F32))
    return jnp.concatenate(parts, axis=1) * (1.0 / HEAD_DIM)


def _mixer_kernel(sinks_ref, proj_ref, pkv_ref, x_ref, qg_ref, kg_ref, bd_ref, lng_ref, lnb_ref,
                  wsp_ref, bsp_ref, ag_ref, sgg_ref, wout_ref, ln2_ref,
                  h_ref, xn2_ref, a_ref, merged_ref):
    i = pl.program_id(0)
    nsb = MIX_R // WINDOW
    grp = N_Q_HEADS // N_KV_HEADS

    bd = bd_ref[...]
    kv_all = jnp.concatenate([pkv_ref[...], proj_ref[:, D_ATTN:D_ATTN + 2 * D_KV]], axis=0)
    k_all = kv_all[:, :D_KV]
    kn = (k_all * lax.rsqrt(_head_mean_sq(k_all, bd) + EPS) * kg_ref[...]).astype(BF16)
    vb = kv_all[:, D_KV:].astype(BF16)
    q_all = proj_ref[:, :D_ATTN]
    qn = (q_all * lax.rsqrt(_head_mean_sq(q_all, bd) + EPS) * qg_ref[...] * (HEAD_DIM ** -0.5)).astype(BF16)

    rows = lax.broadcasted_iota(I32, (grp * WINDOW, 2 * WINDOW), 0)
    cols = lax.broadcasted_iota(I32, (grp * WINDOW, 2 * WINDOW), 1)
    rel = cols - (rows & (WINDOW - 1))
    band = (rel >= 1) & (rel <= WINDOW)
    cur = cols >= WINDOW
    rblk = lax.broadcasted_iota(I32, (grp * WINDOW, 1), 0) // WINDOW

    tr = lax.broadcasted_iota(I32, (WINDOW, WINDOW), 0)
    tc = lax.broadcasted_iota(I32, (WINDOW, WINDOW), 1)
    wsp = [jnp.where(tc <= tr, wsp_ref[g], 0.0).astype(BF16) for g in range(N_SG_HEADS)]

    for sb in range(nsb):
        r0 = sb * WINDOW
        has_prev = (i * nsb + sb) > 0
        ok = band & (cur | has_prev)
        for g in range(N_KV_HEADS):
            kc = kn[r0:r0 + 2 * WINDOW, g * HEAD_DIM:(g + 1) * HEAD_DIM]
            vc = vb[r0:r0 + 2 * WINDOW, g * HEAD_DIM:(g + 1) * HEAD_DIM]
            qg = jnp.concatenate(
                [qn[r0:r0 + WINDOW, (g * grp + j) * HEAD_DIM:(g * grp + j + 1) * HEAD_DIM] for j in range(grp)],
                axis=0)
            s = lax.dot_general(qg, kc, (((1,), (1,)), ((), ())), preferred_element_type=F32)
            s = jnp.where(ok, s, NEG_INF)
            sink = jnp.full((grp * WINDOW, 1), sinks_ref[g * grp], F32)
            for j in range(1, grp):
                sink = jnp.where(rblk == j, sinks_ref[g * grp + j], sink)
            m = jnp.maximum(jnp.max(s, axis=-1, keepdims=True), sink)
            p = jnp.exp(s - m)
            den = jnp.sum(p, axis=-1, keepdims=True) + jnp.exp(sink - m)
            o = jnp.dot((p / den).astype(BF16), vc, preferred_element_type=F32)
            for j in range(grp):
                hh = g * grp + j
                a_ref[r0:r0 + WINDOW, hh * HEAD_DIM:(hh + 1) * HEAD_DIM] = o[j * WINDOW:(j + 1) * WINDOW]
        merged_ref[r0:r0 + WINDOW, :D_ATTN] = _rms(a_ref[r0:r0 + WINDOW, :], ag_ref[...]).astype(BF16)

        u = _gelu(proj_ref[r0:r0 + WINDOW, D_ATTN + 2 * D_KV:D_ATTN + 2 * D_KV + D_SG])
        vg = _gelu(proj_ref[r0:r0 + WINDOW, D_ATTN + 2 * D_KV + D_SG:])
        mu = jnp.mean(vg, axis=-1, keepdims=True)
        vc0 = vg - mu
        var = jnp.mean(vc0 * vc0, axis=-1, keepdims=True)
        vn = (vc0 * lax.rsqrt(var + EPS) * lng_ref[...] + lnb_ref[...]).astype(BF16)
        mixed = jnp.concatenate(
            [jnp.dot(wsp[g], vn[:, g * SG_HEAD_DIM:(g + 1) * SG_HEAD_DIM], preferred_element_type=F32)
             for g in range(N_SG_HEADS)], axis=1) + bsp_ref[...]
        merged_ref[r0:r0 + WINDOW, D_ATTN:] = _rms(u * mixed, sgg_ref[...]).astype(BF16)

        hblk = x_ref[r0:r0 + WINDOW, :] + jnp.dot(merged_ref[r0:r0 + WINDOW, :], wout_ref[...],
                                                  preferred_element_type=F32)
        h_ref[r0:r0 + WINDOW, :] = hblk
        xn2_ref[r0:r0 + WINDOW, :] = _rms(hblk, ln2_ref[...])


def _mixer(proj, x2, sinks, q_norm_g, k_norm_g, sg_ln_g, sg_ln_b, w_spatial, b_spatial,
           attn_out_g, sg_out_g, w_out_bf, ln2_g):
    t = x2.shape[0]
    nsb = MIX_R // WINDOW
    kvw = 2 * D_KV
    qg = jnp.tile(q_norm_g, N_Q_HEADS).reshape(1, D_ATTN)
    kg = jnp.tile(k_norm_g, N_KV_HEADS).reshape(1, D_KV)
    bd = jnp.asarray(np.kron(np.eye(MXU_DIM // HEAD_DIM), np.ones((HEAD_DIM, HEAD_DIM))), BF16)
    bsp = jnp.repeat(b_spatial.T, SG_HEAD_DIM, axis=1)
    row = lambda v, d: v.reshape(1, d)
    const = lambda shape: pl.BlockSpec(shape, lambda i, s: tuple(0 for _ in shape), pipeline_mode=pl.Buffered(1))
    grid_spec = pltpu.PrefetchScalarGridSpec(
        num_scalar_prefetch=1,
        grid=(t // MIX_R,),
        in_specs=[
            pl.BlockSpec((MIX_R, D_IN), lambda i, s: (i, 0)),
            pl.BlockSpec((WINDOW, kvw), lambda i, s: (jnp.maximum(i * nsb - 1, 0), D_ATTN // kvw)),
            pl.BlockSpec((MIX_R, D_MODEL), lambda i, s: (i, 0)),
            const((1, D_ATTN)), const((1, D_KV)), const((MXU_DIM, MXU_DIM)),
            const((1, D_SG)), const((1, D_SG)),
            const((N_SG_HEADS, WINDOW, WINDOW)), const((WINDOW, D_SG)),
            const((1, D_ATTN)), const((1, D_SG)),
            const((D_MIX, D_MODEL)), const((1, D_MODEL)),
        ],
        out_specs=[
            pl.BlockSpec((MIX_R, D_MODEL), lambda i, s: (i, 0)),
            pl.BlockSpec((MIX_R, D_MODEL), lambda i, s: (i, 0)),
        ],
        scratch_shapes=[pltpu.VMEM((MIX_R, D_ATTN), F32), pltpu.VMEM((MIX_R, D_MIX), BF16)],
    )
    return pl.pallas_call(
        _mixer_kernel,
        grid_spec=grid_spec,
        out_shape=[jax.ShapeDtypeStruct((t, D_MODEL), F32), jax.ShapeDtypeStruct((t, D_MODEL), F32)],
        compiler_params=_params("arbitrary"),
        name="mixer",
    )(sinks, proj, proj, x2, qg, kg, bd, row(sg_ln_g, D_SG), row(sg_ln_b, D_SG), w_spatial, bsp,
      row(attn_out_g, D_ATTN), row(sg_out_g, D_SG), w_out_bf, row(ln2_g, D_MODEL))


def _router_kernel(xn_ref, wr_ref, br_ref, idx_ref, gate_ref, rank_ref, cnt_ref, carry_ref):
    i = pl.program_id(0)
    tb = ROUTE_TB

    @pl.when(i == 0)
    def _():
        carry_ref[...] = jnp.zeros_like(carry_ref)

    def split(v):
        hi = v.astype(BF16)
        return hi, (v - hi.astype(F32)).astype(BF16)

    def nt(a, b):
        return lax.dot_general(a, b, (((1,), (1,)), ((), ())), preferred_element_type=F32)

    w_hi, w_lo = split(wr_ref[...])
    x_hi, x_lo = split(xn_ref[...])
    logits = nt(w_hi, x_hi) + (nt(w_hi, x_lo) + nt(w_lo, x_hi)) + br_ref[...]
    eid = lax.broadcasted_iota(I32, (N_EXPERTS, tb), 0)
    work = logits
    vals, idxs = [], []
    for _ in range(TOP_K):
        m = jnp.max(work, axis=0, keepdims=True)
        sel = jnp.min(jnp.where(work == m, eid, N_EXPERTS), axis=0, keepdims=True)
        vals.append(m)
        idxs.append(sel)
        work = jnp.where(eid == sel, -jnp.inf, work)
    ex = [jnp.exp(v - vals[0]) for v in vals]
    den = ex[0] + ex[1] + ex[2] + ex[3]
    chosen = (eid == idxs[0]) | (eid == idxs[1]) | (eid == idxs[2]) | (eid == idxs[3])
    onehot = jnp.where(chosen, 1.0, 0.0)

    tr = lax.broadcasted_iota(I32, (tb, tb), 0)
    tc = lax.broadcasted_iota(I32, (tb, tb), 1)
    before = jnp.where(tr < tc, 1.0, 0.0).astype(BF16)
    cnt = jnp.dot(onehot.astype(BF16), before, preferred_element_type=F32) + carry_ref[:, :1]
    for k in range(TOP_K):
        idx_ref[k:k + 1, :] = idxs[k]
        gate_ref[k:k + 1, :] = ex[k] / den
        rank_ref[k:k + 1, :] = jnp.sum(jnp.where(eid == idxs[k], cnt, 0.0), axis=0, keepdims=True).astype(I32)
    carry_ref[...] = carry_ref[...] + jnp.sum(onehot, axis=1, keepdims=True)
    cnt_ref[...] = carry_ref[...].astype(I32)


def _router(xn2, w_router, b_router):
    t = xn2.shape[0]
    tb = ROUTE_TB
    return pl.pallas_call(
        _router_kernel,
        grid=(t // tb,),
        in_specs=[
            pl.BlockSpec((tb, D_MODEL), lambda i: (i, 0)),
            pl.BlockSpec((N_EXPERTS, D_MODEL), lambda i: (0, 0)),
            pl.BlockSpec((N_EXPERTS, 1), lambda i: (0, 0)),
        ],
        out_specs=[
            pl.BlockSpec((TOP_K, tb), lambda i: (0, i)),
            pl.BlockSpec((TOP_K, tb), lambda i: (0, i)),
            pl.BlockSpec((TOP_K, tb), lambda i: (0, i)),
            pl.BlockSpec((N_EXPERTS, LANES), lambda i: (0, 0)),
        ],
        out_shape=[
            jax.ShapeDtypeStruct((TOP_K, t), I32),
            jax.ShapeDtypeStruct((TOP_K, t), F32),
            jax.ShapeDtypeStruct((TOP_K, t), I32),
            jax.ShapeDtypeStruct((N_EXPERTS, LANES), I32),
        ],
        scratch_shapes=[pltpu.VMEM((N_EXPERTS, LANES), F32)],
        compiler_params=_params("arbitrary"),
        name="router",
    )(xn2, w_router.T, b_router.reshape(N_EXPERTS, 1))


def _row_copy(src, s, dst, d, sem):
    return pltpu.make_async_copy(src.at[pl.ds(s, 1)], dst.at[pl.ds(d, 1)], sem)


def _tile_copy(src, dst, d8, sem):
    return pltpu.make_async_copy(src, dst.at[pl.ds(pl.multiple_of(d8 * SUBLANES, SUBLANES), SUBLANES)], sem)


def _dispatch_kernel(dest_ref, zmeta_ref, xn_ref, xs_ref, zero_ref, sem, zsem, *, t):
    i = pl.program_id(0)
    tb = ROW_TB

    @pl.when(i == 0)
    def _():
        zero_ref[...] = jnp.zeros_like(zero_ref)

        def pads(start):
            def per_expert(e, c):
                def per_row(r, c2):
                    cp = _row_copy(zero_ref, 0, xs_ref, zmeta_ref[e] + r, zsem)
                    cp.start() if start else cp.wait()
                    return c2
                return lax.fori_loop(0, zmeta_ref[N_EXPERTS + e], per_row, c)
            lax.fori_loop(0, N_EXPERTS, per_expert, 0)

            def per_tile(r, c):
                cp = _tile_copy(zero_ref, xs_ref, zmeta_ref[2 * N_EXPERTS] + r, zsem)
                cp.start() if start else cp.wait()
                return c
            lax.fori_loop(0, zmeta_ref[2 * N_EXPERTS + 1], per_tile, 0)

        pads(True)
        pads(False)

    def rows(start):
        def per_group(g, c):
            for u in range(SUBLANES):
                for k in range(TOP_K):
                    d = dest_ref[k * t + i * tb + g * SUBLANES + u]
                    cp = _row_copy(xn_ref.at[g], u, xs_ref, d, sem)
                    cp.start(priority=k % 2) if start else cp.wait()
            return c
        lax.fori_loop(0, tb // SUBLANES, per_group, 0)
    rows(True)
    rows(False)


def _dispatch(dest_flat, zmeta, xn2, n_rows):
    t = xn2.shape[0]
    grid_spec = pltpu.PrefetchScalarGridSpec(
        num_scalar_prefetch=2,
        grid=(t // ROW_TB,),
        in_specs=[pl.BlockSpec((ROW_TB // SUBLANES, SUBLANES, D_MODEL), lambda i, d, z: (i, 0, 0))],
        out_specs=pl.BlockSpec(memory_space=pl.ANY),
        scratch_shapes=[pltpu.VMEM((SUBLANES, D_MODEL), F32), pltpu.SemaphoreType.DMA, pltpu.SemaphoreType.DMA],
    )
    return pl.pallas_call(
        functools.partial(_dispatch_kernel, t=t),
        grid_spec=grid_spec,
        out_shape=jax.ShapeDtypeStruct((n_rows, D_MODEL), F32),
        compiler_params=_params("arbitrary"),
        name="dispatch",
    )(dest_flat, zmeta, xn2.reshape(t // SUBLANES, SUBLANES, D_MODEL))


def _moe_kernel(ie_ref, ist_ref, ins_ref, isz_ref, meta_ref, xs_ref, wgu_hbm, bgu_ref, wd_hbm, bd_ref, ys_ref,
                rawg_ref, rawd0_ref, rawd1_ref, wdm0_ref, wdm1_ref,
                acc_ref, xbf_ref, xst_ref, ha0_ref, hb0_ref, ha1_ref, hb1_ref,
                zero_ref, gsem, dsem, xsem, osem, zsem, *, n_items, n_chunks):
    h_refs = ((ha0_ref, hb0_ref), (ha1_ref, hb1_ref))
    rawd_ref, wdm_ref = (rawd0_ref, rawd1_ref), (wdm0_ref, wdm1_ref)
    it = pl.program_id(0)
    nsub = ins_ref[it]
    start = ist_ref[it]
    total = meta_ref[2]
    slot_acc = it & 1
    f = MOE_F
    half = f // 2

    def by_size(item, fn):
        for v, sub in enumerate(MOE_SUBS):
            @pl.when(isz_ref[item] == v)
            def _():
                fn(sub)

    def x_copy(item_start, s, slot, sub):
        r = pl.multiple_of(item_start + s * sub, SUBLANES)
        return pltpu.make_async_copy(xs_ref.at[pl.ds(r, sub)], xst_ref.at[slot, pl.ds(0, sub)], xsem.at[slot])

    def x_prefetch(item, sub):
        for s in range(2):
            @pl.when(s < ins_ref[item])
            def _():
                x_copy(ist_ref[item], s, s, sub).start()

    def y_copies(item, start_dma):
        def sized(sub):
            def one(s, c):
                r = pl.multiple_of(ist_ref[item] + s * sub, SUBLANES)
                cp = pltpu.make_async_copy(acc_ref.at[item & 1, pl.ds(s * sub, sub)],
                                           ys_ref.at[pl.ds(r, sub)], osem.at[item & 1])
                cp.start() if start_dma else cp.wait()
                return c
            lax.fori_loop(0, ins_ref[item], one, 0)
        by_size(item, sized)

    def gslot(item, j):
        return lax.rem(item * n_chunks + j, N_GU_SLOTS)

    def w_copies(item, j, slot):
        e = ie_ref[item]
        g = gslot(item, j)
        cg = pl.multiple_of(j * 2 * f, 2 * f)
        cd = pl.multiple_of(j * f, f)
        return (pltpu.make_async_copy(wgu_hbm.at[e, :, pl.ds(cg, 2 * f)], rawg_ref.at[g], gsem.at[g]),
                pltpu.make_async_copy(wd_hbm.at[e, pl.ds(cd, f), :], rawd_ref[slot], dsem.at[slot]))

    def interleave_down(slot):
        for c in range(D_MODEL // LANES):
            cs = slice(c * LANES, (c + 1) * LANES)
            wdm_ref[slot][c, pl.ds(0, half, stride=2), :] = rawd_ref[slot][:half, cs]
            wdm_ref[slot][c, pl.ds(1, half, stride=2), :] = rawd_ref[slot][half:, cs]

    @pl.when(it == 0)
    def _():
        for c in range(2):
            for cp in w_copies(0, c, c):
                cp.start()
        for cp in w_copies(0, 0, 0):
            cp.wait()
        interleave_down(0)
        by_size(0, lambda sub: x_prefetch(0, sub))

    def item_body(sub):
        def stage(s, c):
            slot = s & 1
            r = pl.multiple_of(s * sub, ROW_PACK)
            x_copy(start, s, slot, sub).wait()
            xbf_ref[pl.ds(r, sub), :] = xst_ref[slot, :sub, :].astype(BF16)
            acc_ref[slot_acc, pl.ds(r, sub), :] = jnp.broadcast_to(bd_ref[0], (sub, D_MODEL))

            @pl.when(s + 2 < nsub)
            def _():
                x_copy(start, s + 2, slot, sub).start()
            return c
        lax.fori_loop(0, nsub, stage, 0)

        nxt = jnp.minimum(it + 1, n_items - 1)

        @pl.when(it + 1 < n_items)
        def _():
            by_size(nxt, lambda nsz: x_prefetch(nxt, nsz))

        even = (lax.broadcasted_iota(I32, (sub, f), 1) & 1) == 0

        def up_proj(s, h_ref, j):
            r = pl.multiple_of(s * sub, ROW_PACK)
            h_ref[:sub, :] = jnp.dot(xbf_ref[pl.ds(r, sub), :], rawg_ref[gslot(it, j)].astype(BF16),
                                     preferred_element_type=F32)

        def down_proj(s, h_ref, w, j):
            r = pl.multiple_of(s * sub, ROW_PACK)
            h = h_ref[:sub, :] + bgu_ref[0, pl.ds(j, 1), :]
            gate = jnp.minimum(h, SWIGLU_LIMIT)
            glu = gate * jax.nn.sigmoid(gate * SWIGLU_ALPHA)
            up = jnp.clip(h, -SWIGLU_LIMIT, SWIGLU_LIMIT) + 1.0
            prod = glu * pltpu.roll(up, 2 * f - 1, axis=1)
            act = jnp.where(even, prod[:, :f], pltpu.roll(prod, 1, axis=1)[:, f:])
            wd = jnp.concatenate([wdm_ref[w][c] for c in range(D_MODEL // LANES)], axis=1).astype(BF16)
            acc_ref[slot_acc, pl.ds(r, sub), :] += jnp.dot(act.astype(BF16), wd, preferred_element_type=F32)

        def chunk_pass(j, w, can_be_last):
            j2 = j + 2
            it2 = it + jnp.where(j2 >= n_chunks, 1, 0)

            @pl.when(it2 < total)
            def _():
                for cp in w_copies(jnp.minimum(it2, n_items - 1), j2 & (n_chunks - 1), w):
                    cp.start()

            j1 = j + 1
            it1 = it + jnp.where(j1 >= n_chunks, 1, 0)

            @pl.when(it1 < total)
            def _():
                for cp in w_copies(jnp.minimum(it1, n_items - 1), j1 & (n_chunks - 1), 1 - w):
                    cp.wait()

            interleave_down(1 - w)

            ha_ref, hb_ref = h_refs[w]
            ha_next = h_refs[1 - w][0]
            n_pairs = (nsub - 1) // 2

            def pair(i, c):
                s = 2 * i
                up_proj(s + 1, hb_ref, j)
                down_proj(s, ha_ref, w, j)
                up_proj(s + 2, ha_ref, j)
                down_proj(s + 1, hb_ref, w, j)
                return c
            lax.fori_loop(0, n_pairs, pair, 0)
            s = 2 * n_pairs

            def tail(with_next):
                @pl.when(s == nsub - 1)
                def _():
                    if with_next:
                        up_proj(0, ha_next, j + 1)
                    down_proj(s, ha_ref, w, j)

                @pl.when(s < nsub - 1)
                def _():
                    up_proj(s + 1, hb_ref, j)
                    down_proj(s, ha_ref, w, j)
                    if with_next:
                        up_proj(0, ha_next, j + 1)
                    down_proj(s + 1, hb_ref, w, j)

            if can_be_last:
                @pl.when(j < n_chunks - 1)
                def _():
                    tail(True)

                @pl.when(j == n_chunks - 1)
                def _():
                    tail(False)
            else:
                tail(True)

        up_proj(0, h_refs[0][0], 0)

        def chunk_pair(p, c):
            chunk_pass(2 * p, 0, False)
            chunk_pass(2 * p + 1, 1, True)
            return c
        lax.fori_loop(0, n_chunks // 2, chunk_pair, 0)

    @pl.when(nsub > 0)
    def _():
        by_size(it, item_body)

    @pl.when(it > 0)
    def _():
        y_copies(it - 1, False)

    y_copies(it, True)

    @pl.when(it == n_items - 1)
    def _():
        y_copies(it, False)

        zero_ref[...] = jnp.zeros_like(zero_ref)

        def fill(start_dma):
            def per_tile(r, c):
                cp = _tile_copy(zero_ref, ys_ref, meta_ref[0] + r, zsem)
                cp.start() if start_dma else cp.wait()
                return c
            lax.fori_loop(0, meta_ref[1], per_tile, 0)
        fill(True)
        fill(False)


def _moe(item_e, item_start, item_nsub, item_size, meta, xs, w_gate_up, b_gate_up, w_down, b_down):
    n_items = item_e.shape[0]
    n_rows = xs.shape[0]
    n_chunks = D_FF // MOE_F
    grid_spec = pltpu.PrefetchScalarGridSpec(
        num_scalar_prefetch=5,
        grid=(n_items,),
        in_specs=[
            pl.BlockSpec(memory_space=pl.ANY),
            pl.BlockSpec(memory_space=pl.ANY),
            pl.BlockSpec((1, n_chunks, 2 * MOE_F), lambda it, ie, ist, ins, isz, mt: (ie[it], 0, 0)),
            pl.BlockSpec(memory_space=pl.ANY),
            pl.BlockSpec((1, 1, D_MODEL), lambda it, ie, ist, ins, isz, mt: (ie[it], 0, 0)),
        ],
        out_specs=pl.BlockSpec(memory_space=pl.ANY),
        scratch_shapes=[
            pltpu.VMEM((N_GU_SLOTS, D_MODEL, 2 * MOE_F), F32),
            pltpu.VMEM((MOE_F, D_MODEL), F32), pltpu.VMEM((MOE_F, D_MODEL), F32),
            pltpu.VMEM((D_MODEL // LANES, MOE_F, LANES), F32), pltpu.VMEM((D_MODEL // LANES, MOE_F, LANES), F32),
            pltpu.VMEM((2, MOE_BM, D_MODEL), F32),
            pltpu.VMEM((MOE_BM, D_MODEL), BF16),
            pltpu.VMEM((2, MOE_SUB, D_MODEL), F32),
            pltpu.VMEM((MOE_SUB, 2 * MOE_F), F32), pltpu.VMEM((MOE_SUB, 2 * MOE_F), F32),
            pltpu.VMEM((MOE_SUB, 2 * MOE_F), F32), pltpu.VMEM((MOE_SUB, 2 * MOE_F), F32),
            pltpu.VMEM((SUBLANES, D_MODEL), F32),
            pltpu.SemaphoreType.DMA((N_GU_SLOTS,)),
            pltpu.SemaphoreType.DMA((2,)),
            pltpu.SemaphoreType.DMA((2,)),
            pltpu.SemaphoreType.DMA((2,)),
            pltpu.SemaphoreType.DMA,
        ],
    )
    return pl.pallas_call(
        functools.partial(_moe_kernel, n_items=n_items, n_chunks=n_chunks),
        grid_spec=grid_spec,
        out_shape=jax.ShapeDtypeStruct((n_rows, D_MODEL), F32),
        compiler_params=_params("arbitrary"),
        name="moe",
    )(item_e, item_start, item_nsub, item_size, meta, xs, w_gate_up,
      b_gate_up.reshape(N_EXPERTS, n_chunks, 2 * MOE_F), w_down, b_down.reshape(N_EXPERTS, 1, D_MODEL))


def _combine_kernel(dest_ref, ys_ref, h_ref, gate_ref, p_ref, wple_ref, pg_ref, wpg_ref, o_ref, ybuf_ref, sem, *, t):
    i = pl.program_id(0)
    n = pl.num_programs(0)
    tb = ROW_TB
    slot = i & 1

    def gather(blk, sl, start):
        def per_group(g, c):
            for u in range(SUBLANES):
                for k in range(TOP_K):
                    d = dest_ref[k * t + blk * tb + g * SUBLANES + u]
                    cp = _row_copy(ys_ref, d, ybuf_ref.at[sl, k, g], u, sem.at[sl])
                    cp.start() if start else cp.wait()
            return c
        lax.fori_loop(0, tb // SUBLANES, per_group, 0)

    def on_slot(cond, blk, rel, start):
        for sl in range(2):
            @pl.when(cond & (slot == (sl ^ rel)))
            def _():
                gather(blk, sl, start)

    on_slot(i == 0, 0, 0, True)
    on_slot(i + 1 < n, i + 1, 1, True)
    on_slot(i >= 0, i, 0, False)

    gates = gate_ref[...]
    h2 = h_ref[...]
    for k in range(TOP_K):
        h2 = h2 + ybuf_ref[slot, k].reshape(tb, D_MODEL) * gates[:, k:k + 1]
    gate = jax.nn.sigmoid(jnp.dot(_rms(h2, pg_ref[...]).astype(BF16), wpg_ref[...].astype(BF16),
                                  preferred_element_type=F32))
    o_ref[...] = h2 + jnp.dot(p_ref[...].astype(BF16), wple_ref[...].astype(BF16),
                              preferred_element_type=F32) * gate


def _combine(dest_flat, ys, h, gates_tk, p2, w_ple, ple_norm_g, w_pg):
    t = h.shape[0]
    tb = ROW_TB
    grid_spec = pltpu.PrefetchScalarGridSpec(
        num_scalar_prefetch=1,
        grid=(t // tb,),
        in_specs=[
            pl.BlockSpec(memory_space=pl.ANY),
            pl.BlockSpec((tb, D_MODEL), lambda i, d: (i, 0)),
            pl.BlockSpec((tb, TOP_K), lambda i, d: (i, 0)),
            pl.BlockSpec((tb, D_PLE), lambda i, d: (i, 0)),
            pl.BlockSpec((D_PLE, D_MODEL), lambda i, d: (0, 0), pipeline_mode=pl.Buffered(1)),
            pl.BlockSpec((1, D_MODEL), lambda i, d: (0, 0)),
            pl.BlockSpec((D_MODEL, D_MODEL), lambda i, d: (0, 0), pipeline_mode=pl.Buffered(1)),
        ],
        out_specs=pl.BlockSpec((tb, D_MODEL), lambda i, d: (i, 0)),
        scratch_shapes=[pltpu.VMEM((2, TOP_K, tb // SUBLANES, SUBLANES, D_MODEL), F32),
                        pltpu.SemaphoreType.DMA((2,))],
    )
    return pl.pallas_call(
        functools.partial(_combine_kernel, t=t),
        grid_spec=grid_spec,
        out_shape=jax.ShapeDtypeStruct((t, D_MODEL), F32),
        compiler_params=_params("arbitrary"),
        name="combine",
    )(dest_flat, ys, h, gates_tk, p2, w_ple, ple_norm_g.reshape(1, D_MODEL), w_pg)


def _plan(idx_t, rank_t, counts, t):
    tk = t * TOP_K
    n_items = tk // MOE_BM + N_EXPERTS
    n_rows = tk + N_EXPERTS * ROW_ALIGN + MOE_SUB
    seg = (counts + ROW_ALIGN - 1) // ROW_ALIGN * ROW_ALIGN
    seg_end = jnp.cumsum(seg)
    seg_start = seg_end - seg
    eids = jnp.arange(N_EXPERTS, dtype=I32)[:, None, None]
    first = jnp.sum(jnp.where(idx_t[None] == eids, seg_start[:, None, None], 0), axis=0)
    dest = (first + rank_t).reshape(-1).astype(I32)

    per_e = (counts + MOE_BM - 1) // MOE_BM
    it_end = jnp.cumsum(per_e)
    it_off = it_end - per_e
    total = it_end[-1]
    its = jnp.arange(n_items, dtype=I32)
    valid = its < total
    its_c = jnp.minimum(its, total - 1)
    e_of = jnp.minimum(jnp.searchsorted(it_end, its_c, side="right"), N_EXPERTS - 1).astype(I32)
    li = its_c - it_off[e_of]
    rows = jnp.clip(counts[e_of] - li * MOE_BM, 0, MOE_BM)
    nsub_by_size = [(rows + sz - 1) // sz for sz in MOE_SUBS]
    size_id = jnp.zeros_like(rows)
    for v in range(1, len(MOE_SUBS)):
        size_id = jnp.where(nsub_by_size[v] <= nsub_by_size[0], v, size_id)
    sub_rows = jnp.asarray(MOE_SUBS, I32)[size_id]
    nsub = jnp.where(valid, (rows + sub_rows - 1) // sub_rows, 0).astype(I32)
    start = (seg_start[e_of] + li * MOE_BM).astype(I32)

    used_end = start[total - 1] + sub_rows[total - 1] * nsub[total - 1]
    moe_meta = jnp.stack([used_end // SUBLANES, (n_rows - used_end) // SUBLANES, total]).astype(I32)
    zmeta = jnp.concatenate([
        seg_start + counts, seg - counts,
        jnp.stack([seg_end[-1] // SUBLANES, (n_rows - seg_end[-1]) // SUBLANES]),
    ]).astype(I32)
    return dest, zmeta, e_of, start, nsub, size_id.astype(I32), moe_meta, n_rows


def kernel(x, p, ln1_g, w_in, q_norm_g, k_norm_g, sinks, sg_ln_g, sg_ln_b, w_spatial, b_spatial, attn_out_g,
           sg_out_g, w_out, ln2_g, w_router, b_router, w_gate_up, b_gate_up, w_down, b_down, w_ple, ple_norm_g,
           w_ple_gate):
    bsz, s, d = x.shape
    assert bsz == 1 and d == D_MODEL and p.shape[0] == 1
    t = bsz * s
    h = x.reshape(t, d)
    li = 0
    proj = _in_proj(h, ln1_g[li], w_in[li])
    h, xn2 = _mixer(proj, h, sinks[li], q_norm_g[li], k_norm_g[li], sg_ln_g[li], sg_ln_b[li], w_spatial[li],
                    b_spatial[li], attn_out_g[li], sg_out_g[li], w_out[li].astype(BF16), ln2_g[li])
    idx_t, gate_t, rank_t, cnt = _router(xn2, w_router[li], b_router[li])
    dest, zmeta, item_e, item_start, item_nsub, item_size, moe_meta, n_rows = _plan(idx_t, rank_t, cnt[:, 0], t)
    xs = _dispatch(dest, zmeta, xn2, n_rows)
    ys = _moe(item_e, item_start, item_nsub, item_size, moe_meta, xs, w_gate_up[li], b_gate_up[li], w_down[li],
              b_down[li])
    out = _combine(dest, ys, h, gate_t.T, p[li].reshape(t, D_PLE), w_ple[li], ple_norm_g[li], w_ple_gate[li])
    return out.reshape(bsz, s, d)
```

```python
import functools

import jax
import jax.numpy as jnp
import numpy as np
from jax import lax
from jax.experimental import pallas as pl
from jax.experimental.pallas import tpu as pltpu

F32 = jnp.float32
BF16 = jnp.bfloat16
I32 = jnp.int32

D_MODEL = 2048
D_PLE = 256
N_Q_HEADS = 16
N_KV_HEADS = 4
HEAD_DIM = 64
WINDOW = 128
N_SG_HEADS = 8
SG_HEAD_DIM = 128
D_ATTN = N_Q_HEADS * HEAD_DIM
D_KV = N_KV_HEADS * HEAD_DIM
D_SG = N_SG_HEADS * SG_HEAD_DIM
D_MIX = D_ATTN + D_SG
D_IN = D_ATTN + 2 * D_KV + 2 * D_SG
N_EXPERTS = 32
TOP_K = 4
D_FF = D_MODEL
SWIGLU_LIMIT = 7.0
SWIGLU_ALPHA = 1.702
EPS = 1e-6
NEG_INF = -1e30

SUBLANES = 8
LANES = 128
MXU_DIM = 256
VMEM_LIMIT_BYTES = 56 * 1024 * 1024

IN_BM = 512
MIX_R = 512
ROUTE_TB = 512
ROW_TB = 256
MOE_SUB = 256
MOE_SUBS = (256, 224)
ROW_PACK = 16
MOE_BM = 1280
MOE_F = 256
ROW_ALIGN = SUBLANES


def _rms(x, g):
    ms = jnp.mean(x * x, axis=-1, keepdims=True)
    return x * lax.rsqrt(ms + EPS) * g


def _gelu(x):
    return 0.5 * x * (1.0 + lax.erf(x * np.float32(np.sqrt(0.5))))


def _params(*sem):
    return pltpu.CompilerParams(dimension_semantics=sem, vmem_limit_bytes=VMEM_LIMIT_BYTES)


def _in_proj_kernel(x_ref, g_ref, w_ref, o_ref):
    o_ref[...] = jnp.dot(_rms(x_ref[...], g_ref[...]).astype(BF16), w_ref[...].astype(BF16),
                         preferred_element_type=F32)


def _in_proj(x2, ln1_g, w_in):
    t = x2.shape[0]
    return pl.pallas_call(
        _in_proj_kernel,
        grid=(t // IN_BM,),
        in_specs=[
            pl.BlockSpec((IN_BM, D_MODEL), lambda i: (i, 0)),
            pl.BlockSpec((1, D_MODEL), lambda i: (0, 0)),
            pl.BlockSpec((D_MODEL, D_IN), lambda i: (0, 0), pipeline_mode=pl.Buffered(1)),
        ],
        out_specs=pl.BlockSpec((IN_BM, D_IN), lambda i: (i, 0)),
        out_shape=jax.ShapeDtypeStruct((t, D_IN), F32),
        compiler_params=_params("arbitrary"),
        name="in_proj",
    )(x2, ln1_g.reshape(1, D_MODEL), w_in)


def _head_mean_sq(t, bd):
    tt = t * t
    hi = tt.astype(BF16)
    lo = (tt - hi.astype(F32)).astype(BF16)
    w = bd.shape[0]
    parts = []
    for c in range(t.shape[1] // w):
        cs = slice(c * w, (c + 1) * w)
        parts.append(jnp.dot(hi[:, cs], bd, preferred_element_type=F32)
                     + jnp.dot(lo[:, cs], bd, preferred_element_type=F32))
    return jnp.concatenate(parts, axis=1) * (1.0 / HEAD_DIM)


def _mixer_kernel(sinks_ref, proj_ref, pkv_ref, x_ref, qg_ref, kg_ref, bd_ref, lng_ref, lnb_ref,
                  wsp_ref, bsp_ref, ag_ref, sgg_ref, wout_ref, ln2_ref,
                  h_ref, xn2_ref, a_ref, merged_ref):
    i = pl.program_id(0)
    nsb = MIX_R // WINDOW
    grp = N_Q_HEADS // N_KV_HEADS

    bd = bd_ref[...]
    kv_all = jnp.concatenate([pkv_ref[...], proj_ref[:, D_ATTN:D_ATTN + 2 * D_KV]], axis=0)
    k_all = kv_all[:, :D_KV]
    kn = (k_all * lax.rsqrt(_head_mean_sq(k_all, bd) + EPS) * kg_ref[...]).astype(BF16)
    vb = kv_all[:, D_KV:].astype(BF16)
    q_all = proj_ref[:, :D_ATTN]
    qn = (q_all * lax.rsqrt(_head_mean_sq(q_all, bd) + EPS) * qg_ref[...] * (HEAD_DIM ** -0.5)).astype(BF16)

    rows = lax.broadcasted_iota(I32, (grp * WINDOW, 2 * WINDOW), 0)
    cols = lax.broadcasted_iota(I32, (grp * WINDOW, 2 * WINDOW), 1)
    rel = cols - (rows & (WINDOW - 1))
    band = (rel >= 1) & (rel <= WINDOW)
    cur = cols >= WINDOW
    rblk = lax.broadcasted_iota(I32, (grp * WINDOW, 1), 0) // WINDOW

    tr = lax.broadcasted_iota(I32, (WINDOW, WINDOW), 0)
    tc = lax.broadcasted_iota(I32, (WINDOW, WINDOW), 1)
    wsp = [jnp.where(tc <= tr, wsp_ref[g], 0.0).astype(BF16) for g in range(N_SG_HEADS)]

    for sb in range(nsb):
        r0 = sb * WINDOW
        has_prev = (i * nsb + sb) > 0
        ok = band & (cur | has_prev)
        for g in range(N_KV_HEADS):
            kc = kn[r0:r0 + 2 * WINDOW, g * HEAD_DIM:(g + 1) * HEAD_DIM]
            vc = vb[r0:r0 + 2 * WINDOW, g * HEAD_DIM:(g + 1) * HEAD_DIM]
            qg = jnp.concatenate(
                [qn[r0:r0 + WINDOW, (g * grp + j) * HEAD_DIM:(g * grp + j + 1) * HEAD_DIM] for j in range(grp)],
                axis=0)
            s = lax.dot_general(qg, kc, (((1,), (1,)), ((), ())), preferred_element_type=F32)
            s = jnp.where(ok, s, NEG_INF)
            sink = jnp.full((grp * WINDOW, 1), sinks_ref[g * grp], F32)
            for j in range(1, grp):
                sink = jnp.where(rblk == j, sinks_ref[g * grp + j], sink)
            m = jnp.maximum(jnp.max(s, axis=-1, keepdims=True), sink)
            p = jnp.exp(s - m)
            den = jnp.sum(p, axis=-1, keepdims=True) + jnp.exp(sink - m)
            o = jnp.dot((p / den).astype(BF16), vc, preferred_element_type=F32)
            for j in range(grp):
                hh = g * grp + j
                a_ref[r0:r0 + WINDOW, hh * HEAD_DIM:(hh + 1) * HEAD_DIM] = o[j * WINDOW:(j + 1) * WINDOW]
        merged_ref[r0:r0 + WINDOW, :D_ATTN] = _rms(a_ref[r0:r0 + WINDOW, :], ag_ref[...]).astype(BF16)

        u = _gelu(proj_ref[r0:r0 + WINDOW, D_ATTN + 2 * D_KV:D_ATTN + 2 * D_KV + D_SG])
        vg = _gelu(proj_ref[r0:r0 + WINDOW, D_ATTN + 2 * D_KV + D_SG:])
        mu = jnp.mean(vg, axis=-1, keepdims=True)
        vc0 = vg - mu
        var = jnp.mean(vc0 * vc0, axis=-1, keepdims=True)
        vn = (vc0 * lax.rsqrt(var + EPS) * lng_ref[...] + lnb_ref[...]).astype(BF16)
        mixed = jnp.concatenate(
            [jnp.dot(wsp[g], vn[:, g * SG_HEAD_DIM:(g + 1) * SG_HEAD_DIM], preferred_element_type=F32)
             for g in range(N_SG_HEADS)], axis=1) + bsp_ref[...]
        merged_ref[r0:r0 + WINDOW, D_ATTN:] = _rms(u * mixed, sgg_ref[...]).astype(BF16)

        hblk = x_ref[r0:r0 + WINDOW, :] + jnp.dot(merged_ref[r0:r0 + WINDOW, :], wout_ref[...],
                                                  preferred_element_type=F32)
        h_ref[r0:r0 + WINDOW, :] = hblk
        xn2_ref[r0:r0 + WINDOW, :] = _rms(hblk, ln2_ref[...])


def _mixer(proj, x2, sinks, q_norm_g, k_norm_g, sg_ln_g, sg_ln_b, w_spatial, b_spatial,
           attn_out_g, sg_out_g, w_out_bf, ln2_g):
    t = x2.shape[0]
    nsb = MIX_R // WINDOW
    kvw = 2 * D_KV
    qg = jnp.tile(q_norm_g, N_Q_HEADS).reshape(1, D_ATTN)
    kg = jnp.tile(k_norm_g, N_KV_HEADS).reshape(1, D_KV)
    bd = jnp.asarray(np.kron(np.eye(MXU_DIM // HEAD_DIM), np.ones((HEAD_DIM, HEAD_DIM))), BF16)
    bsp = jnp.repeat(b_spatial.T, SG_HEAD_DIM, axis=1)
    row = lambda v, d: v.reshape(1, d)
    const = lambda shape: pl.BlockSpec(shape, lambda i, s: tuple(0 for _ in shape), pipeline_mode=pl.Buffered(1))
    grid_spec = pltpu.PrefetchScalarGridSpec(
        num_scalar_prefetch=1,
        grid=(t // MIX_R,),
        in_specs=[
            pl.BlockSpec((MIX_R, D_IN), lambda i, s: (i, 0)),
            pl.BlockSpec((WINDOW, kvw), lambda i, s: (jnp.maximum(i * nsb - 1, 0), D_ATTN // kvw)),
            pl.BlockSpec((MIX_R, D_MODEL), lambda i, s: (i, 0)),
            const((1, D_ATTN)), const((1, D_KV)), const((MXU_DIM, MXU_DIM)),
            const((1, D_SG)), const((1, D_SG)),
            const((N_SG_HEADS, WINDOW, WINDOW)), const((WINDOW, D_SG)),
            const((1, D_ATTN)), const((1, D_SG)),
            const((D_MIX, D_MODEL)), const((1, D_MODEL)),
        ],
        out_specs=[
            pl.BlockSpec((MIX_R, D_MODEL), lambda i, s: (i, 0)),
            pl.BlockSpec((MIX_R, D_MODEL), lambda i, s: (i, 0)),
        ],
        scratch_shapes=[pltpu.VMEM((MIX_R, D_ATTN), F32), pltpu.VMEM((MIX_R, D_MIX), BF16)],
    )
    return pl.pallas_call(
        _mixer_kernel,
        grid_spec=grid_spec,
        out_shape=[jax.ShapeDtypeStruct((t, D_MODEL), F32), jax.ShapeDtypeStruct((t, D_MODEL), F32)],
        compiler_params=_params("arbitrary"),
        name="mixer",
    )(sinks, proj, proj, x2, qg, kg, bd, row(sg_ln_g, D_SG), row(sg_ln_b, D_SG), w_spatial, bsp,
      row(attn_out_g, D_ATTN), row(sg_out_g, D_SG), w_out_bf, row(ln2_g, D_MODEL))


def _router_kernel(xn_ref, wr_ref, br_ref, idx_ref, gate_ref, rank_ref, cnt_ref, carry_ref):
    i = pl.program_id(0)
    tb = ROUTE_TB

    @pl.when(i == 0)
    def _():
        carry_ref[...] = jnp.zeros_like(carry_ref)

    def split(v):
        hi = v.astype(BF16)
        return hi, (v - hi.astype(F32)).astype(BF16)

    def nt(a, b):
        return lax.dot_general(a, b, (((1,), (1,)), ((), ())), preferred_element_type=F32)

    w_hi, w_lo = split(wr_ref[...])
    x_hi, x_lo = split(xn_ref[...])
    logits = nt(w_hi, x_hi) + (nt(w_hi, x_lo) + nt(w_lo, x_hi)) + br_ref[...]
    eid = lax.broadcasted_iota(I32, (N_EXPERTS, tb), 0)
    work = logits
    vals, idxs = [], []
    for _ in range(TOP_K):
        m = jnp.max(work, axis=0, keepdims=True)
        sel = jnp.min(jnp.where(work == m, eid, N_EXPERTS), axis=0, keepdims=True)
        vals.append(m)
        idxs.append(sel)
        work = jnp.where(eid == sel, -jnp.inf, work)
    ex = [jnp.exp(v - vals[0]) for v in vals]
    den = ex[0] + ex[1] + ex[2] + ex[3]
    chosen = (eid == idxs[0]) | (eid == idxs[1]) | (eid == idxs[2]) | (eid == idxs[3])
    onehot = jnp.where(chosen, 1.0, 0.0)

    tr = lax.broadcasted_iota(I32, (tb, tb), 0)
    tc = lax.broadcasted_iota(I32, (tb, tb), 1)
    before = jnp.where(tr < tc, 1.0, 0.0).astype(BF16)
    cnt = jnp.dot(onehot.astype(BF16), before, preferred_element_type=F32) + carry_ref[:, :1]
    for k in range(TOP_K):
        idx_ref[k:k + 1, :] = idxs[k]
        gate_ref[k:k + 1, :] = ex[k] / den
        rank_ref[k:k + 1, :] = jnp.sum(jnp.where(eid == idxs[k], cnt, 0.0), axis=0, keepdims=True).astype(I32)
    carry_ref[...] = carry_ref[...] + jnp.sum(onehot, axis=1, keepdims=True)
    cnt_ref[...] = carry_ref[...].astype(I32)


def _router(xn2, w_router, b_router):
    t = xn2.shape[0]
    tb = ROUTE_TB
    return pl.pallas_call(
        _router_kernel,
        grid=(t // tb,),
        in_specs=[
            pl.BlockSpec((tb, D_MODEL), lambda i: (i, 0)),
            pl.BlockSpec((N_EXPERTS, D_MODEL), lambda i: (0, 0)),
            pl.BlockSpec((N_EXPERTS, 1), lambda i: (0, 0)),
        ],
        out_specs=[
            pl.BlockSpec((TOP_K, tb), lambda i: (0, i)),
            pl.BlockSpec((TOP_K, tb), lambda i: (0, i)),
            pl.BlockSpec((TOP_K, tb), lambda i: (0, i)),
            pl.BlockSpec((N_EXPERTS, LANES), lambda i: (0, 0)),
        ],
        out_shape=[
            jax.ShapeDtypeStruct((TOP_K, t), I32),
            jax.ShapeDtypeStruct((TOP_K, t), F32),
            jax.ShapeDtypeStruct((TOP_K, t), I32),
            jax.ShapeDtypeStruct((N_EXPERTS, LANES), I32),
        ],
        scratch_shapes=[pltpu.VMEM((N_EXPERTS, LANES), F32)],
        compiler_params=_params("arbitrary"),
        name="router",
    )(xn2, w_router.T, b_router.reshape(N_EXPERTS, 1))


def _row_copy(src, s, dst, d, sem):
    return pltpu.make_async_copy(src.at[pl.ds(s, 1)], dst.at[pl.ds(d, 1)], sem)


def _tile_copy(src, dst, d8, sem):
    return pltpu.make_async_copy(src, dst.at[pl.ds(pl.multiple_of(d8 * SUBLANES, SUBLANES), SUBLANES)], sem)


def _dispatch_kernel(dest_ref, zmeta_ref, xn_ref, xs_ref, zero_ref, sem, zsem, *, t):
    i = pl.program_id(0)
    tb = ROW_TB

    @pl.when(i == 0)
    def _():
        zero_ref[...] = jnp.zeros_like(zero_ref)

        def pads(start):
            def per_expert(e, c):
                def per_row(r, c2):
                    cp = _row_copy(zero_ref, 0, xs_ref, zmeta_ref[e] + r, zsem)
                    cp.start() if start else cp.wait()
                    return c2
                return lax.fori_loop(0, zmeta_ref[N_EXPERTS + e], per_row, c)
            lax.fori_loop(0, N_EXPERTS, per_expert, 0)

            def per_tile(r, c):
                cp = _tile_copy(zero_ref, xs_ref, zmeta_ref[2 * N_EXPERTS] + r, zsem)
                cp.start() if start else cp.wait()
                return c
            lax.fori_loop(0, zmeta_ref[2 * N_EXPERTS + 1], per_tile, 0)

        pads(True)
        pads(False)

    def rows(start):
        def per_group(g, c):
            for u in range(SUBLANES):
                for k in range(TOP_K):
                    d = dest_ref[k * t + i * tb + g * SUBLANES + u]
                    cp = _row_copy(xn_ref.at[g], u, xs_ref, d, sem)
                    cp.start(priority=k % 2) if start else cp.wait()
            return c
        lax.fori_loop(0, tb // SUBLANES, per_group, 0)
    rows(True)
    rows(False)


def _dispatch(dest_flat, zmeta, xn2, n_rows):
    t = xn2.shape[0]
    grid_spec = pltpu.PrefetchScalarGridSpec(
        num_scalar_prefetch=2,
        grid=(t // ROW_TB,),
        in_specs=[pl.BlockSpec((ROW_TB // SUBLANES, SUBLANES, D_MODEL), lambda i, d, z: (i, 0, 0))],
        out_specs=pl.BlockSpec(memory_space=pl.ANY),
        scratch_shapes=[pltpu.VMEM((SUBLANES, D_MODEL), F32), pltpu.SemaphoreType.DMA, pltpu.SemaphoreType.DMA],
    )
    return pl.pallas_call(
        functools.partial(_dispatch_kernel, t=t),
        grid_spec=grid_spec,
        out_shape=jax.ShapeDtypeStruct((n_rows, D_MODEL), F32),
        compiler_params=_params("arbitrary"),
        name="dispatch",
    )(dest_flat, zmeta, xn2.reshape(t // SUBLANES, SUBLANES, D_MODEL))


def _moe_kernel(ie_ref, ist_ref, ins_ref, isz_ref, meta_ref, xs_ref, wgu_hbm, bgu_ref, wd_hbm, bd_ref, ys_ref,
                rawg0_ref, rawg1_ref, wgu0_ref, wgu1_ref, rawd0_ref, rawd1_ref, wdm0_ref, wdm1_ref,
                acc_ref, xbf_ref, xst_ref, ha0_ref, hb0_ref, ha1_ref, hb1_ref,
                zero_ref, gsem, dsem, xsem, osem, zsem, *, n_items, n_chunks):
    h_refs = ((ha0_ref, hb0_ref), (ha1_ref, hb1_ref))
    rawg_ref, wgu_bf_ref = (rawg0_ref, rawg1_ref), (wgu0_ref, wgu1_ref)
    rawd_ref, wdm_ref = (rawd0_ref, rawd1_ref), (wdm0_ref, wdm1_ref)
    it = pl.program_id(0)
    nsub = ins_ref[it]
    start = ist_ref[it]
    total = meta_ref[2]
    slot_acc = it & 1
    f = MOE_F
    half = f // 2

    def by_size(item, fn):
        for v, sub in enumerate(MOE_SUBS):
            @pl.when(isz_ref[item] == v)
            def _():
                fn(sub)

    def x_copy(item_start, s, slot, sub):
        r = pl.multiple_of(item_start + s * sub, SUBLANES)
        return pltpu.make_async_copy(xs_ref.at[pl.ds(r, sub)], xst_ref.at[slot, pl.ds(0, sub)], xsem.at[slot])

    def x_prefetch(item, sub):
        for s in range(2):
            @pl.when(s < ins_ref[item])
            def _():
                x_copy(ist_ref[item], s, s, sub).start()

    def y_copies(item, start_dma):
        def sized(sub):
            def one(s, c):
                r = pl.multiple_of(ist_ref[item] + s * sub, SUBLANES)
                cp = pltpu.make_async_copy(acc_ref.at[item & 1, pl.ds(s * sub, sub)],
                                           ys_ref.at[pl.ds(r, sub)], osem.at[item & 1])
                cp.start() if start_dma else cp.wait()
                return c
            lax.fori_loop(0, ins_ref[item], one, 0)
        by_size(item, sized)

    def w_copies(item, j, slot):
        e = ie_ref[item]
        cg = pl.multiple_of(j * 2 * f, 2 * f)
        cd = pl.multiple_of(j * f, f)
        return (pltpu.make_async_copy(wgu_hbm.at[e, :, pl.ds(cg, 2 * f)], rawg_ref[slot], gsem.at[slot]),
                pltpu.make_async_copy(wd_hbm.at[e, pl.ds(cd, f), :], rawd_ref[slot], dsem.at[slot]))

    def interleave_down(slot):
        for c in range(D_MODEL // LANES):
            cs = slice(c * LANES, (c + 1) * LANES)
            wdm_ref[slot][c, pl.ds(0, half, stride=2), :] = rawd_ref[slot][:half, cs]
            wdm_ref[slot][c, pl.ds(1, half, stride=2), :] = rawd_ref[slot][half:, cs]

    @pl.when(it == 0)
    def _():
        for c in range(2):
            for cp in w_copies(0, c, c):
                cp.start()
        for cp in w_copies(0, 0, 0):
            cp.wait()
        interleave_down(0)
        by_size(0, lambda sub: x_prefetch(0, sub))

    def item_body(sub):
        def stage(s, c):
            slot = s & 1
            r = pl.multiple_of(s * sub, ROW_PACK)
            x_copy(start, s, slot, sub).wait()
            xbf_ref[pl.ds(r, sub), :] = xst_ref[slot, :sub, :].astype(BF16)
            acc_ref[slot_acc, pl.ds(r, sub), :] = jnp.broadcast_to(bd_ref[0], (sub, D_MODEL))

            @pl.when(s + 2 < nsub)
            def _():
                x_copy(start, s + 2, slot, sub).start()
            return c
        lax.fori_loop(0, nsub, stage, 0)

        nxt = jnp.minimum(it + 1, n_items - 1)

        @pl.when(it + 1 < n_items)
        def _():
            by_size(nxt, lambda nsz: x_prefetch(nxt, nsz))

        even = (lax.broadcasted_iota(I32, (sub, f), 1) & 1) == 0

        def up_first(h_ref, w):
            wbf = rawg_ref[w][...].astype(BF16)
            wgu_bf_ref[w][...] = wbf
            h_ref[:sub, :] = jnp.dot(xbf_ref[pl.ds(0, sub), :], wbf, preferred_element_type=F32)

        def up_proj(s, h_ref, w):
            r = pl.multiple_of(s * sub, ROW_PACK)
            h_ref[:sub, :] = jnp.dot(xbf_ref[pl.ds(r, sub), :], wgu_bf_ref[w][...], preferred_element_type=F32)

        def down_proj(s, h_ref, w, j):
            r = pl.multiple_of(s * sub, ROW_PACK)
            h = h_ref[:sub, :] + bgu_ref[0, pl.ds(j, 1), :]
            gate = jnp.minimum(h, SWIGLU_LIMIT)
            glu = gate * jax.nn.sigmoid(gate * SWIGLU_ALPHA)
            up = jnp.clip(h, -SWIGLU_LIMIT, SWIGLU_LIMIT) + 1.0
            prod = glu * pltpu.roll(up, 2 * f - 1, axis=1)
            act = jnp.where(even, prod[:, :f], pltpu.roll(prod, 1, axis=1)[:, f:])
            wd = jnp.concatenate([wdm_ref[w][c] for c in range(D_MODEL // LANES)], axis=1).astype(BF16)
            acc_ref[slot_acc, pl.ds(r, sub), :] += jnp.dot(act.astype(BF16), wd, preferred_element_type=F32)

        def chunk_pass(j, w, can_be_last):
            j2 = j + 2
            it2 = it + jnp.where(j2 >= n_chunks, 1, 0)

            @pl.when(it2 < total)
            def _():
                for cp in w_copies(jnp.minimum(it2, n_items - 1), j2 & (n_chunks - 1), w):
                    cp.start()

            j1 = j + 1
            it1 = it + jnp.where(j1 >= n_chunks, 1, 0)

            @pl.when(it1 < total)
            def _():
                for cp in w_copies(jnp.minimum(it1, n_items - 1), j1 & (n_chunks - 1), 1 - w):
                    cp.wait()

            interleave_down(1 - w)

            ha_ref, hb_ref = h_refs[w]
            ha_next = h_refs[1 - w][0]
            n_pairs = (nsub - 1) // 2

            def pair(i, c):
                s = 2 * i
                up_proj(s + 1, hb_ref, w)
                down_proj(s, ha_ref, w, j)
                up_proj(s + 2, ha_ref, w)
                down_proj(s + 1, hb_ref, w, j)
                return c
            lax.fori_loop(0, n_pairs, pair, 0)
            s = 2 * n_pairs

            def tail(with_next):
                @pl.when(s == nsub - 1)
                def _():
                    if with_next:
                        up_first(ha_next, 1 - w)
                    down_proj(s, ha_ref, w, j)

                @pl.when(s < nsub - 1)
                def _():
                    up_proj(s + 1, hb_ref, w)
                    down_proj(s, ha_ref, w, j)
                    if with_next:
                        up_first(ha_next, 1 - w)
                    down_proj(s + 1, hb_ref, w, j)

            if can_be_last:
                @pl.when(j < n_chunks - 1)
                def _():
                    tail(True)

                @pl.when(j == n_chunks - 1)
                def _():
                    tail(False)
            else:
                tail(True)

        up_first(h_refs[0][0], 0)

        def chunk_pair(p, c):
            chunk_pass(2 * p, 0, False)
            chunk_pass(2 * p + 1, 1, True)
            return c
        lax.fori_loop(0, n_chunks // 2, chunk_pair, 0)

    @pl.when(nsub > 0)
    def _():
        by_size(it, item_body)

    @pl.when(it > 0)
    def _():
        y_copies(it - 1, False)

    y_copies(it, True)

    @pl.when(it == n_items - 1)
    def _():
        y_copies(it, False)

        zero_ref[...] = jnp.zeros_like(zero_ref)

        def fill(start_dma):
            def per_tile(r, c):
                cp = _tile_copy(zero_ref, ys_ref, meta_ref[0] + r, zsem)
                cp.start() if start_dma else cp.wait()
                return c
            lax.fori_loop(0, meta_ref[1], per_tile, 0)
        fill(True)
        fill(False)


def _moe(item_e, item_start, item_nsub, item_size, meta, xs, w_gate_up, b_gate_up, w_down, b_down):
    n_items = item_e.shape[0]
    n_rows = xs.shape[0]
    n_chunks = D_FF // MOE_F
    grid_spec = pltpu.PrefetchScalarGridSpec(
        num_scalar_prefetch=5,
        grid=(n_items,),
        in_specs=[
            pl.BlockSpec(memory_space=pl.ANY),
            pl.BlockSpec(memory_space=pl.ANY),
            pl.BlockSpec((1, n_chunks, 2 * MOE_F), lambda it, ie, ist, ins, isz, mt: (ie[it], 0, 0)),
            pl.BlockSpec(memory_space=pl.ANY),
            pl.BlockSpec((1, 1, D_MODEL), lambda it, ie, ist, ins, isz, mt: (ie[it], 0, 0)),
        ],
        out_specs=pl.BlockSpec(memory_space=pl.ANY),
        scratch_shapes=[
            pltpu.VMEM((D_MODEL, 2 * MOE_F), F32), pltpu.VMEM((D_MODEL, 2 * MOE_F), F32),
            pltpu.VMEM((D_MODEL, 2 * MOE_F), BF16), pltpu.VMEM((D_MODEL, 2 * MOE_F), BF16),
            pltpu.VMEM((MOE_F, D_MODEL), F32), pltpu.VMEM((MOE_F, D_MODEL), F32),
            pltpu.VMEM((D_MODEL // LANES, MOE_F, LANES), F32), pltpu.VMEM((D_MODEL // LANES, MOE_F, LANES), F32),
            pltpu.VMEM((2, MOE_BM, D_MODEL), F32),
            pltpu.VMEM((MOE_BM, D_MODEL), BF16),
            pltpu.VMEM((2, MOE_SUB, D_MODEL), F32),
            pltpu.VMEM((MOE_SUB, 2 * MOE_F), F32), pltpu.VMEM((MOE_SUB, 2 * MOE_F), F32),
            pltpu.VMEM((MOE_SUB, 2 * MOE_F), F32), pltpu.VMEM((MOE_SUB, 2 * MOE_F), F32),
            pltpu.VMEM((SUBLANES, D_MODEL), F32),
            pltpu.SemaphoreType.DMA((2,)),
            pltpu.SemaphoreType.DMA((2,)),
            pltpu.SemaphoreType.DMA((2,)),
            pltpu.SemaphoreType.DMA((2,)),
            pltpu.SemaphoreType.DMA,
        ],
    )
    return pl.pallas_call(
        functools.partial(_moe_kernel, n_items=n_items, n_chunks=n_chunks),
        grid_spec=grid_spec,
        out_shape=jax.ShapeDtypeStruct((n_rows, D_MODEL), F32),
        compiler_params=_params("arbitrary"),
        name="moe",
    )(item_e, item_start, item_nsub, item_size, meta, xs, w_gate_up,
      b_gate_up.reshape(N_EXPERTS, n_chunks, 2 * MOE_F), w_down, b_down.reshape(N_EXPERTS, 1, D_MODEL))


def _combine_kernel(dest_ref, ys_ref, h_ref, gate_ref, p_ref, wple_ref, pg_ref, wpg_ref, o_ref, ybuf_ref, sem, *, t):
    i = pl.program_id(0)
    n = pl.num_programs(0)
    tb = ROW_TB
    slot = i & 1

    def gather(blk, sl, start):
        def per_group(g, c):
            for u in range(SUBLANES):
                for k in range(TOP_K):
                    d = dest_ref[k * t + blk * tb + g * SUBLANES + u]
                    cp = _row_copy(ys_ref, d, ybuf_ref.at[sl, k, g], u, sem.at[sl])
                    cp.start() if start else cp.wait()
            return c
        lax.fori_loop(0, tb // SUBLANES, per_group, 0)

    def on_slot(cond, blk, rel, start):
        for sl in range(2):
            @pl.when(cond & (slot == (sl ^ rel)))
            def _():
                gather(blk, sl, start)

    on_slot(i == 0, 0, 0, True)
    on_slot(i + 1 < n, i + 1, 1, True)
    on_slot(i >= 0, i, 0, False)

    gates = gate_ref[...]
    h2 = h_ref[...]
    for k in range(TOP_K):
        h2 = h2 + ybuf_ref[slot, k].reshape(tb, D_MODEL) * gates[:, k:k + 1]
    gate = jax.nn.sigmoid(jnp.dot(_rms(h2, pg_ref[...]).astype(BF16), wpg_ref[...].astype(BF16),
                                  preferred_element_type=F32))
    o_ref[...] = h2 + jnp.dot(p_ref[...].astype(BF16), wple_ref[...].astype(BF16),
                              preferred_element_type=F32) * gate


def _combine(dest_flat, ys, h, gates_tk, p2, w_ple, ple_norm_g, w_pg):
    t = h.shape[0]
    tb = ROW_TB
    grid_spec = pltpu.PrefetchScalarGridSpec(
        num_scalar_prefetch=1,
        grid=(t // tb,),
        in_specs=[
            pl.BlockSpec(memory_space=pl.ANY),
            pl.BlockSpec((tb, D_MODEL), lambda i, d: (i, 0)),
            pl.BlockSpec((tb, TOP_K), lambda i, d: (i, 0)),
            pl.BlockSpec((tb, D_PLE), lambda i, d: (i, 0)),
            pl.BlockSpec((D_PLE, D_MODEL), lambda i, d: (0, 0), pipeline_mode=pl.Buffered(1)),
            pl.BlockSpec((1, D_MODEL), lambda i, d: (0, 0)),
            pl.BlockSpec((D_MODEL, D_MODEL), lambda i, d: (0, 0), pipeline_mode=pl.Buffered(1)),
        ],
        out_specs=pl.BlockSpec((tb, D_MODEL), lambda i, d: (i, 0)),
        scratch_shapes=[pltpu.VMEM((2, TOP_K, tb // SUBLANES, SUBLANES, D_MODEL), F32),
                        pltpu.SemaphoreType.DMA((2,))],
    )
    return pl.pallas_call(
        functools.partial(_combine_kernel, t=t),
        grid_spec=grid_spec,
        out_shape=jax.ShapeDtypeStruct((t, D_MODEL), F32),
        compiler_params=_params("arbitrary"),
        name="combine",
    )(dest_flat, ys, h, gates_tk, p2, w_ple, ple_norm_g.reshape(1, D_MODEL), w_pg)


def _plan(idx_t, rank_t, counts, t):
    tk = t * TOP_K
    n_items = tk // MOE_BM + N_EXPERTS
    n_rows = tk + N_EXPERTS * ROW_ALIGN + MOE_SUB
    seg = (counts + ROW_ALIGN - 1) // ROW_ALIGN * ROW_ALIGN
    seg_end = jnp.cumsum(seg)
    seg_start = seg_end - seg
    eids = jnp.arange(N_EXPERTS, dtype=I32)[:, None, None]
    first = jnp.sum(jnp.where(idx_t[None] == eids, seg_start[:, None, None], 0), axis=0)
    dest = (first + rank_t).reshape(-1).astype(I32)

    per_e = (counts + MOE_BM - 1) // MOE_BM
    it_end = jnp.cumsum(per_e)
    it_off = it_end - per_e
    total = it_end[-1]
    its = jnp.arange(n_items, dtype=I32)
    valid = its < total
    its_c = jnp.minimum(its, total - 1)
    e_of = jnp.minimum(jnp.sum(it_end[None, :] <= its_c[:, None], axis=1), N_EXPERTS - 1).astype(I32)
    pick = e_of[:, None] == jnp.arange(N_EXPERTS, dtype=I32)[None, :]
    of_item = lambda v: jnp.sum(jnp.where(pick, v[None, :], 0), axis=1)
    li = its_c - of_item(it_off)
    rows = jnp.clip(of_item(counts) - li * MOE_BM, 0, MOE_BM)
    nsub_by_size = [(rows + sz - 1) // sz for sz in MOE_SUBS]
    size_id = jnp.zeros_like(rows)
    for v in range(1, len(MOE_SUBS)):
        size_id = jnp.where(nsub_by_size[v] <= nsub_by_size[0], v, size_id)
    sub_rows = jnp.zeros_like(rows)
    for v, sz in enumerate(MOE_SUBS):
        sub_rows = jnp.where(size_id == v, sz, sub_rows)
    nsub = jnp.where(valid, (rows + sub_rows - 1) // sub_rows, 0).astype(I32)
    start = (of_item(seg_start) + li * MOE_BM).astype(I32)

    last = its == total - 1
    used_end = jnp.sum(jnp.where(last, start + sub_rows * nsub, 0))
    moe_meta = jnp.stack([used_end // SUBLANES, (n_rows - used_end) // SUBLANES, total]).astype(I32)
    zmeta = jnp.concatenate([
        seg_start + counts, seg - counts,
        jnp.stack([seg_end[-1] // SUBLANES, (n_rows - seg_end[-1]) // SUBLANES]),
    ]).astype(I32)
    return dest, zmeta, e_of, start, nsub, size_id.astype(I32), moe_meta, n_rows


def kernel(x, p, ln1_g, w_in, q_norm_g, k_norm_g, sinks, sg_ln_g, sg_ln_b, w_spatial, b_spatial, attn_out_g,
           sg_out_g, w_out, ln2_g, w_router, b_router, w_gate_up, b_gate_up, w_down, b_down, w_ple, ple_norm_g,
           w_ple_gate):
    bsz, s, d = x.shape
    assert bsz == 1 and d == D_MODEL and p.shape[0] == 1
    t = bsz * s
    h = x.reshape(t, d)
    li = 0
    proj = _in_proj(h, ln1_g[li], w_in[li])
    h, xn2 = _mixer(proj, h, sinks[li], q_norm_g[li], k_norm_g[li], sg_ln_g[li], sg_ln_b[li], w_spatial[li],
                    b_spatial[li], attn_out_g[li], sg_out_g[li], w_out[li].astype(BF16), ln2_g[li])
    idx_t, gate_t, rank_t, cnt = _router(xn2, w_router[li], b_router[li])
    dest, zmeta, item_e, item_start, item_nsub, item_size, moe_meta, n_rows = _plan(idx_t, rank_t, cnt[:, 0], t)
    xs = _dispatch(dest, zmeta, xn2, n_rows)
    ys = _moe(item_e, item_start, item_nsub, item_size, moe_meta, xs, w_gate_up[li], b_gate_up[li], w_down[li],
              b_down[li])
    out = _combine(dest, ys, h, gate_t.T, p[li].reshape(t, D_PLE), w_ple[li], ple_norm_g[li], w_ple_gate[li])
    return out.reshape(bsz, s, d)
```

```python
import functools

import jax
import jax.numpy as jnp
import numpy as np
from jax import lax
from jax.experimental import pallas as pl
from jax.experimental.pallas import tpu as pltpu

F32 = jnp.float32
BF16 = jnp.bfloat16
I32 = jnp.int32

D_MODEL = 2048
D_PLE = 256
N_Q_HEADS = 16
N_KV_HEADS = 4
HEAD_DIM = 64
WINDOW = 128
N_SG_HEADS = 8
SG_HEAD_DIM = 128
D_ATTN = N_Q_HEADS * HEAD_DIM
D_KV = N_KV_HEADS * HEAD_DIM
D_SG = N_SG_HEADS * SG_HEAD_DIM
D_MIX = D_ATTN + D_SG
D_IN = D_ATTN + 2 * D_KV + 2 * D_SG
N_EXPERTS = 32
TOP_K = 4
D_FF = D_MODEL
SWIGLU_LIMIT = 7.0
SWIGLU_ALPHA = 1.702
EPS = 1e-6
NEG_INF = -1e30

SUBLANES = 8
LANES = 128
MXU_DIM = 256
VMEM_LIMIT_BYTES = 56 * 1024 * 1024

IN_BM = 512
MIX_R = 512
ROUTE_TB = 512
ROW_TB = 256
MOE_SUB = 256
MOE_SUBS = (256, 224)
ROW_PACK = 16
MOE_BM = 1280
MOE_F = 256
ROW_ALIGN = SUBLANES


def _rms(x, g):
    ms = jnp.mean(x * x, axis=-1, keepdims=True)
    return x * lax.rsqrt(ms + EPS) * g


def _gelu(x):
    return 0.5 * x * (1.0 + lax.erf(x * np.float32(np.sqrt(0.5))))


def _params(*sem):
    return pltpu.CompilerParams(dimension_semantics=sem, vmem_limit_bytes=VMEM_LIMIT_BYTES)


def _in_proj_kernel(x_ref, g_ref, w_ref, o_ref):
    o_ref[...] = jnp.dot(_rms(x_ref[...], g_ref[...]).astype(BF16), w_ref[...].astype(BF16),
                         preferred_element_type=F32)


def _in_proj(x2, ln1_g, w_in):
    t = x2.shape[0]
    return pl.pallas_call(
        _in_proj_kernel,
        grid=(t // IN_BM,),
        in_specs=[
            pl.BlockSpec((IN_BM, D_MODEL), lambda i: (i, 0)),
            pl.BlockSpec((1, D_MODEL), lambda i: (0, 0)),
            pl.BlockSpec((D_MODEL, D_IN), lambda i: (0, 0), pipeline_mode=pl.Buffered(1)),
        ],
        out_specs=pl.BlockSpec((IN_BM, D_IN), lambda i: (i, 0)),
        out_shape=jax.ShapeDtypeStruct((t, D_IN), F32),
        compiler_params=_params("arbitrary"),
        name="in_proj",
    )(x2, ln1_g.reshape(1, D_MODEL), w_in)


def _head_mean_sq(t, bd):
    tt = t * t
    hi = tt.astype(BF16)
    lo = (tt - hi.astype(F32)).astype(BF16)
    w = bd.shape[0]
    parts = []
    for c in range(t.shape[1] // w):
        cs = slice(c * w, (c + 1) * w)
        parts.append(jnp.dot(hi[:, cs], bd, preferred_element_type=F32)
                     + jnp.dot(lo[:, cs], bd, preferred_element_type=F32))
    return jnp.concatenate(parts, axis=1) * (1.0 / HEAD_DIM)


def _mixer_kernel(sinks_ref, proj_ref, pkv_ref, x_ref, qg_ref, kg_ref, bd_ref, lng_ref, lnb_ref,
                  wsp_ref, bsp_ref, ag_ref, sgg_ref, wout_ref, ln2_ref,
                  h_ref, xn2_ref, a_ref, merged_ref):
    i = pl.program_id(0)
    nsb = MIX_R // WINDOW
    grp = N_Q_HEADS // N_KV_HEADS

    bd = bd_ref[...]
    kv_all = jnp.concatenate([pkv_ref[...], proj_ref[:, D_ATTN:D_ATTN + 2 * D_KV]], axis=0)
    k_all = kv_all[:, :D_KV]
    kn = (k_all * lax.rsqrt(_head_mean_sq(k_all, bd) + EPS) * kg_ref[...]).astype(BF16)
    vb = kv_all[:, D_KV:].astype(BF16)
    q_all = proj_ref[:, :D_ATTN]
    qn = (q_all * lax.rsqrt(_head_mean_sq(q_all, bd) + EPS) * qg_ref[...] * (HEAD_DIM ** -0.5)).astype(BF16)

    rows = lax.broadcasted_iota(I32, (grp * WINDOW, 2 * WINDOW), 0)
    cols = lax.broadcasted_iota(I32, (grp * WINDOW, 2 * WINDOW), 1)
    rel = cols - (rows & (WINDOW - 1))
    band = (rel >= 1) & (rel <= WINDOW)
    cur = cols >= WINDOW
    rblk = lax.broadcasted_iota(I32, (grp * WINDOW, 1), 0) // WINDOW

    tr = lax.broadcasted_iota(I32, (WINDOW, WINDOW), 0)
    tc = lax.broadcasted_iota(I32, (WINDOW, WINDOW), 1)
    wsp = [jnp.where(tc <= tr, wsp_ref[g], 0.0).astype(BF16) for g in range(N_SG_HEADS)]

    for sb in range(nsb):
        r0 = sb * WINDOW
        has_prev = (i * nsb + sb) > 0
        ok = band & (cur | has_prev)
        for g in range(N_KV_HEADS):
            kc = kn[r0:r0 + 2 * WINDOW, g * HEAD_DIM:(g + 1) * HEAD_DIM]
            vc = vb[r0:r0 + 2 * WINDOW, g * HEAD_DIM:(g + 1) * HEAD_DIM]
            qg = jnp.concatenate(
                [qn[r0:r0 + WINDOW, (g * grp + j) * HEAD_DIM:(g * grp + j + 1) * HEAD_DIM] for j in range(grp)],
                axis=0)
            s = lax.dot_general(qg, kc, (((1,), (1,)), ((), ())), preferred_element_type=F32)
            s = jnp.where(ok, s, NEG_INF)
            sink = jnp.full((grp * WINDOW, 1), sinks_ref[g * grp], F32)
            for j in range(1, grp):
                sink = jnp.where(rblk == j, sinks_ref[g * grp + j], sink)
            m = jnp.maximum(jnp.max(s, axis=-1, keepdims=True), sink)
            p = jnp.exp(s - m)
            den = jnp.sum(p, axis=-1, keepdims=True) + jnp.exp(sink - m)
            o = jnp.dot((p / den).astype(BF16), vc, preferred_element_type=F32)
            for j in range(grp):
                hh = g * grp + j
                a_ref[r0:r0 + WINDOW, hh * HEAD_DIM:(hh + 1) * HEAD_DIM] = o[j * WINDOW:(j + 1) * WINDOW]
        merged_ref[r0:r0 + WINDOW, :D_ATTN] = _rms(a_ref[r0:r0 + WINDOW, :], ag_ref[...]).astype(BF16)

        u = _gelu(proj_ref[r0:r0 + WINDOW, D_ATTN + 2 * D_KV:D_ATTN + 2 * D_KV + D_SG])
        vg = _gelu(proj_ref[r0:r0 + WINDOW, D_ATTN + 2 * D_KV + D_SG:])
        mu = jnp.mean(vg, axis=-1, keepdims=True)
        vc0 = vg - mu
        var = jnp.mean(vc0 * vc0, axis=-1, keepdims=True)
        vn = (vc0 * lax.rsqrt(var + EPS) * lng_ref[...] + lnb_ref[...]).astype(BF16)
        mixed = jnp.concatenate(
            [jnp.dot(wsp[g], vn[:, g * SG_HEAD_DIM:(g + 1) * SG_HEAD_DIM], preferred_element_type=F32)
             for g in range(N_SG_HEADS)], axis=1) + bsp_ref[...]
        merged_ref[r0:r0 + WINDOW, D_ATTN:] = _rms(u * mixed, sgg_ref[...]).astype(BF16)

        hblk = x_ref[r0:r0 + WINDOW, :] + jnp.dot(merged_ref[r0:r0 + WINDOW, :], wout_ref[...],
                                                  preferred_element_type=F32)
        h_ref[r0:r0 + WINDOW, :] = hblk
        xn2_ref[r0:r0 + WINDOW, :] = _rms(hblk, ln2_ref[...])


def _mixer(proj, x2, sinks, q_norm_g, k_norm_g, sg_ln_g, sg_ln_b, w_spatial, b_spatial,
           attn_out_g, sg_out_g, w_out_bf, ln2_g):
    t = x2.shape[0]
    nsb = MIX_R // WINDOW
    kvw = 2 * D_KV
    qg = jnp.tile(q_norm_g, N_Q_HEADS).reshape(1, D_ATTN)
    kg = jnp.tile(k_norm_g, N_KV_HEADS).reshape(1, D_KV)
    bd = jnp.asarray(np.kron(np.eye(MXU_DIM // HEAD_DIM), np.ones((HEAD_DIM, HEAD_DIM))), BF16)
    bsp = jnp.repeat(b_spatial.T, SG_HEAD_DIM, axis=1)
    row = lambda v, d: v.reshape(1, d)
    const = lambda shape: pl.BlockSpec(shape, lambda i, s: tuple(0 for _ in shape), pipeline_mode=pl.Buffered(1))
    grid_spec = pltpu.PrefetchScalarGridSpec(
        num_scalar_prefetch=1,
        grid=(t // MIX_R,),
        in_specs=[
            pl.BlockSpec((MIX_R, D_IN), lambda i, s: (i, 0)),
            pl.BlockSpec((WINDOW, kvw), lambda i, s: (jnp.maximum(i * nsb - 1, 0), D_ATTN // kvw)),
            pl.BlockSpec((MIX_R, D_MODEL), lambda i, s: (i, 0)),
            const((1, D_ATTN)), const((1, D_KV)), const((MXU_DIM, MXU_DIM)),
            const((1, D_SG)), const((1, D_SG)),
            const((N_SG_HEADS, WINDOW, WINDOW)), const((WINDOW, D_SG)),
            const((1, D_ATTN)), const((1, D_SG)),
            const((D_MIX, D_MODEL)), const((1, D_MODEL)),
        ],
        out_specs=[
            pl.BlockSpec((MIX_R, D_MODEL), lambda i, s: (i, 0)),
            pl.BlockSpec((MIX_R, D_MODEL), lambda i, s: (i, 0)),
        ],
        scratch_shapes=[pltpu.VMEM((MIX_R, D_ATTN), F32), pltpu.VMEM((MIX_R, D_MIX), BF16)],
    )
    return pl.pallas_call(
        _mixer_kernel,
        grid_spec=grid_spec,
        out_shape=[jax.ShapeDtypeStruct((t, D_MODEL), F32), jax.ShapeDtypeStruct((t, D_MODEL), F32)],
        compiler_params=_params("arbitrary"),
        name="mixer",
    )(sinks, proj, proj, x2, qg, kg, bd, row(sg_ln_g, D_SG), row(sg_ln_b, D_SG), w_spatial, bsp,
      row(attn_out_g, D_ATTN), row(sg_out_g, D_SG), w_out_bf, row(ln2_g, D_MODEL))


def _router_kernel(xn_ref, wr_ref, br_ref, idx_ref, gate_ref, rank_ref, cnt_ref, carry_ref):
    i = pl.program_id(0)
    tb = ROUTE_TB

    @pl.when(i == 0)
    def _():
        carry_ref[...] = jnp.zeros_like(carry_ref)

    def split(v):
        hi = v.astype(BF16)
        return hi, (v - hi.astype(F32)).astype(BF16)

    def nt(a, b):
        return lax.dot_general(a, b, (((1,), (1,)), ((), ())), preferred_element_type=F32)

    w_hi, w_lo = split(wr_ref[...])
    x_hi, x_lo = split(xn_ref[...])
    by_x_hi = nt(jnp.concatenate([w_hi, w_lo], axis=0), x_hi)
    logits = by_x_hi[:N_EXPERTS] + (nt(w_hi, x_lo) + by_x_hi[N_EXPERTS:]) + br_ref[...]
    eid = lax.broadcasted_iota(I32, (N_EXPERTS, tb), 0)
    work = logits
    vals, idxs = [], []
    for _ in range(TOP_K):
        m = jnp.max(work, axis=0, keepdims=True)
        sel = jnp.min(jnp.where(work == m, eid, N_EXPERTS), axis=0, keepdims=True)
        vals.append(m)
        idxs.append(sel)
        work = jnp.where(eid == sel, -jnp.inf, work)
    ex = [jnp.exp(v - vals[0]) for v in vals]
    den = ex[0] + ex[1] + ex[2] + ex[3]
    chosen = (eid == idxs[0]) | (eid == idxs[1]) | (eid == idxs[2]) | (eid == idxs[3])
    onehot = jnp.where(chosen, 1.0, 0.0)

    tr = lax.broadcasted_iota(I32, (tb, tb), 0)
    tc = lax.broadcasted_iota(I32, (tb, tb), 1)
    before = jnp.where(tr < tc, 1.0, 0.0).astype(BF16)
    cnt = jnp.dot(onehot.astype(BF16), before, preferred_element_type=F32) + carry_ref[:, :1]
    for k in range(TOP_K):
        idx_ref[k:k + 1, :] = idxs[k]
        gate_ref[k:k + 1, :] = ex[k] / den
        rank_ref[k:k + 1, :] = jnp.sum(jnp.where(eid == idxs[k], cnt, 0.0), axis=0, keepdims=True).astype(I32)
    carry_ref[...] = carry_ref[...] + jnp.sum(onehot, axis=1, keepdims=True)
    cnt_ref[...] = carry_ref[...].astype(I32)


def _router(xn2, w_router, b_router):
    t = xn2.shape[0]
    tb = ROUTE_TB
    return pl.pallas_call(
        _router_kernel,
        grid=(t // tb,),
        in_specs=[
            pl.BlockSpec((tb, D_MODEL), lambda i: (i, 0)),
            pl.BlockSpec((N_EXPERTS, D_MODEL), lambda i: (0, 0)),
            pl.BlockSpec((N_EXPERTS, 1), lambda i: (0, 0)),
        ],
        out_specs=[
            pl.BlockSpec((TOP_K, tb), lambda i: (0, i)),
            pl.BlockSpec((TOP_K, tb), lambda i: (0, i)),
            pl.BlockSpec((TOP_K, tb), lambda i: (0, i)),
            pl.BlockSpec((N_EXPERTS, LANES), lambda i: (0, 0)),
        ],
        out_shape=[
            jax.ShapeDtypeStruct((TOP_K, t), I32),
            jax.ShapeDtypeStruct((TOP_K, t), F32),
            jax.ShapeDtypeStruct((TOP_K, t), I32),
            jax.ShapeDtypeStruct((N_EXPERTS, LANES), I32),
        ],
        scratch_shapes=[pltpu.VMEM((N_EXPERTS, LANES), F32)],
        compiler_params=_params("arbitrary"),
        name="router",
    )(xn2, w_router.T, b_router.reshape(N_EXPERTS, 1))


def _row_copy(src, s, dst, d, sem):
    return pltpu.make_async_copy(src.at[pl.ds(s, 1)], dst.at[pl.ds(d, 1)], sem)


def _tile_copy(src, dst, d8, sem):
    return pltpu.make_async_copy(src, dst.at[pl.ds(pl.multiple_of(d8 * SUBLANES, SUBLANES), SUBLANES)], sem)


def _dispatch_kernel(dest_ref, zmeta_ref, xn_ref, xs_ref, zero_ref, sem, zsem, *, t):
    i = pl.program_id(0)
    tb = ROW_TB

    @pl.when(i == 0)
    def _():
        zero_ref[...] = jnp.zeros_like(zero_ref)

        def pads(start):
            def per_expert(e, c):
                def per_row(r, c2):
                    cp = _row_copy(zero_ref, 0, xs_ref, zmeta_ref[e] + r, zsem)
                    cp.start() if start else cp.wait()
                    return c2
                return lax.fori_loop(0, zmeta_ref[N_EXPERTS + e], per_row, c)
            lax.fori_loop(0, N_EXPERTS, per_expert, 0)

            def per_tile(r, c):
                cp = _tile_copy(zero_ref, xs_ref, zmeta_ref[2 * N_EXPERTS] + r, zsem)
                cp.start() if start else cp.wait()
                return c
            lax.fori_loop(0, zmeta_ref[2 * N_EXPERTS + 1], per_tile, 0)

        pads(True)
        pads(False)

    def rows(start):
        def per_group(g, c):
            for u in range(SUBLANES):
                for k in range(TOP_K):
                    d = dest_ref[k * t + i * tb + g * SUBLANES + u]
                    cp = _row_copy(xn_ref.at[g], u, xs_ref, d, sem)
                    cp.start(priority=k % 2) if start else cp.wait()
            return c
        lax.fori_loop(0, tb // SUBLANES, per_group, 0)
    rows(True)
    rows(False)


def _dispatch(dest_flat, zmeta, xn2, n_rows):
    t = xn2.shape[0]
    grid_spec = pltpu.PrefetchScalarGridSpec(
        num_scalar_prefetch=2,
        grid=(t // ROW_TB,),
        in_specs=[pl.BlockSpec((ROW_TB // SUBLANES, SUBLANES, D_MODEL), lambda i, d, z: (i, 0, 0))],
        out_specs=pl.BlockSpec(memory_space=pl.ANY),
        scratch_shapes=[pltpu.VMEM((SUBLANES, D_MODEL), F32), pltpu.SemaphoreType.DMA, pltpu.SemaphoreType.DMA],
    )
    return pl.pallas_call(
        functools.partial(_dispatch_kernel, t=t),
        grid_spec=grid_spec,
        out_shape=jax.ShapeDtypeStruct((n_rows, D_MODEL), F32),
        compiler_params=_params("arbitrary"),
        name="dispatch",
    )(dest_flat, zmeta, xn2.reshape(t // SUBLANES, SUBLANES, D_MODEL))


def _moe_kernel(ie_ref, ist_ref, ins_ref, isz_ref, meta_ref, xs_ref, wgu_hbm, bgu_ref, wd_hbm, bd_ref, ys_ref,
                rawg0_ref, rawg1_ref, wgu0_ref, wgu1_ref, rawd0_ref, rawd1_ref, wdm0_ref, wdm1_ref,
                acc_ref, xbf_ref, xst_ref, ha0_ref, hb0_ref, ha1_ref, hb1_ref,
                zero_ref, gsem, dsem, xsem, osem, zsem, *, n_items, n_chunks):
    h_refs = ((ha0_ref, hb0_ref), (ha1_ref, hb1_ref))
    rawg_ref, wgu_bf_ref = (rawg0_ref, rawg1_ref), (wgu0_ref, wgu1_ref)
    rawd_ref, wdm_ref = (rawd0_ref, rawd1_ref), (wdm0_ref, wdm1_ref)
    it = pl.program_id(0)
    nsub = ins_ref[it]
    start = ist_ref[it]
    total = meta_ref[2]
    slot_acc = it & 1
    f = MOE_F
    half = f // 2

    def by_size(item, fn):
        for v, sub in enumerate(MOE_SUBS):
            @pl.when(isz_ref[item] == v)
            def _():
                fn(sub)

    def x_copy(item_start, s, slot, sub):
        r = pl.multiple_of(item_start + s * sub, SUBLANES)
        return pltpu.make_async_copy(xs_ref.at[pl.ds(r, sub)], xst_ref.at[slot, pl.ds(0, sub)], xsem.at[slot])

    def x_prefetch(item, sub):
        for s in range(2):
            @pl.when(s < ins_ref[item])
            def _():
                x_copy(ist_ref[item], s, s, sub).start()

    def y_copies(item, start_dma):
        def sized(sub):
            def one(s, c):
                r = pl.multiple_of(ist_ref[item] + s * sub, SUBLANES)
                cp = pltpu.make_async_copy(acc_ref.at[item & 1, pl.ds(s * sub, sub)],
                                           ys_ref.at[pl.ds(r, sub)], osem.at[item & 1])
                cp.start() if start_dma else cp.wait()
                return c
            lax.fori_loop(0, ins_ref[item], one, 0)
        by_size(item, sized)

    def w_copies(item, j, slot):
        e = ie_ref[item]
        cg = pl.multiple_of(j * 2 * f, 2 * f)
        cd = pl.multiple_of(j * f, f)
        return (pltpu.make_async_copy(wgu_hbm.at[e, :, pl.ds(cg, 2 * f)], rawg_ref[slot], gsem.at[slot]),
                pltpu.make_async_copy(wd_hbm.at[e, pl.ds(cd, f), :], rawd_ref[slot], dsem.at[slot]))

    def interleave_down(slot):
        for c in range(D_MODEL // LANES):
            cs = slice(c * LANES, (c + 1) * LANES)
            wdm_ref[slot][c, pl.ds(0, half, stride=2), :] = rawd_ref[slot][:half, cs]
            wdm_ref[slot][c, pl.ds(1, half, stride=2), :] = rawd_ref[slot][half:, cs]

    @pl.when(it == 0)
    def _():
        for c in range(2):
            for cp in w_copies(0, c, c):
                cp.start()
        for cp in w_copies(0, 0, 0):
            cp.wait()
        interleave_down(0)
        by_size(0, lambda sub: x_prefetch(0, sub))

    def item_body(sub):
        def stage(s, c):
            slot = s & 1
            r = pl.multiple_of(s * sub, ROW_PACK)
            x_copy(start, s, slot, sub).wait()
            xbf_ref[pl.ds(r, sub), :] = xst_ref[slot, :sub, :].astype(BF16)
            acc_ref[slot_acc, pl.ds(r, sub), :] = jnp.broadcast_to(bd_ref[0], (sub, D_MODEL))

            @pl.when(s + 2 < nsub)
            def _():
                x_copy(start, s + 2, slot, sub).start()
            return c
        lax.fori_loop(0, nsub, stage, 0)

        nxt = jnp.minimum(it + 1, n_items - 1)

        @pl.when(it + 1 < n_items)
        def _():
            by_size(nxt, lambda nsz: x_prefetch(nxt, nsz))

        even = (lax.broadcasted_iota(I32, (sub, f), 1) & 1) == 0

        def up_first(h_ref, w):
            wbf = rawg_ref[w][...].astype(BF16)
            wgu_bf_ref[w][...] = wbf
            h_ref[:sub, :] = jnp.dot(xbf_ref[pl.ds(0, sub), :], wbf, preferred_element_type=F32)

        def up_proj(s, h_ref, w):
            r = pl.multiple_of(s * sub, ROW_PACK)
            h_ref[:sub, :] = jnp.dot(xbf_ref[pl.ds(r, sub), :], wgu_bf_ref[w][...], preferred_element_type=F32)

        def down_proj(s, h_ref, w, j):
            r = pl.multiple_of(s * sub, ROW_PACK)
            h = h_ref[:sub, :] + bgu_ref[0, pl.ds(j, 1), :]
            gate = jnp.minimum(h, SWIGLU_LIMIT)
            glu = gate * jax.nn.sigmoid(gate * SWIGLU_ALPHA)
            up = jnp.clip(h, -SWIGLU_LIMIT, SWIGLU_LIMIT) + 1.0
            prod = glu * pltpu.roll(up, 2 * f - 1, axis=1)
            act = jnp.where(even, prod[:, :f], pltpu.roll(prod, 1, axis=1)[:, f:])
            wd = jnp.concatenate([wdm_ref[w][c] for c in range(D_MODEL // LANES)], axis=1).astype(BF16)
            acc_ref[slot_acc, pl.ds(r, sub), :] += jnp.dot(act.astype(BF16), wd, preferred_element_type=F32)

        def chunk_pass(j, w, can_be_last):
            j2 = j + 2
            it2 = it + jnp.where(j2 >= n_chunks, 1, 0)

            @pl.when(it2 < total)
            def _():
                for cp in w_copies(jnp.minimum(it2, n_items - 1), j2 & (n_chunks - 1), w):
                    cp.start()

            j1 = j + 1
            it1 = it + jnp.where(j1 >= n_chunks, 1, 0)

            @pl.when(it1 < total)
            def _():
                for cp in w_copies(jnp.minimum(it1, n_items - 1), j1 & (n_chunks - 1), 1 - w):
                    cp.wait()

            interleave_down(1 - w)

            ha_ref, hb_ref = h_refs[w]
            ha_next = h_refs[1 - w][0]
            n_pairs = (nsub - 1) // 2

            def pair(i, c):
                s = 2 * i
                up_proj(s + 1, hb_ref, w)
                down_proj(s, ha_ref, w, j)
                up_proj(s + 2, ha_ref, w)
                down_proj(s + 1, hb_ref, w, j)
                return c
            lax.fori_loop(0, n_pairs, pair, 0)
            s = 2 * n_pairs

            def tail(with_next):
                @pl.when(s == nsub - 1)
                def _():
                    if with_next:
                        up_first(ha_next, 1 - w)
                    down_proj(s, ha_ref, w, j)

                @pl.when(s < nsub - 1)
                def _():
                    up_proj(s + 1, hb_ref, w)
                    down_proj(s, ha_ref, w, j)
                    if with_next:
                        up_first(ha_next, 1 - w)
                    down_proj(s + 1, hb_ref, w, j)

            if can_be_last:
                @pl.when(j < n_chunks - 1)
                def _():
                    tail(True)

                @pl.when(j == n_chunks - 1)
                def _():
                    tail(False)
            else:
                tail(True)

        up_first(h_refs[0][0], 0)

        def chunk_pair(p, c):
            chunk_pass(2 * p, 0, False)
            chunk_pass(2 * p + 1, 1, True)
            return c
        lax.fori_loop(0, n_chunks // 2, chunk_pair, 0)

    @pl.when(nsub > 0)
    def _():
        by_size(it, item_body)

    @pl.when(it > 0)
    def _():
        y_copies(it - 1, False)

    y_copies(it, True)

    @pl.when(it == n_items - 1)
    def _():
        y_copies(it, False)

        zero_ref[...] = jnp.zeros_like(zero_ref)

        def fill(start_dma):
            def per_tile(r, c):
                cp = _tile_copy(zero_ref, ys_ref, meta_ref[0] + r, zsem)
                cp.start() if start_dma else cp.wait()
                return c
            lax.fori_loop(0, meta_ref[1], per_tile, 0)
        fill(True)
        fill(False)


def _moe(item_e, item_start, item_nsub, item_size, meta, xs, w_gate_up, b_gate_up, w_down, b_down):
    n_items = item_e.shape[0]
    n_rows = xs.shape[0]
    n_chunks = D_FF // MOE_F
    grid_spec = pltpu.PrefetchScalarGridSpec(
        num_scalar_prefetch=5,
        grid=(n_items,),
        in_specs=[
            pl.BlockSpec(memory_space=pl.ANY),
            pl.BlockSpec(memory_space=pl.ANY),
            pl.BlockSpec((1, n_chunks, 2 * MOE_F), lambda it, ie, ist, ins, isz, mt: (ie[it], 0, 0)),
            pl.BlockSpec(memory_space=pl.ANY),
            pl.BlockSpec((1, 1, D_MODEL), lambda it, ie, ist, ins, isz, mt: (ie[it], 0, 0)),
        ],
        out_specs=pl.BlockSpec(memory_space=pl.ANY),
        scratch_shapes=[
            pltpu.VMEM((D_MODEL, 2 * MOE_F), F32), pltpu.VMEM((D_MODEL, 2 * MOE_F), F32),
            pltpu.VMEM((D_MODEL, 2 * MOE_F), BF16), pltpu.VMEM((D_MODEL, 2 * MOE_F), BF16),
            pltpu.VMEM((MOE_F, D_MODEL), F32), pltpu.VMEM((MOE_F, D_MODEL), F32),
            pltpu.VMEM((D_MODEL // LANES, MOE_F, LANES), F32), pltpu.VMEM((D_MODEL // LANES, MOE_F, LANES), F32),
            pltpu.VMEM((2, MOE_BM, D_MODEL), F32),
            pltpu.VMEM((MOE_BM, D_MODEL), BF16),
            pltpu.VMEM((2, MOE_SUB, D_MODEL), F32),
            pltpu.VMEM((MOE_SUB, 2 * MOE_F), F32), pltpu.VMEM((MOE_SUB, 2 * MOE_F), F32),
            pltpu.VMEM((MOE_SUB, 2 * MOE_F), F32), pltpu.VMEM((MOE_SUB, 2 * MOE_F), F32),
            pltpu.VMEM((SUBLANES, D_MODEL), F32),
            pltpu.SemaphoreType.DMA((2,)),
            pltpu.SemaphoreType.DMA((2,)),
            pltpu.SemaphoreType.DMA((2,)),
            pltpu.SemaphoreType.DMA((2,)),
            pltpu.SemaphoreType.DMA,
        ],
    )
    return pl.pallas_call(
        functools.partial(_moe_kernel, n_items=n_items, n_chunks=n_chunks),
        grid_spec=grid_spec,
        out_shape=jax.ShapeDtypeStruct((n_rows, D_MODEL), F32),
        compiler_params=_params("arbitrary"),
        name="moe",
    )(item_e, item_start, item_nsub, item_size, meta, xs, w_gate_up,
      b_gate_up.reshape(N_EXPERTS, n_chunks, 2 * MOE_F), w_down, b_down.reshape(N_EXPERTS, 1, D_MODEL))


def _combine_kernel(dest_ref, ys_ref, h_ref, gate_ref, p_ref, wple_ref, pg_ref, wpg_ref, o_ref, ybuf_ref, sem, *, t):
    i = pl.program_id(0)
    n = pl.num_programs(0)
    tb = ROW_TB
    slot = i & 1

    def gather(blk, sl, start):
        def per_group(g, c):
            for u in range(SUBLANES):
                for k in range(TOP_K):
                    d = dest_ref[k * t + blk * tb + g * SUBLANES + u]
                    cp = _row_copy(ys_ref, d, ybuf_ref.at[sl, k, g], u, sem.at[sl])
                    cp.start() if start else cp.wait()
            return c
        lax.fori_loop(0, tb // SUBLANES, per_group, 0)

    def on_slot(cond, blk, rel, start):
        for sl in range(2):
            @pl.when(cond & (slot == (sl ^ rel)))
            def _():
                gather(blk, sl, start)

    on_slot(i == 0, 0, 0, True)
    on_slot(i + 1 < n, i + 1, 1, True)
    on_slot(i >= 0, i, 0, False)

    gates = gate_ref[...]
    h2 = h_ref[...]
    for k in range(TOP_K):
        h2 = h2 + ybuf_ref[slot, k].reshape(tb, D_MODEL) * gates[:, k:k + 1]
    gate = jax.nn.sigmoid(jnp.dot(_rms(h2, pg_ref[...]).astype(BF16), wpg_ref[...].astype(BF16),
                                  preferred_element_type=F32))
    o_ref[...] = h2 + jnp.dot(p_ref[...].astype(BF16), wple_ref[...].astype(BF16),
                              preferred_element_type=F32) * gate


def _combine(dest_flat, ys, h, gates_tk, p2, w_ple, ple_norm_g, w_pg):
    t = h.shape[0]
    tb = ROW_TB
    grid_spec = pltpu.PrefetchScalarGridSpec(
        num_scalar_prefetch=1,
        grid=(t // tb,),
        in_specs=[
            pl.BlockSpec(memory_space=pl.ANY),
            pl.BlockSpec((tb, D_MODEL), lambda i, d: (i, 0)),
            pl.BlockSpec((tb, TOP_K), lambda i, d: (i, 0)),
            pl.BlockSpec((tb, D_PLE), lambda i, d: (i, 0)),
            pl.BlockSpec((D_PLE, D_MODEL), lambda i, d: (0, 0), pipeline_mode=pl.Buffered(1)),
            pl.BlockSpec((1, D_MODEL), lambda i, d: (0, 0)),
            pl.BlockSpec((D_MODEL, D_MODEL), lambda i, d: (0, 0), pipeline_mode=pl.Buffered(1)),
        ],
        out_specs=pl.BlockSpec((tb, D_MODEL), lambda i, d: (i, 0)),
        scratch_shapes=[pltpu.VMEM((2, TOP_K, tb // SUBLANES, SUBLANES, D_MODEL), F32),
                        pltpu.SemaphoreType.DMA((2,))],
    )
    return pl.pallas_call(
        functools.partial(_combine_kernel, t=t),
        grid_spec=grid_spec,
        out_shape=jax.ShapeDtypeStruct((t, D_MODEL), F32),
        compiler_params=_params("arbitrary"),
        name="combine",
    )(dest_flat, ys, h, gates_tk, p2, w_ple, ple_norm_g.reshape(1, D_MODEL), w_pg)


def _plan(idx_t, rank_t, counts, t):
    tk = t * TOP_K
    n_items = tk // MOE_BM + N_EXPERTS
    n_rows = tk + N_EXPERTS * ROW_ALIGN + MOE_SUB
    seg = (counts + ROW_ALIGN - 1) // ROW_ALIGN * ROW_ALIGN
    seg_end = jnp.cumsum(seg)
    seg_start = seg_end - seg
    eids = jnp.arange(N_EXPERTS, dtype=I32)[:, None, None]
    first = jnp.sum(jnp.where(idx_t[None] == eids, seg_start[:, None, None], 0), axis=0)
    dest = (first + rank_t).reshape(-1).astype(I32)

    per_e = (counts + MOE_BM - 1) // MOE_BM
    it_end = jnp.cumsum(per_e)
    it_off = it_end - per_e
    total = it_end[-1]
    its = jnp.arange(n_items, dtype=I32)
    valid = its < total
    its_c = jnp.minimum(its, total - 1)
    e_of = jnp.minimum(jnp.sum(it_end[None, :] <= its_c[:, None], axis=1), N_EXPERTS - 1).astype(I32)
    pick = e_of[:, None] == jnp.arange(N_EXPERTS, dtype=I32)[None, :]
    of_item = lambda v: jnp.sum(jnp.where(pick, v[None, :], 0), axis=1)
    li = its_c - of_item(it_off)
    rows = jnp.clip(of_item(counts) - li * MOE_BM, 0, MOE_BM)
    nsub_by_size = [(rows + sz - 1) // sz for sz in MOE_SUBS]
    size_id = jnp.zeros_like(rows)
    for v in range(1, len(MOE_SUBS)):
        size_id = jnp.where(nsub_by_size[v] <= nsub_by_size[0], v, size_id)
    sub_rows = jnp.zeros_like(rows)
    for v, sz in enumerate(MOE_SUBS):
        sub_rows = jnp.where(size_id == v, sz, sub_rows)
    nsub = jnp.where(valid, (rows + sub_rows - 1) // sub_rows, 0).astype(I32)
    start = (of_item(seg_start) + li * MOE_BM).astype(I32)

    last = its == total - 1
    used_end = jnp.sum(jnp.where(last, start + sub_rows * nsub, 0))
    moe_meta = jnp.stack([used_end // SUBLANES, (n_rows - used_end) // SUBLANES, total]).astype(I32)
    zmeta = jnp.concatenate([
        seg_start + counts, seg - counts,
        jnp.stack([seg_end[-1] // SUBLANES, (n_rows - seg_end[-1]) // SUBLANES]),
    ]).astype(I32)
    return dest, zmeta, e_of, start, nsub, size_id.astype(I32), moe_meta, n_rows


def kernel(x, p, ln1_g, w_in, q_norm_g, k_norm_g, sinks, sg_ln_g, sg_ln_b, w_spatial, b_spatial, attn_out_g,
           sg_out_g, w_out, ln2_g, w_router, b_router, w_gate_up, b_gate_up, w_down, b_down, w_ple, ple_norm_g,
           w_ple_gate):
    bsz, s, d = x.shape
    assert bsz == 1 and d == D_MODEL and p.shape[0] == 1
    t = bsz * s
    h = x.reshape(t, d)
    li = 0
    proj = _in_proj(h, ln1_g[li], w_in[li])
    h, xn2 = _mixer(proj, h, sinks[li], q_norm_g[li], k_norm_g[li], sg_ln_g[li], sg_ln_b[li], w_spatial[li],
                    b_spatial[li], attn_out_g[li], sg_out_g[li], w_out[li].astype(BF16), ln2_g[li])
    idx_t, gate_t, rank_t, cnt = _router(xn2, w_router[li], b_router[li])
    dest, zmeta, item_e, item_start, item_nsub, item_size, moe_meta, n_rows = _plan(idx_t, rank_t, cnt[:, 0], t)
    xs = _dispatch(dest, zmeta, xn2, n_rows)
    ys = _moe(item_e, item_start, item_nsub, item_size, moe_meta, xs, w_gate_up[li], b_gate_up[li], w_down[li],
              b_down[li])
    out = _combine(dest, ys, h, gate_t.T, p[li].reshape(t, D_PLE), w_ple[li], ple_norm_g[li], w_ple_gate[li])
    return out.reshape(bsz, s, d)
```

```python
import functools

import jax
import jax.numpy as jnp
import numpy as np
from jax import lax
from jax.experimental import pallas as pl
from jax.experimental.pallas import tpu as pltpu

F32 = jnp.float32
BF16 = jnp.bfloat16
I32 = jnp.int32

D_MODEL = 2048
D_PLE = 256
N_Q_HEADS = 16
N_KV_HEADS = 4
HEAD_DIM = 64
WINDOW = 128
N_SG_HEADS = 8
SG_HEAD_DIM = 128
D_ATTN = N_Q_HEADS * HEAD_DIM
D_KV = N_KV_HEADS * HEAD_DIM
D_SG = N_SG_HEADS * SG_HEAD_DIM
D_MIX = D_ATTN + D_SG
D_IN = D_ATTN + 2 * D_KV + 2 * D_SG
N_EXPERTS = 32
TOP_K = 4
D_FF = D_MODEL
SWIGLU_LIMIT = 7.0
SWIGLU_ALPHA = 1.702
EPS = 1e-6
NEG_INF = -1e30

SUBLANES = 8
LANES = 128
MXU_DIM = 256
VMEM_LIMIT_BYTES = 56 * 1024 * 1024

IN_BM = 512
MIX_R = 512
ROUTE_TB = 512
ROW_TB = 256
MOE_SUB = 256
MOE_SUBS = (256, 224, 208)
ROW_PACK = 16
MOE_BM = 1280
MOE_F = 256
ROW_ALIGN = SUBLANES


def _rms(x, g):
    ms = jnp.mean(x * x, axis=-1, keepdims=True)
    return x * lax.rsqrt(ms + EPS) * g


def _gelu(x):
    return 0.5 * x * (1.0 + lax.erf(x * np.float32(np.sqrt(0.5))))


def _params(*sem):
    return pltpu.CompilerParams(dimension_semantics=sem, vmem_limit_bytes=VMEM_LIMIT_BYTES)


def _in_proj_kernel(x_ref, g_ref, w_ref, o_ref):
    o_ref[...] = jnp.dot(_rms(x_ref[...], g_ref[...]).astype(BF16), w_ref[...].astype(BF16),
                         preferred_element_type=F32)


def _in_proj(x2, ln1_g, w_in):
    t = x2.shape[0]
    return pl.pallas_call(
        _in_proj_kernel,
        grid=(t // IN_BM,),
        in_specs=[
            pl.BlockSpec((IN_BM, D_MODEL), lambda i: (i, 0)),
            pl.BlockSpec((1, D_MODEL), lambda i: (0, 0)),
            pl.BlockSpec((D_MODEL, D_IN), lambda i: (0, 0), pipeline_mode=pl.Buffered(1)),
        ],
        out_specs=pl.BlockSpec((IN_BM, D_IN), lambda i: (i, 0)),
        out_shape=jax.ShapeDtypeStruct((t, D_IN), F32),
        compiler_params=_params("arbitrary"),
        name="in_proj",
    )(x2, ln1_g.reshape(1, D_MODEL), w_in)


def _head_mean_sq(t, bd):
    tt = t * t
    hi = tt.astype(BF16)
    lo = (tt - hi.astype(F32)).astype(BF16)
    w = bd.shape[0]
    parts = []
    for c in range(t.shape[1] // w):
        cs = slice(c * w, (c + 1) * w)
        parts.append(jnp.dot(hi[:, cs], bd, preferred_element_type=F32)
                     + jnp.dot(lo[:, cs], bd, preferred_element_type=F32))
    return jnp.concatenate(parts, axis=1) * (1.0 / HEAD_DIM)


def _mixer_kernel(sinks_ref, proj_ref, pkv_ref, x_ref, qg_ref, kg_ref, bd_ref, lng_ref, lnb_ref,
                  wsp_ref, bsp_ref, ag_ref, sgg_ref, wout_ref, ln2_ref,
                  h_ref, xn2_ref, a_ref, merged_ref):
    i = pl.program_id(0)
    nsb = MIX_R // WINDOW
    grp = N_Q_HEADS // N_KV_HEADS

    bd = bd_ref[...]
    kv_all = jnp.concatenate([pkv_ref[...], proj_ref[:, D_ATTN:D_ATTN + 2 * D_KV]], axis=0)
    k_all = kv_all[:, :D_KV]
    kn = (k_all * lax.rsqrt(_head_mean_sq(k_all, bd) + EPS) * kg_ref[...]).astype(BF16)
    vb = kv_all[:, D_KV:].astype(BF16)
    q_all = proj_ref[:, :D_ATTN]
    qn = (q_all * lax.rsqrt(_head_mean_sq(q_all, bd) + EPS) * qg_ref[...] * (HEAD_DIM ** -0.5)).astype(BF16)

    rows = lax.broadcasted_iota(I32, (grp * WINDOW, 2 * WINDOW), 0)
    cols = lax.broadcasted_iota(I32, (grp * WINDOW, 2 * WINDOW), 1)
    rel = cols - (rows & (WINDOW - 1))
    band = (rel >= 1) & (rel <= WINDOW)
    cur = cols >= WINDOW
    rblk = lax.broadcasted_iota(I32, (grp * WINDOW, 1), 0) // WINDOW

    tr = lax.broadcasted_iota(I32, (WINDOW, WINDOW), 0)
    tc = lax.broadcasted_iota(I32, (WINDOW, WINDOW), 1)
    wsp = [jnp.where(tc <= tr, wsp_ref[g], 0.0).astype(BF16) for g in range(N_SG_HEADS)]

    for sb in range(nsb):
        r0 = sb * WINDOW
        has_prev = (i * nsb + sb) > 0
        ok = band & (cur | has_prev)
        for g in range(N_KV_HEADS):
            kc = kn[r0:r0 + 2 * WINDOW, g * HEAD_DIM:(g + 1) * HEAD_DIM]
            vc = vb[r0:r0 + 2 * WINDOW, g * HEAD_DIM:(g + 1) * HEAD_DIM]
            qg = jnp.concatenate(
                [qn[r0:r0 + WINDOW, (g * grp + j) * HEAD_DIM:(g * grp + j + 1) * HEAD_DIM] for j in range(grp)],
                axis=0)
            s = lax.dot_general(qg, kc, (((1,), (1,)), ((), ())), preferred_element_type=F32)
            s = jnp.where(ok, s, NEG_INF)
            sink = jnp.full((grp * WINDOW, 1), sinks_ref[g * grp], F32)
            for j in range(1, grp):
                sink = jnp.where(rblk == j, sinks_ref[g * grp + j], sink)
            m = jnp.maximum(jnp.max(s, axis=-1, keepdims=True), sink)
            p = jnp.exp(s - m)
            den = jnp.sum(p, axis=-1, keepdims=True) + jnp.exp(sink - m)
            o = jnp.dot((p / den).astype(BF16), vc, preferred_element_type=F32)
            for j in range(grp):
                hh = g * grp + j
                a_ref[r0:r0 + WINDOW, hh * HEAD_DIM:(hh + 1) * HEAD_DIM] = o[j * WINDOW:(j + 1) * WINDOW]
        merged_ref[r0:r0 + WINDOW, :D_ATTN] = _rms(a_ref[r0:r0 + WINDOW, :], ag_ref[...]).astype(BF16)

        u = _gelu(proj_ref[r0:r0 + WINDOW, D_ATTN + 2 * D_KV:D_ATTN + 2 * D_KV + D_SG])
        vg = _gelu(proj_ref[r0:r0 + WINDOW, D_ATTN + 2 * D_KV + D_SG:])
        mu = jnp.mean(vg, axis=-1, keepdims=True)
        vc0 = vg - mu
        var = jnp.mean(vc0 * vc0, axis=-1, keepdims=True)
        vn = (vc0 * lax.rsqrt(var + EPS) * lng_ref[...] + lnb_ref[...]).astype(BF16)
        mixed = jnp.concatenate(
            [jnp.dot(wsp[g], vn[:, g * SG_HEAD_DIM:(g + 1) * SG_HEAD_DIM], preferred_element_type=F32)
             for g in range(N_SG_HEADS)], axis=1) + bsp_ref[...]
        merged_ref[r0:r0 + WINDOW, D_ATTN:] = _rms(u * mixed, sgg_ref[...]).astype(BF16)

        hblk = x_ref[r0:r0 + WINDOW, :] + jnp.dot(merged_ref[r0:r0 + WINDOW, :], wout_ref[...],
                                                  preferred_element_type=F32)
        h_ref[r0:r0 + WINDOW, :] = hblk
        xn2_ref[r0:r0 + WINDOW, :] = _rms(hblk, ln2_ref[...])


def _mixer(proj, x2, sinks, q_norm_g, k_norm_g, sg_ln_g, sg_ln_b, w_spatial, b_spatial,
           attn_out_g, sg_out_g, w_out_bf, ln2_g):
    t = x2.shape[0]
    nsb = MIX_R // WINDOW
    kvw = 2 * D_KV
    qg = jnp.tile(q_norm_g, N_Q_HEADS).reshape(1, D_ATTN)
    kg = jnp.tile(k_norm_g, N_KV_HEADS).reshape(1, D_KV)
    bd = jnp.asarray(np.kron(np.eye(MXU_DIM // HEAD_DIM), np.ones((HEAD_DIM, HEAD_DIM))), BF16)
    bsp = jnp.repeat(b_spatial.T, SG_HEAD_DIM, axis=1)
    row = lambda v, d: v.reshape(1, d)
    const = lambda shape: pl.BlockSpec(shape, lambda i, s: tuple(0 for _ in shape), pipeline_mode=pl.Buffered(1))
    grid_spec = pltpu.PrefetchScalarGridSpec(
        num_scalar_prefetch=1,
        grid=(t // MIX_R,),
        in_specs=[
            pl.BlockSpec((MIX_R, D_IN), lambda i, s: (i, 0)),
            pl.BlockSpec((WINDOW, kvw), lambda i, s: (jnp.maximum(i * nsb - 1, 0), D_ATTN // kvw)),
            pl.BlockSpec((MIX_R, D_MODEL), lambda i, s: (i, 0)),
            const((1, D_ATTN)), const((1, D_KV)), const((MXU_DIM, MXU_DIM)),
            const((1, D_SG)), const((1, D_SG)),
            const((N_SG_HEADS, WINDOW, WINDOW)), const((WINDOW, D_SG)),
            const((1, D_ATTN)), const((1, D_SG)),
            const((D_MIX, D_MODEL)), const((1, D_MODEL)),
        ],
        out_specs=[
            pl.BlockSpec((MIX_R, D_MODEL), lambda i, s: (i, 0)),
            pl.BlockSpec((MIX_R, D_MODEL), lambda i, s: (i, 0)),
        ],
        scratch_shapes=[pltpu.VMEM((MIX_R, D_ATTN), F32), pltpu.VMEM((MIX_R, D_MIX), BF16)],
    )
    return pl.pallas_call(
        _mixer_kernel,
        grid_spec=grid_spec,
        out_shape=[jax.ShapeDtypeStruct((t, D_MODEL), F32), jax.ShapeDtypeStruct((t, D_MODEL), F32)],
        compiler_params=_params("arbitrary"),
        name="mixer",
    )(sinks, proj, proj, x2, qg, kg, bd, row(sg_ln_g, D_SG), row(sg_ln_b, D_SG), w_spatial, bsp,
      row(attn_out_g, D_ATTN), row(sg_out_g, D_SG), w_out_bf, row(ln2_g, D_MODEL))


def _router_kernel(xn_ref, wr_ref, br_ref, idx_ref, gate_ref, rank_ref, cnt_ref, carry_ref):
    i = pl.program_id(0)
    tb = ROUTE_TB

    @pl.when(i == 0)
    def _():
        carry_ref[...] = jnp.zeros_like(carry_ref)

    def split(v):
        hi = v.astype(BF16)
        return hi, (v - hi.astype(F32)).astype(BF16)

    def nt(a, b):
        return lax.dot_general(a, b, (((1,), (1,)), ((), ())), preferred_element_type=F32)

    w_hi, w_lo = split(wr_ref[...])
    x_hi, x_lo = split(xn_ref[...])
    by_x_hi = nt(jnp.concatenate([w_hi, w_lo], axis=0), x_hi)
    logits = by_x_hi[:N_EXPERTS] + (nt(w_hi, x_lo) + by_x_hi[N_EXPERTS:]) + br_ref[...]
    eid = lax.broadcasted_iota(I32, (N_EXPERTS, tb), 0)
    work = logits
    vals, idxs = [], []
    for _ in range(TOP_K):
        m = jnp.max(work, axis=0, keepdims=True)
        sel = jnp.min(jnp.where(work == m, eid, N_EXPERTS), axis=0, keepdims=True)
        vals.append(m)
        idxs.append(sel)
        work = jnp.where(eid == sel, -jnp.inf, work)
    ex = [jnp.exp(v - vals[0]) for v in vals]
    den = ex[0] + ex[1] + ex[2] + ex[3]
    chosen = (eid == idxs[0]) | (eid == idxs[1]) | (eid == idxs[2]) | (eid == idxs[3])
    onehot = jnp.where(chosen, 1.0, 0.0)

    tr = lax.broadcasted_iota(I32, (tb, tb), 0)
    tc = lax.broadcasted_iota(I32, (tb, tb), 1)
    before = jnp.where(tr < tc, 1.0, 0.0).astype(BF16)
    cnt = jnp.dot(onehot.astype(BF16), before, preferred_element_type=F32) + carry_ref[:, :1]
    for k in range(TOP_K):
        idx_ref[k:k + 1, :] = idxs[k]
        gate_ref[k:k + 1, :] = ex[k] / den
        rank_ref[k:k + 1, :] = jnp.sum(jnp.where(eid == idxs[k], cnt, 0.0), axis=0, keepdims=True).astype(I32)
    carry_ref[...] = carry_ref[...] + jnp.sum(onehot, axis=1, keepdims=True)
    cnt_ref[...] = carry_ref[...].astype(I32)


def _router(xn2, w_router, b_router):
    t = xn2.shape[0]
    tb = ROUTE_TB
    return pl.pallas_call(
        _router_kernel,
        grid=(t // tb,),
        in_specs=[
            pl.BlockSpec((tb, D_MODEL), lambda i: (i, 0)),
            pl.BlockSpec((N_EXPERTS, D_MODEL), lambda i: (0, 0)),
            pl.BlockSpec((N_EXPERTS, 1), lambda i: (0, 0)),
        ],
        out_specs=[
            pl.BlockSpec((TOP_K, tb), lambda i: (0, i)),
            pl.BlockSpec((TOP_K, tb), lambda i: (0, i)),
            pl.BlockSpec((TOP_K, tb), lambda i: (0, i)),
            pl.BlockSpec((N_EXPERTS, LANES), lambda i: (0, 0)),
        ],
        out_shape=[
            jax.ShapeDtypeStruct((TOP_K, t), I32),
            jax.ShapeDtypeStruct((TOP_K, t), F32),
            jax.ShapeDtypeStruct((TOP_K, t), I32),
            jax.ShapeDtypeStruct((N_EXPERTS, LANES), I32),
        ],
        scratch_shapes=[pltpu.VMEM((N_EXPERTS, LANES), F32)],
        compiler_params=_params("arbitrary"),
        name="router",
    )(xn2, w_router.T, b_router.reshape(N_EXPERTS, 1))


def _row_copy(src, s, dst, d, sem):
    return pltpu.make_async_copy(src.at[pl.ds(s, 1)], dst.at[pl.ds(d, 1)], sem)


def _tile_copy(src, dst, d8, sem):
    return pltpu.make_async_copy(src, dst.at[pl.ds(pl.multiple_of(d8 * SUBLANES, SUBLANES), SUBLANES)], sem)


def _dispatch_kernel(dest_ref, zmeta_ref, xn_ref, xs_ref, zero_ref, sem, zsem, *, t):
    i = pl.program_id(0)
    tb = ROW_TB

    @pl.when(i == 0)
    def _():
        zero_ref[...] = jnp.zeros_like(zero_ref)

        def pads(start):
            def per_expert(e, c):
                def per_row(r, c2):
                    cp = _row_copy(zero_ref, 0, xs_ref, zmeta_ref[e] + r, zsem)
                    cp.start() if start else cp.wait()
                    return c2
                return lax.fori_loop(0, zmeta_ref[N_EXPERTS + e], per_row, c)
            lax.fori_loop(0, N_EXPERTS, per_expert, 0)

            def per_tile(r, c):
                cp = _tile_copy(zero_ref, xs_ref, zmeta_ref[2 * N_EXPERTS] + r, zsem)
                cp.start() if start else cp.wait()
                return c
            lax.fori_loop(0, zmeta_ref[2 * N_EXPERTS + 1], per_tile, 0)

        pads(True)
        pads(False)

    def rows(start):
        def per_group(g, c):
            for u in range(SUBLANES):
                for k in range(TOP_K):
                    d = dest_ref[k * t + i * tb + g * SUBLANES + u]
                    cp = _row_copy(xn_ref.at[g], u, xs_ref, d, sem)
                    cp.start(priority=k % 2) if start else cp.wait()
            return c
        lax.fori_loop(0, tb // SUBLANES, per_group, 0)
    rows(True)
    rows(False)


def _dispatch(dest_flat, zmeta, xn2, n_rows):
    t = xn2.shape[0]
    grid_spec = pltpu.PrefetchScalarGridSpec(
        num_scalar_prefetch=2,
        grid=(t // ROW_TB,),
        in_specs=[pl.BlockSpec((ROW_TB // SUBLANES, SUBLANES, D_MODEL), lambda i, d, z: (i, 0, 0))],
        out_specs=pl.BlockSpec(memory_space=pl.ANY),
        scratch_shapes=[pltpu.VMEM((SUBLANES, D_MODEL), F32), pltpu.SemaphoreType.DMA, pltpu.SemaphoreType.DMA],
    )
    return pl.pallas_call(
        functools.partial(_dispatch_kernel, t=t),
        grid_spec=grid_spec,
        out_shape=jax.ShapeDtypeStruct((n_rows, D_MODEL), F32),
        compiler_params=_params("arbitrary"),
        name="dispatch",
    )(dest_flat, zmeta, xn2.reshape(t // SUBLANES, SUBLANES, D_MODEL))


def _moe_kernel(ie_ref, ist_ref, ins_ref, isz_ref, meta_ref, xs_ref, wgu_hbm, bgu_ref, wd_hbm, bd_ref, ys_ref,
                rawg0_ref, rawg1_ref, wgu0_ref, wgu1_ref, rawd0_ref, rawd1_ref, wdm0_ref, wdm1_ref,
                acc_ref, xbf_ref, xst_ref, ha0_ref, hb0_ref, ha1_ref, hb1_ref,
                zero_ref, gsem, dsem, xsem, osem, zsem, *, n_items, n_chunks):
    h_refs = ((ha0_ref, hb0_ref), (ha1_ref, hb1_ref))
    rawg_ref, wgu_bf_ref = (rawg0_ref, rawg1_ref), (wgu0_ref, wgu1_ref)
    rawd_ref, wdm_ref = (rawd0_ref, rawd1_ref), (wdm0_ref, wdm1_ref)
    it = pl.program_id(0)
    nsub = ins_ref[it]
    start = ist_ref[it]
    total = meta_ref[2]
    slot_acc = it & 1
    f = MOE_F
    half = f // 2

    def by_size(item, fn):
        for v, sub in enumerate(MOE_SUBS):
            @pl.when(isz_ref[item] == v)
            def _():
                fn(sub)

    def x_copy(item_start, s, slot, sub):
        r = pl.multiple_of(item_start + s * sub, SUBLANES)
        return pltpu.make_async_copy(xs_ref.at[pl.ds(r, sub)], xst_ref.at[slot, pl.ds(0, sub)], xsem.at[slot])

    def x_prefetch(item, sub):
        for s in range(2):
            @pl.when(s < ins_ref[item])
            def _():
                x_copy(ist_ref[item], s, s, sub).start()

    def y_copies(item, start_dma):
        def sized(sub):
            def one(s, c):
                r = pl.multiple_of(ist_ref[item] + s * sub, SUBLANES)
                cp = pltpu.make_async_copy(acc_ref.at[item & 1, pl.ds(s * sub, sub)],
                                           ys_ref.at[pl.ds(r, sub)], osem.at[item & 1])
                cp.start() if start_dma else cp.wait()
                return c
            lax.fori_loop(0, ins_ref[item], one, 0)
        by_size(item, sized)

    def w_copies(item, j, slot):
        e = ie_ref[item]
        cg = pl.multiple_of(j * 2 * f, 2 * f)
        cd = pl.multiple_of(j * f, f)
        return (pltpu.make_async_copy(wgu_hbm.at[e, :, pl.ds(cg, 2 * f)], rawg_ref[slot], gsem.at[slot]),
                pltpu.make_async_copy(wd_hbm.at[e, pl.ds(cd, f), :], rawd_ref[slot], dsem.at[slot]))

    def interleave_down(slot):
        for c in range(D_MODEL // LANES):
            cs = slice(c * LANES, (c + 1) * LANES)
            wdm_ref[slot][c, pl.ds(0, half, stride=2), :] = rawd_ref[slot][:half, cs]
            wdm_ref[slot][c, pl.ds(1, half, stride=2), :] = rawd_ref[slot][half:, cs]

    @pl.when(it == 0)
    def _():
        for c in range(2):
            for cp in w_copies(0, c, c):
                cp.start()
        for cp in w_copies(0, 0, 0):
            cp.wait()
        interleave_down(0)
        by_size(0, lambda sub: x_prefetch(0, sub))

    def item_body(sub):
        def stage(s, c):
            slot = s & 1
            r = pl.multiple_of(s * sub, ROW_PACK)
            x_copy(start, s, slot, sub).wait()
            xbf_ref[pl.ds(r, sub), :] = xst_ref[slot, :sub, :].astype(BF16)
            acc_ref[slot_acc, pl.ds(r, sub), :] = jnp.broadcast_to(bd_ref[0], (sub, D_MODEL))

            @pl.when(s + 2 < nsub)
            def _():
                x_copy(start, s + 2, slot, sub).start()
            return c
        lax.fori_loop(0, nsub, stage, 0)

        nxt = jnp.minimum(it + 1, n_items - 1)

        @pl.when(it + 1 < n_items)
        def _():
            by_size(nxt, lambda nsz: x_prefetch(nxt, nsz))

        even = (lax.broadcasted_iota(I32, (sub, f), 1) & 1) == 0

        def up_first(h_ref, w):
            wbf = rawg_ref[w][...].astype(BF16)
            wgu_bf_ref[w][...] = wbf
            h_ref[:sub, :] = jnp.dot(xbf_ref[pl.ds(0, sub), :], wbf, preferred_element_type=F32)

        def up_proj(s, h_ref, w):
            r = pl.multiple_of(s * sub, ROW_PACK)
            h_ref[:sub, :] = jnp.dot(xbf_ref[pl.ds(r, sub), :], wgu_bf_ref[w][...], preferred_element_type=F32)

        def down_proj(s, h_ref, w, j):
            r = pl.multiple_of(s * sub, ROW_PACK)
            h = h_ref[:sub, :] + bgu_ref[0, pl.ds(j, 1), :]
            gate = jnp.minimum(h, SWIGLU_LIMIT)
            glu = gate * jax.nn.sigmoid(gate * SWIGLU_ALPHA)
            up = jnp.clip(h, -SWIGLU_LIMIT, SWIGLU_LIMIT) + 1.0
            prod = glu * pltpu.roll(up, 2 * f - 1, axis=1)
            act = jnp.where(even, prod[:, :f], pltpu.roll(prod, 1, axis=1)[:, f:])
            wd = jnp.concatenate([wdm_ref[w][c] for c in range(D_MODEL // LANES)], axis=1).astype(BF16)
            acc_ref[slot_acc, pl.ds(r, sub), :] += jnp.dot(act.astype(BF16), wd, preferred_element_type=F32)

        def chunk_pass(j, w, can_be_last):
            j2 = j + 2
            it2 = it + jnp.where(j2 >= n_chunks, 1, 0)

            @pl.when(it2 < total)
            def _():
                for cp in w_copies(jnp.minimum(it2, n_items - 1), j2 & (n_chunks - 1), w):
                    cp.start()

            j1 = j + 1
            it1 = it + jnp.where(j1 >= n_chunks, 1, 0)

            @pl.when(it1 < total)
            def _():
                for cp in w_copies(jnp.minimum(it1, n_items - 1), j1 & (n_chunks - 1), 1 - w):
                    cp.wait()

            interleave_down(1 - w)

            ha_ref, hb_ref = h_refs[w]
            ha_next = h_refs[1 - w][0]
            n_pairs = (nsub - 1) // 2

            def pair(i, c):
                s = 2 * i
                up_proj(s + 1, hb_ref, w)
                down_proj(s, ha_ref, w, j)
                up_proj(s + 2, ha_ref, w)
                down_proj(s + 1, hb_ref, w, j)
                return c
            lax.fori_loop(0, n_pairs, pair, 0)
            s = 2 * n_pairs

            def tail(with_next):
                @pl.when(s == nsub - 1)
                def _():
                    if with_next:
                        up_first(ha_next, 1 - w)
                    down_proj(s, ha_ref, w, j)

                @pl.when(s < nsub - 1)
                def _():
                    up_proj(s + 1, hb_ref, w)
                    down_proj(s, ha_ref, w, j)
                    if with_next:
                        up_first(ha_next, 1 - w)
                    down_proj(s + 1, hb_ref, w, j)

            if can_be_last:
                @pl.when(j < n_chunks - 1)
                def _():
                    tail(True)

                @pl.when(j == n_chunks - 1)
                def _():
                    tail(False)
            else:
                tail(True)

        up_first(h_refs[0][0], 0)

        def chunk_pair(p, c):
            chunk_pass(2 * p, 0, False)
            chunk_pass(2 * p + 1, 1, True)
            return c
        lax.fori_loop(0, n_chunks // 2, chunk_pair, 0)

    @pl.when(nsub > 0)
    def _():
        by_size(it, item_body)

    @pl.when(it > 0)
    def _():
        y_copies(it - 1, False)

    y_copies(it, True)

    @pl.when(it == n_items - 1)
    def _():
        y_copies(it, False)

        zero_ref[...] = jnp.zeros_like(zero_ref)

        def fill(start_dma):
            def per_tile(r, c):
                cp = _tile_copy(zero_ref, ys_ref, meta_ref[0] + r, zsem)
                cp.start() if start_dma else cp.wait()
                return c
            lax.fori_loop(0, meta_ref[1], per_tile, 0)
        fill(True)
        fill(False)


def _moe(item_e, item_start, item_nsub, item_size, meta, xs, w_gate_up, b_gate_up, w_down, b_down):
    n_items = item_e.shape[0]
    n_rows = xs.shape[0]
    n_chunks = D_FF // MOE_F
    grid_spec = pltpu.PrefetchScalarGridSpec(
        num_scalar_prefetch=5,
        grid=(n_items,),
        in_specs=[
            pl.BlockSpec(memory_space=pl.ANY),
            pl.BlockSpec(memory_space=pl.ANY),
            pl.BlockSpec((1, n_chunks, 2 * MOE_F), lambda it, ie, ist, ins, isz, mt: (ie[it], 0, 0)),
            pl.BlockSpec(memory_space=pl.ANY),
            pl.BlockSpec((1, 1, D_MODEL), lambda it, ie, ist, ins, isz, mt: (ie[it], 0, 0)),
        ],
        out_specs=pl.BlockSpec(memory_space=pl.ANY),
        scratch_shapes=[
            pltpu.VMEM((D_MODEL, 2 * MOE_F), F32), pltpu.VMEM((D_MODEL, 2 * MOE_F), F32),
            pltpu.VMEM((D_MODEL, 2 * MOE_F), BF16), pltpu.VMEM((D_MODEL, 2 * MOE_F), BF16),
            pltpu.VMEM((MOE_F, D_MODEL), F32), pltpu.VMEM((MOE_F, D_MODEL), F32),
            pltpu.VMEM((D_MODEL // LANES, MOE_F, LANES), F32), pltpu.VMEM((D_MODEL // LANES, MOE_F, LANES), F32),
            pltpu.VMEM((2, MOE_BM, D_MODEL), F32),
            pltpu.VMEM((MOE_BM, D_MODEL), BF16),
            pltpu.VMEM((2, MOE_SUB, D_MODEL), F32),
            pltpu.VMEM((MOE_SUB, 2 * MOE_F), F32), pltpu.VMEM((MOE_SUB, 2 * MOE_F), F32),
            pltpu.VMEM((MOE_SUB, 2 * MOE_F), F32), pltpu.VMEM((MOE_SUB, 2 * MOE_F), F32),
            pltpu.VMEM((SUBLANES, D_MODEL), F32),
            pltpu.SemaphoreType.DMA((2,)),
            pltpu.SemaphoreType.DMA((2,)),
            pltpu.SemaphoreType.DMA((2,)),
            pltpu.SemaphoreType.DMA((2,)),
            pltpu.SemaphoreType.DMA,
        ],
    )
    return pl.pallas_call(
        functools.partial(_moe_kernel, n_items=n_items, n_chunks=n_chunks),
        grid_spec=grid_spec,
        out_shape=jax.ShapeDtypeStruct((n_rows, D_MODEL), F32),
        compiler_params=_params("arbitrary"),
        name="moe",
    )(item_e, item_start, item_nsub, item_size, meta, xs, w_gate_up,
      b_gate_up.reshape(N_EXPERTS, n_chunks, 2 * MOE_F), w_down, b_down.reshape(N_EXPERTS, 1, D_MODEL))


def _combine_kernel(dest_ref, ys_ref, h_ref, gate_ref, p_ref, wple_ref, pg_ref, wpg_ref, o_ref, ybuf_ref, sem, *, t):
    i = pl.program_id(0)
    n = pl.num_programs(0)
    tb = ROW_TB
    slot = i & 1

    def gather(blk, sl, start):
        def per_group(g, c):
            for u in range(SUBLANES):
                for k in range(TOP_K):
                    d = dest_ref[k * t + blk * tb + g * SUBLANES + u]
                    cp = _row_copy(ys_ref, d, ybuf_ref.at[sl, k, g], u, sem.at[sl])
                    cp.start() if start else cp.wait()
            return c
        lax.fori_loop(0, tb // SUBLANES, per_group, 0)

    def on_slot(cond, blk, rel, start):
        for sl in range(2):
            @pl.when(cond & (slot == (sl ^ rel)))
            def _():
                gather(blk, sl, start)

    on_slot(i == 0, 0, 0, True)
    on_slot(i + 1 < n, i + 1, 1, True)
    on_slot(i >= 0, i, 0, False)

    gates = gate_ref[...]
    h2 = h_ref[...]
    for k in range(TOP_K):
        h2 = h2 + ybuf_ref[slot, k].reshape(tb, D_MODEL) * gates[:, k:k + 1]
    gate = jax.nn.sigmoid(jnp.dot(_rms(h2, pg_ref[...]).astype(BF16), wpg_ref[...].astype(BF16),
                                  preferred_element_type=F32))
    o_ref[...] = h2 + jnp.dot(p_ref[...].astype(BF16), wple_ref[...].astype(BF16),
                              preferred_element_type=F32) * gate


def _combine(dest_flat, ys, h, gates_tk, p2, w_ple, ple_norm_g, w_pg):
    t = h.shape[0]
    tb = ROW_TB
    grid_spec = pltpu.PrefetchScalarGridSpec(
        num_scalar_prefetch=1,
        grid=(t // tb,),
        in_specs=[
            pl.BlockSpec(memory_space=pl.ANY),
            pl.BlockSpec((tb, D_MODEL), lambda i, d: (i, 0)),
            pl.BlockSpec((tb, TOP_K), lambda i, d: (i, 0)),
            pl.BlockSpec((tb, D_PLE), lambda i, d: (i, 0)),
            pl.BlockSpec((D_PLE, D_MODEL), lambda i, d: (0, 0), pipeline_mode=pl.Buffered(1)),
            pl.BlockSpec((1, D_MODEL), lambda i, d: (0, 0)),
            pl.BlockSpec((D_MODEL, D_MODEL), lambda i, d: (0, 0), pipeline_mode=pl.Buffered(1)),
        ],
        out_specs=pl.BlockSpec((tb, D_MODEL), lambda i, d: (i, 0)),
        scratch_shapes=[pltpu.VMEM((2, TOP_K, tb // SUBLANES, SUBLANES, D_MODEL), F32),
                        pltpu.SemaphoreType.DMA((2,))],
    )
    return pl.pallas_call(
        functools.partial(_combine_kernel, t=t),
        grid_spec=grid_spec,
        out_shape=jax.ShapeDtypeStruct((t, D_MODEL), F32),
        compiler_params=_params("arbitrary"),
        name="combine",
    )(dest_flat, ys, h, gates_tk, p2, w_ple, ple_norm_g.reshape(1, D_MODEL), w_pg)


def _plan(idx_t, rank_t, counts, t):
    tk = t * TOP_K
    n_items = tk // MOE_BM + N_EXPERTS
    n_rows = tk + N_EXPERTS * ROW_ALIGN + MOE_SUB
    seg = (counts + ROW_ALIGN - 1) // ROW_ALIGN * ROW_ALIGN
    seg_end = jnp.cumsum(seg)
    seg_start = seg_end - seg
    eids = jnp.arange(N_EXPERTS, dtype=I32)[:, None, None]
    first = jnp.sum(jnp.where(idx_t[None] == eids, seg_start[:, None, None], 0), axis=0)
    dest = (first + rank_t).reshape(-1).astype(I32)

    per_e = (counts + MOE_BM - 1) // MOE_BM
    it_end = jnp.cumsum(per_e)
    it_off = it_end - per_e
    total = it_end[-1]
    its = jnp.arange(n_items, dtype=I32)
    valid = its < total
    its_c = jnp.minimum(its, total - 1)
    e_of = jnp.minimum(jnp.sum(it_end[None, :] <= its_c[:, None], axis=1), N_EXPERTS - 1).astype(I32)
    pick = e_of[:, None] == jnp.arange(N_EXPERTS, dtype=I32)[None, :]
    of_item = lambda v: jnp.sum(jnp.where(pick, v[None, :], 0), axis=1)
    li = its_c - of_item(it_off)
    rows = jnp.clip(of_item(counts) - li * MOE_BM, 0, MOE_BM)
    nsub_by_size = [(rows + sz - 1) // sz for sz in MOE_SUBS]
    size_id = jnp.zeros_like(rows)
    for v in range(1, len(MOE_SUBS)):
        size_id = jnp.where(nsub_by_size[v] <= nsub_by_size[0], v, size_id)
    sub_rows = jnp.zeros_like(rows)
    for v, sz in enumerate(MOE_SUBS):
        sub_rows = jnp.where(size_id == v, sz, sub_rows)
    nsub = jnp.where(valid, (rows + sub_rows - 1) // sub_rows, 0).astype(I32)
    start = (of_item(seg_start) + li * MOE_BM).astype(I32)

    last = its == total - 1
    used_end = jnp.sum(jnp.where(last, start + sub_rows * nsub, 0))
    moe_meta = jnp.stack([used_end // SUBLANES, (n_rows - used_end) // SUBLANES, total]).astype(I32)
    zmeta = jnp.concatenate([
        seg_start + counts, seg - counts,
        jnp.stack([seg_end[-1] // SUBLANES, (n_rows - seg_end[-1]) // SUBLANES]),
    ]).astype(I32)
    return dest, zmeta, e_of, start, nsub, size_id.astype(I32), moe_meta, n_rows


def kernel(x, p, ln1_g, w_in, q_norm_g, k_norm_g, sinks, sg_ln_g, sg_ln_b, w_spatial, b_spatial, attn_out_g,
           sg_out_g, w_out, ln2_g, w_router, b_router, w_gate_up, b_gate_up, w_down, b_down, w_ple, ple_norm_g,
           w_ple_gate):
    bsz, s, d = x.shape
    assert bsz == 1 and d == D_MODEL and p.shape[0] == 1
    t = bsz * s
    h = x.reshape(t, d)
    li = 0
    proj = _in_proj(h, ln1_g[li], w_in[li])
    h, xn2 = _mixer(proj, h, sinks[li], q_norm_g[li], k_norm_g[li], sg_ln_g[li], sg_ln_b[li], w_spatial[li],
                    b_spatial[li], attn_out_g[li], sg_out_g[li], w_out[li].astype(BF16), ln2_g[li])
    idx_t, gate_t, rank_t, cnt = _router(xn2, w_router[li], b_router[li])
    dest, zmeta, item_e, item_start, item_nsub, item_size, moe_meta, n_rows = _plan(idx_t, rank_t, cnt[:, 0], t)
    xs = _dispatch(dest, zmeta, xn2, n_rows)
    ys = _moe(item_e, item_start, item_nsub, item_size, moe_meta, xs, w_gate_up[li], b_gate_up[li], w_down[li],
              b_down[li])
    out = _combine(dest, ys, h, gate_t.T, p[li].reshape(t, D_PLE), w_ple[li], ple_norm_g[li], w_ple_gate[li])
    return out.reshape(bsz, s, d)
```
